```python
import math
import jax
import jax.numpy as jnp
from jax import lax
import numpy as np

D_MODEL = 2048
BATCH = 8
SEQ = 2048
DEPTH = 2

N_MIXERS = 2
N_EVEN = (DEPTH + 1) // 2
N_ODD = DEPTH // 2
CONV_WIDTH = 3
N_HEADS = 16
HEAD_DIM = D_MODEL // N_HEADS
MOBA_BLOCK = 256
MOBA_TOPK = 3
Q_CHUNK = 16
ROPE_THETA = 10000.0
D_FF_DENSE = 5632
N_EXPERTS = 8
EXPERT_TOPK = 2
D_FF_EXPERT = 7168
EXPERT_ROW_BLOCK = 512
NORM_EPS = 1e-6
NEG_INF = -1e30
MOD_STD = 0.2

kernel_name = "hybrid_shortconv_moba_moe_adaln"


def rms_norm(x, gain):
    xf = x.astype(jnp.float32)
    y = xf * lax.rsqrt(jnp.mean(xf * xf, axis=-1, keepdims=True) + NORM_EPS)
    return (y * gain.astype(jnp.float32)).astype(x.dtype)


def modulate(h, shift, scale):
    return h * (1.0 + scale[:, None, :]) + shift[:, None, :]


def swiglu(h, w_gate, w_up, w_down):
    return (jax.nn.silu(h @ w_gate) * (h @ w_up)) @ w_down


def rope(x, positions):
    half = HEAD_DIM // 2
    inv_freq = jnp.exp(-math.log(ROPE_THETA) * jnp.arange(half, dtype=jnp.float32) / half)
    ang = positions.astype(jnp.float32)[:, None] * inv_freq[None, :]
    cos, sin = jnp.cos(ang), jnp.sin(ang)
    xf = x.astype(jnp.float32)
    x1, x2 = xf[..., :half], xf[..., half:]
    return jnp.concatenate([x1 * cos - x2 * sin, x2 * cos + x1 * sin], axis=-1).astype(x.dtype)


def short_conv_mixer(h, w_in, conv_w, w_out):
    b_gate, c_gate, v = jnp.split(h @ w_in, 3, axis=-1)
    u = c_gate * v
    u = lax.conv_general_dilated(
        u, conv_w[:, None, :].astype(u.dtype), window_strides=(1,),
        padding=[(CONV_WIDTH - 1, 0)], dimension_numbers=("NWC", "WIO", "NWC"),
        feature_group_count=u.shape[-1])
    return (b_gate * u) @ w_out


def moba_attention(h, w_qkv, q_gain, k_gain, w_out):
    B, S, _ = h.shape
    qkv = (h @ w_qkv).reshape(B, S, 3, N_HEADS, HEAD_DIM)
    q, k, v = (jnp.transpose(qkv[:, :, i], (0, 2, 1, 3)) for i in range(3))
    positions = jnp.arange(S)
    q = rope(rms_norm(q, q_gain), positions)
    k = rope(rms_norm(k, k_gain), positions)
    n_blk = -(-S // MOBA_BLOCK)
    pad = n_blk * MOBA_BLOCK - S
    kb = jnp.pad(k, ((0, 0), (0, 0), (0, pad), (0, 0))).reshape(B, N_HEADS, n_blk, MOBA_BLOCK, HEAD_DIM)
    vb = jnp.pad(v, ((0, 0), (0, 0), (0, pad), (0, 0))).reshape(B, N_HEADS, n_blk, MOBA_BLOCK, HEAD_DIM)
    k_mean = jnp.mean(kb.astype(jnp.float32), axis=3)
    n_sel = min(MOBA_TOPK, n_blk - 1)
    scale = HEAD_DIM ** -0.5
    n_chunks = S // Q_CHUNK
    q_chunks = q.reshape(B, N_HEADS, n_chunks, Q_CHUNK, HEAD_DIM).transpose(2, 0, 1, 3, 4)
    b_idx = jnp.arange(B)[:, None, None, None]
    h_idx = jnp.arange(N_HEADS)[None, :, None, None]
    blk_ids = jnp.arange(n_blk)
    key_off = jnp.arange(MOBA_BLOCK)

    def attend_chunk(args):
        q_c, c_idx = args
        q_pos = c_idx * Q_CHUNK + jnp.arange(Q_CHUNK)
        own = (c_idx * Q_CHUNK) // MOBA_BLOCK
        k_own = lax.dynamic_index_in_dim(kb, own, axis=2, keepdims=False)
        v_own = lax.dynamic_index_in_dim(vb, own, axis=2, keepdims=False)
        s_own = jnp.einsum("bhqd,bhkd->bhqk", q_c, k_own).astype(jnp.float32) * scale
        s_own = jnp.where(own * MOBA_BLOCK + key_off[None, :] <= q_pos[:, None], s_own, NEG_INF)
        if n_sel == 0:
            p_own = jax.nn.softmax(s_own, axis=-1).astype(v_own.dtype)
            return jnp.einsum("bhqk,bhkd->bhqd", p_own, v_own)
        gate = jnp.einsum("bhqd,bhnd->bhqn", q_c.astype(jnp.float32), k_mean)
        gate = jnp.where(blk_ids < own, gate, NEG_INF)
        _, sel = lax.top_k(gate, n_sel)
        sel_ok = sel < own
        k_sel = kb[b_idx, h_idx, sel]
        v_sel = vb[b_idx, h_idx, sel]
        s_sel = jnp.einsum("bhqd,bhqnkd->bhqnk", q_c, k_sel).astype(jnp.float32) * scale
        s_sel = jnp.where(sel_ok[..., None], s_sel, NEG_INF)
        s_all = jnp.concatenate([s_sel.reshape(B, N_HEADS, Q_CHUNK, n_sel * MOBA_BLOCK), s_own], axis=-1)
        p = jax.nn.softmax(s_all, axis=-1).astype(v_own.dtype)
        p_sel = p[..., :n_sel * MOBA_BLOCK].reshape(B, N_HEADS, Q_CHUNK, n_sel, MOBA_BLOCK)
        p_own = p[..., n_sel * MOBA_BLOCK:]
        return (jnp.einsum("bhqnk,bhqnkd->bhqd", p_sel, v_sel)
                + jnp.einsum("bhqk,bhkd->bhqd", p_own, v_own))

    o = lax.map(attend_chunk, (q_chunks, jnp.arange(n_chunks)))
    o = o.transpose(1, 0, 3, 2, 4).reshape(B, S, D_MODEL)
    return o @ w_out


def moe_swiglu(h, w_router, b_router, w_gate, w_up, w_down):
    B, S, D = h.shape
    tok = h.reshape(B * S, D)
    n_tok = B * S
    n_assign = n_tok * EXPERT_TOPK
    logits = (tok @ w_router).astype(jnp.float32) + b_router.astype(jnp.float32)
    top_logit, top_idx = lax.top_k(logits, EXPERT_TOPK)
    top_w = jax.nn.softmax(top_logit, axis=-1)
    flat_e = top_idx.reshape(-1)
    flat_tok = jnp.repeat(jnp.arange(n_tok, dtype=jnp.int32), EXPERT_TOPK)
    flat_w = top_w.reshape(-1)
    counts = jnp.bincount(flat_e, length=N_EXPERTS)
    padded = (counts + EXPERT_ROW_BLOCK - 1) // EXPERT_ROW_BLOCK * EXPERT_ROW_BLOCK
    pad_end = jnp.cumsum(padded)
    pad_start = pad_end - padded
    start = jnp.cumsum(counts) - counts
    order = jnp.argsort(flat_e)
    sorted_e = flat_e[order]
    dest = pad_start[sorted_e] + jnp.arange(n_assign) - start[sorted_e]
    n_blocks = -(-n_assign // EXPERT_ROW_BLOCK) + N_EXPERTS
    n_rows = n_blocks * EXPERT_ROW_BLOCK
    row_tok = jnp.zeros((n_rows,), jnp.int32).at[dest].set(flat_tok[order])
    row_w = jnp.zeros((n_rows,), jnp.float32).at[dest].set(flat_w[order])
    block_e = jnp.minimum(
        jnp.searchsorted(pad_end, jnp.arange(n_blocks) * EXPERT_ROW_BLOCK, side="right"),
        N_EXPERTS - 1)
    xs = tok[row_tok].reshape(n_blocks, EXPERT_ROW_BLOCK, D)

    def expert_block(args):
        x_b, e = args
        return swiglu(x_b, w_gate[e], w_up[e], w_down[e])

    ys = lax.map(expert_block, (xs, block_e)).reshape(n_rows, D)
    out = jnp.zeros_like(tok).at[row_tok].add(ys * row_w[:, None].astype(ys.dtype))
    return out.reshape(B, S, D)


def setup_inputs(seed: int = 0) -> dict:
    key = jax.random.key(seed)
    ks = jax.random.split(key, 24)
    D = D_MODEL

    def nrm(k, shape, std):
        return jax.random.normal(k, shape, jnp.float32) * std

    return {
        "x": nrm(ks[0], (BATCH, SEQ, D), 1.0),
        "c": nrm(ks[1], (BATCH, D), 1.0),
        "mod_w": nrm(ks[2], (DEPTH, D, 6 * D), MOD_STD * D ** -0.5),
        "mod_b": nrm(ks[3], (DEPTH, 6 * D), 0.02),
        "norm_mix": 1.0 + nrm(ks[4], (DEPTH, D), 0.05),
        "norm_ffn": 1.0 + nrm(ks[5], (DEPTH, D), 0.05),
        "conv_in": nrm(ks[6], (N_EVEN, D, 3 * D), D ** -0.5),
        "conv_w": nrm(ks[7], (N_EVEN, CONV_WIDTH, D), CONV_WIDTH ** -0.5),
        "conv_out": nrm(ks[8], (N_EVEN, D, D), D ** -0.5),
        "ffn_gate": nrm(ks[9], (N_EVEN, D, D_FF_DENSE), D ** -0.5),
        "ffn_up": nrm(ks[10], (N_EVEN, D, D_FF_DENSE), D ** -0.5),
        "ffn_down": nrm(ks[11], (N_EVEN, D_FF_DENSE, D), D_FF_DENSE ** -0.5),
        "qkv_w": nrm(ks[12], (N_ODD, D, 3 * D), D ** -0.5),
        "q_norm": 1.0 + nrm(ks[13], (N_ODD, HEAD_DIM), 0.05),
        "k_norm": 1.0 + nrm(ks[14], (N_ODD, HEAD_DIM), 0.05),
        "attn_out": nrm(ks[15], (N_ODD, D, D), D ** -0.5),
        "router_w": nrm(ks[16], (N_ODD, D, N_EXPERTS), D ** -0.5),
        "router_b": nrm(ks[17], (N_ODD, N_EXPERTS), 0.01),
        "exp_gate": nrm(ks[18], (N_ODD, N_EXPERTS, D, D_FF_EXPERT), D ** -0.5),
        "exp_up": nrm(ks[19], (N_ODD, N_EXPERTS, D, D_FF_EXPERT), D ** -0.5),
        "exp_down": nrm(ks[20], (N_ODD, N_EXPERTS, D_FF_EXPERT, D), D_FF_EXPERT ** -0.5),
    }


def reference(x, c, mod_w, mod_b, norm_mix, norm_ffn, conv_in, conv_w, conv_out,
              ffn_gate, ffn_up, ffn_down, qkv_w, q_norm, k_norm, attn_out,
              router_w, router_b, exp_gate, exp_up, exp_down):
    c_act = jax.nn.silu(c)
    for i in range(DEPTH):
        j = i // N_MIXERS
        mod = c_act @ mod_w[i] + mod_b[i]
        sh_m, sc_m, g_m, sh_f, sc_f, g_f = jnp.split(mod, 6, axis=-1)
        h = modulate(rms_norm(x, norm_mix[i]), sh_m, sc_m)
        if i % N_MIXERS == 0:
            y = short_conv_mixer(h, conv_in[j], conv_w[j], conv_out[j])
        else:
            y = moba_attention(h, qkv_w[j], q_norm[j], k_norm[j], attn_out[j])
        x = x + g_m[:, None, :] * y
        h = modulate(rms_norm(x, norm_ffn[i]), sh_f, sc_f)
        if i % 2 == 0:
            y = swiglu(h, ffn_gate[j], ffn_up[j], ffn_down[j])
        else:
            y = moe_swiglu(h, router_w[j], router_b[j], exp_gate[j], exp_up[j], exp_down[j])
        x = x + g_f[:, None, :] * y
    return x
```

```python
import functools
import math

import jax
import jax.numpy as jnp
from jax import lax
from jax.experimental import pallas as pl
from jax.experimental.pallas import tpu as pltpu

N_HEADS = 16
CONV_WIDTH = 3
MOBA_BLOCK = 256
MOBA_TOPK = 3
ROPE_THETA = 10000.0
EXPERT_TOPK = 2
EXPERT_ROWS = 512
NORM_EPS = 1e-6
NEG_INF = -1e30

LANES = 128
SUBLANES = 8
MIB = 1024 * 1024

F32 = jnp.float32
BF16 = jnp.bfloat16

_NT_DIMS = (((1,), (1,)), ((), ()))


def _params(semantics, vmem_mib):
    return pltpu.CompilerParams(dimension_semantics=semantics,
                                vmem_limit_bytes=vmem_mib * MIB)


def _dot(a, b):
    return jnp.dot(a, b, preferred_element_type=F32)


def _silu(a):
    return a * (1.0 / (1.0 + jnp.exp(-a)))


def _mod_kernel(c_ref, w_ref, b_ref, o_ref):
    ca = _silu(c_ref[...])
    o_ref[0] = _dot(ca.astype(BF16), w_ref[0].astype(BF16)) + b_ref[0]


def _mod_call(c, mod_w, mod_b, tn=1024):
    depth, d, n = mod_w.shape
    bsz = c.shape[0]
    return pl.pallas_call(
        _mod_kernel,
        grid=(depth, n // tn),
        in_specs=[
            pl.BlockSpec((bsz, d), lambda l, j: (0, 0)),
            pl.BlockSpec((1, d, tn), lambda l, j: (l, 0, j)),
            pl.BlockSpec((1, 1, tn), lambda l, j: (l, 0, j)),
        ],
        out_specs=pl.BlockSpec((1, bsz, tn), lambda l, j: (l, 0, j)),
        out_shape=jax.ShapeDtypeStruct((depth, bsz, n), F32),
        compiler_params=_params(("arbitrary", "arbitrary"), 40),
        name="adaln_mod",
    )(c, mod_w, mod_b.reshape(depth, 1, n))


def _normmod(x, gain, scale, shift):
    ms = jnp.mean(x * x, axis=-1, keepdims=True)
    y = x * lax.rsqrt(ms + NORM_EPS) * gain
    return y * (1.0 + scale) + shift


def _normmod_kernel(x_ref, gain_ref, sc_ref, sh_ref, o_ref):
    o_ref[...] = _normmod(x_ref[...], gain_ref[...], sc_ref[0], sh_ref[0]).astype(o_ref.dtype)


def _mod_spec(d, rows_per_batch, row, chunk):
    return pl.BlockSpec((1, 1, d), lambda i, *_: (row + i // rows_per_batch, 0, chunk))


def _normmod_call(x, gain, modr, row, sc_chunk, sh_chunk, seq, tm=512):
    t, d = x.shape
    rpb = seq // tm
    return pl.pallas_call(
        _normmod_kernel,
        grid=(t // tm,),
        in_specs=[
            pl.BlockSpec((tm, d), lambda i: (i, 0)),
            pl.BlockSpec((1, d), lambda i: (0, 0)),
            _mod_spec(d, rpb, row, sc_chunk),
            _mod_spec(d, rpb, row, sh_chunk),
        ],
        out_specs=pl.BlockSpec((tm, d), lambda i: (i, 0)),
        out_shape=jax.ShapeDtypeStruct((t, d), BF16),
        compiler_params=_params(("arbitrary",), 32),
        name="normmod",
    )(x, gain.reshape(1, d), modr, modr)


def _convin_kernel(h_ref, wb_ref, wc_ref, wv_ref, cw_ref, o_ref, u_scr):
    seq = h_ref.shape[0]
    h = h_ref[...]
    u = _dot(h, wc_ref[...]) * _dot(h, wv_ref[...])
    u_scr[0:SUBLANES, :] = jnp.zeros((SUBLANES, u.shape[1]), F32)
    u_scr[SUBLANES:, :] = u
    cw = cw_ref[...]
    conv = cw[CONV_WIDTH - 1:CONV_WIDTH, :] * u
    for tap in range(1, CONV_WIDTH):
        conv = conv + (cw[CONV_WIDTH - 1 - tap:CONV_WIDTH - tap, :]
                       * u_scr[SUBLANES - tap:SUBLANES - tap + seq, :])
    o_ref[...] = (_dot(h, wb_ref[...]) * conv).astype(o_ref.dtype)


def _convin_call(h, w_in, conv_w, seq, tn=256):
    t, d = h.shape
    nj = d // tn
    return pl.pallas_call(
        _convin_kernel,
        grid=(nj, t // seq),
        in_specs=[
            pl.BlockSpec((seq, d), lambda j, b: (b, 0)),
            pl.BlockSpec((d, tn), lambda j, b: (0, j)),
            pl.BlockSpec((d, tn), lambda j, b: (0, j + nj)),
            pl.BlockSpec((d, tn), lambda j, b: (0, j + 2 * nj)),
            pl.BlockSpec((CONV_WIDTH, tn), lambda j, b: (0, j)),
        ],
        out_specs=pl.BlockSpec((seq, tn), lambda j, b: (b, j)),
        out_shape=jax.ShapeDtypeStruct((t, d), BF16),
        scratch_shapes=[pltpu.VMEM((seq + SUBLANES, tn), F32)],
        compiler_params=_params(("arbitrary", "arbitrary"), 48),
        name="conv_in",
    )(h, w_in, w_in, w_in, conv_w)


def _resmm_kernel(a_ref, w_ref, x_ref, g_ref, o_ref):
    o_ref[...] = x_ref[...] + g_ref[0] * _dot(a_ref[...], w_ref[...])


def _resmm_call(a, w, x, modr, row, g_chunk, seq, tm=1024, tn=512):
    t, k = a.shape
    d = w.shape[1]
    rpb = seq // tm
    per_chunk = d // tn
    return pl.pallas_call(
        _resmm_kernel,
        grid=(d // tn, t // tm),
        in_specs=[
            pl.BlockSpec((tm, k), lambda j, i: (i, 0)),
            pl.BlockSpec((k, tn), lambda j, i: (0, j)),
            pl.BlockSpec((tm, tn), lambda j, i: (i, j)),
            pl.BlockSpec((1, 1, tn), lambda j, i: (row + i // rpb, 0, g_chunk * per_chunk + j)),
        ],
        out_specs=pl.BlockSpec((tm, tn), lambda j, i: (i, j)),
        out_shape=jax.ShapeDtypeStruct((t, d), F32),
        compiler_params=_params(("arbitrary", "arbitrary"), 40),
        name="res_matmul",
    )(a, w, x, modr)


def _ffn_kernel(h_ref, wg_ref, wu_ref, wd_ref, x_ref, g_ref, o_ref):
    j = pl.program_id(1)
    h = h_ref[...]
    act = _silu(_dot(h, wg_ref[...])) * _dot(h, wu_ref[...])
    part = _dot(act.astype(BF16), wd_ref[...])

    @pl.when(j == 0)
    def _():
        o_ref[...] = part

    @pl.when(j > 0)
    def _():
        o_ref[...] += part

    @pl.when(j == pl.num_programs(1) - 1)
    def _():
        o_ref[...] = x_ref[...] + g_ref[0] * o_ref[...]


def _ffn_call(h, wg, wu, wd, x, modr, row, g_chunk, seq, tm=512, tf=512):
    t, d = h.shape
    f = wg.shape[1]
    rpb = seq // tm
    return pl.pallas_call(
        _ffn_kernel,
        grid=(t // tm, f // tf),
        in_specs=[
            pl.BlockSpec((tm, d), lambda i, j: (i, 0)),
            pl.BlockSpec((d, tf), lambda i, j: (0, j)),
            pl.BlockSpec((d, tf), lambda i, j: (0, j)),
            pl.BlockSpec((tf, d), lambda i, j: (j, 0)),
            pl.BlockSpec((tm, d), lambda i, j: (i, 0)),
            pl.BlockSpec((1, 1, d), lambda i, j: (row + i // rpb, 0, g_chunk)),
        ],
        out_specs=pl.BlockSpec((tm, d), lambda i, j: (i, 0)),
        out_shape=jax.ShapeDtypeStruct((t, d), F32),
        compiler_params=_params(("arbitrary", "arbitrary"), 48),
        name="dense_ffn",
    )(h, wg, wu, wd, x, modr)


def _qk_kernel(h_ref, w_ref, gain_ref, cos_ref, sin_ref, o_ref, *, head_dim, out_scale):
    acc = _dot(h_ref[...], w_ref[...])
    gain = gain_ref[...]
    cos = cos_ref[...]
    sin = sin_ref[...]
    for hh in range(acc.shape[1] // head_dim):
        xh = acc[:, hh * head_dim:(hh + 1) * head_dim]
        ms = jnp.mean(xh * xh, axis=-1, keepdims=True)
        y = xh * lax.rsqrt(ms + NORM_EPS) * gain
        rot = pltpu.roll(y, head_dim // 2, axis=1)
        out = y * cos + rot * sin
        if out_scale != 1.0:
            out = out * out_scale
        o_ref[:, hh * head_dim:(hh + 1) * head_dim] = out.astype(o_ref.dtype)


def _v_kernel(h_ref, w_ref, o_ref):
    o_ref[...] = _dot(h_ref[...], w_ref[...]).astype(o_ref.dtype)


def _qkv_call(h, w_qkv, which, seq, head_dim, gain=None, cos=None, sin=None,
              out_scale=1.0, tm=1024, tn=512):
    t, d = h.shape
    nj = d // tn
    off = which * nj
    rpb = seq // tm
    h_spec = pl.BlockSpec((tm, d), lambda j, i: (i, 0))
    w_spec = pl.BlockSpec((d, tn), lambda j, i: (0, j + off))
    out_spec = pl.BlockSpec((tm, tn), lambda j, i: (i, j))
    out_shape = jax.ShapeDtypeStruct((t, d), BF16)
    cp = _params(("arbitrary", "arbitrary"), 40)
    if gain is None:
        return pl.pallas_call(_v_kernel, grid=(nj, t // tm), in_specs=[h_spec, w_spec],
                              out_specs=out_spec, out_shape=out_shape, compiler_params=cp,
                              name="v_proj")(h, w_qkv)
    tab_spec = pl.BlockSpec((tm, head_dim), lambda j, i: (i % rpb, 0))
    return pl.pallas_call(
        functools.partial(_qk_kernel, head_dim=head_dim, out_scale=out_scale),
        grid=(nj, t // tm),
        in_specs=[h_spec, w_spec, pl.BlockSpec((1, head_dim), lambda j, i: (0, 0)),
                  tab_spec, tab_spec],
        out_specs=out_spec, out_shape=out_shape, compiler_params=cp,
        name="qk_proj",
    )(h, w_qkv, gain.reshape(1, head_dim), cos, sin)


def _attn_kernel(q_ref, k_ref, v_ref, o_ref, vt_scr, sel_scr):
    seq, hd = q_ref.shape
    blk = MOBA_BLOCK
    nb = seq // blk

    rows = []
    for n in range(nb):
        vt_scr[n] = v_ref[n * blk:(n + 1) * blk, :].astype(F32).T.astype(BF16)
        kb = k_ref[n * blk:(n + 1) * blk, :].astype(F32)
        rows.append(jnp.sum(kb, axis=0, keepdims=True) * (1.0 / blk))
    km = jnp.concatenate(rows, axis=0)
    km_a = km.astype(BF16)
    rem = km - km_a.astype(F32)
    km_b = rem.astype(BF16)
    km_c = (rem - km_b.astype(F32)).astype(BF16)
    km_parts = jnp.concatenate([km_a, km_b, km_c, jnp.zeros_like(km_a)], axis=0)

    blk_id = lax.broadcasted_iota(jnp.int32, (nb, blk), 0)
    causal = (lax.broadcasted_iota(jnp.int32, (blk, blk), 0)
              <= lax.broadcasted_iota(jnp.int32, (blk, blk), 1))

    def q_block(qi, carry):
        q0 = pl.multiple_of(qi * blk, blk)
        qb = q_ref[pl.ds(q0, blk), :]
        g3 = lax.dot_general(km_parts, qb, _NT_DIMS, preferred_element_type=F32)
        gate = g3[0:nb] + g3[nb:2 * nb] + g3[2 * nb:3 * nb]
        gate = jnp.where(blk_id < qi, gate, NEG_INF)
        rank = jnp.zeros((nb, blk), F32)
        for m in range(nb):
            gm = gate[m:m + 1, :]
            beats = (gm > gate) | ((gm == gate) & (blk_id > m))
            rank = rank + beats.astype(F32)
        sel = (rank < MOBA_TOPK) & (blk_id < qi)
        sel_scr[...] = sel.astype(F32)

        s = lax.dot_general(k_ref[pl.ds(q0, blk), :], qb, _NT_DIMS, preferred_element_type=F32)
        s = jnp.where(causal, s, NEG_INF)
        m0 = jnp.max(s, axis=0, keepdims=True)
        p = jnp.exp(s - m0)
        l0 = jnp.sum(p, axis=0, keepdims=True)
        acc0 = _dot(vt_scr[qi], p.astype(BF16))

        def kv_block(n, c):
            m_run, l_run, acc = c
            k0 = pl.multiple_of(n * blk, blk)
            s = lax.dot_general(k_ref[pl.ds(k0, blk), :], qb, _NT_DIMS, preferred_element_type=F32)
            s = jnp.where(sel_scr[pl.ds(n, 1), :] > 0.0, s, NEG_INF)
            m_new = jnp.maximum(m_run, jnp.max(s, axis=0, keepdims=True))
            alpha = jnp.exp(m_run - m_new)
            p = jnp.exp(s - m_new)
            l_new = alpha * l_run + jnp.sum(p, axis=0, keepdims=True)
            acc_new = alpha * acc + _dot(vt_scr[n], p.astype(BF16))
            return m_new, l_new, acc_new

        _, l_fin, acc = lax.fori_loop(0, qi, kv_block, (m0, l0, acc0))
        out_t = acc * (1.0 / l_fin)
        o_ref[pl.ds(q0, blk), :] = out_t.T.astype(o_ref.dtype)
        return carry

    lax.fori_loop(0, nb, q_block, 0)


def _attn_call(q, k, v, seq, head_dim):
    t, d = q.shape
    nh = d // head_dim
    spec = pl.BlockSpec((seq, head_dim), lambda b, h: (b, h))
    return pl.pallas_call(
        _attn_kernel,
        grid=(t // seq, nh),
        in_specs=[spec, spec, spec],
        out_specs=spec,
        out_shape=jax.ShapeDtypeStruct((t, d), BF16),
        scratch_shapes=[pltpu.VMEM((seq // MOBA_BLOCK, head_dim, MOBA_BLOCK), BF16),
                        pltpu.VMEM((seq // MOBA_BLOCK, MOBA_BLOCK), F32)],
        compiler_params=_params(("arbitrary", "arbitrary"), 32),
        name="moba_attention",
    )(q, k, v)


def _router_kernel(x_ref, gain_ref, sc_ref, sh_ref, wr_ref, br_ref, h_ref, idx_ref, wt_ref,
                   *, n_experts):
    h = _normmod(x_ref[...], gain_ref[...], sc_ref[0], sh_ref[0])
    h_ref[...] = h
    h_a = h.astype(BF16)
    h_b = (h - h_a.astype(F32)).astype(BF16)
    w = wr_ref[...]
    w_a = w.astype(BF16)
    w_b = (w - w_a.astype(F32)).astype(BF16)
    logits = _dot(h_a, w_a) + _dot(h_b, w_a) + _dot(h_a, w_b) + br_ref[...]
    lane = lax.broadcasted_iota(jnp.int32, logits.shape, 1)
    logits = jnp.where(lane < n_experts, logits, -jnp.inf)
    m1 = jnp.max(logits, axis=-1, keepdims=True)
    i1 = jnp.min(jnp.where(logits == m1, lane, LANES), axis=-1, keepdims=True)
    rest = jnp.where(lane == i1, -jnp.inf, logits)
    m2 = jnp.max(rest, axis=-1, keepdims=True)
    i2 = jnp.min(jnp.where(rest == m2, lane, LANES), axis=-1, keepdims=True)
    e2 = jnp.exp(m2 - m1)
    w1 = 1.0 / (1.0 + e2)
    w2 = e2 * w1
    idx_ref[...] = jnp.where(lane == 0, i1, jnp.where(lane == 1, i2, 0))
    wt_ref[...] = jnp.where(lane == 0, w1, jnp.where(lane == 1, w2, 0.0))


def _router_call(x, gain, modr, row, sc_chunk, sh_chunk, w_router, b_router, seq, tm=512):
    t, d = x.shape
    n_experts = w_router.shape[1]
    rpb = seq // tm
    wr = jnp.zeros((d, LANES), F32).at[:, :n_experts].set(w_router)
    br = jnp.zeros((1, LANES), F32).at[0, :n_experts].set(b_router)
    return pl.pallas_call(
        functools.partial(_router_kernel, n_experts=n_experts),
        grid=(t // tm,),
        in_specs=[
            pl.BlockSpec((tm, d), lambda i: (i, 0)),
            pl.BlockSpec((1, d), lambda i: (0, 0)),
            _mod_spec(d, rpb, row, sc_chunk),
            _mod_spec(d, rpb, row, sh_chunk),
            pl.BlockSpec((d, LANES), lambda i: (0, 0)),
            pl.BlockSpec((1, LANES), lambda i: (0, 0)),
        ],
        out_specs=[
            pl.BlockSpec((tm, d), lambda i: (i, 0)),
            pl.BlockSpec((tm, LANES), lambda i: (i, 0)),
            pl.BlockSpec((tm, LANES), lambda i: (i, 0)),
        ],
        out_shape=[
            jax.ShapeDtypeStruct((t, d), F32),
            jax.ShapeDtypeStruct((t, LANES), jnp.int32),
            jax.ShapeDtypeStruct((t, LANES), F32),
        ],
        compiler_params=_params(("arbitrary",), 40),
        name="moe_router",
    )(x, gain.reshape(1, d), modr, modr, wr, br)


def _dispatch_copy(dest_ref, h_ref, xs_ref, sem, r, k):
    d = dest_ref[0, 0, EXPERT_TOPK * r + k]
    return pltpu.make_async_copy(h_ref.at[pl.ds(r, 1), :], xs_ref.at[pl.ds(d, 1), :], sem)


def _dispatch_kernel(dest_ref, h_ref, xs_in_ref, xs_ref, sem):
    del xs_in_ref
    tm = h_ref.shape[0]

    def issue(r, c):
        for k in range(EXPERT_TOPK):
            _dispatch_copy(dest_ref, h_ref, xs_ref, sem, r, k).start()
        return c

    def drain(r, c):
        for k in range(EXPERT_TOPK):
            _dispatch_copy(dest_ref, h_ref, xs_ref, sem, r, k).wait()
        return c

    lax.fori_loop(0, tm, issue, 0)
    lax.fori_loop(0, tm, drain, 0)


def _dispatch_call(h, dest, n_rows, tm=256):
    t, d = h.shape
    nt = t // tm
    return pl.pallas_call(
        _dispatch_kernel,
        grid=(nt,),
        in_specs=[
            pl.BlockSpec((1, 1, EXPERT_TOPK * tm), lambda i: (i, 0, 0), memory_space=pltpu.SMEM),
            pl.BlockSpec((tm, d), lambda i: (i, 0)),
            pl.BlockSpec(memory_space=pl.ANY),
        ],
        out_specs=pl.BlockSpec(memory_space=pl.ANY),
        out_shape=jax.ShapeDtypeStruct((n_rows, d), h.dtype),
        scratch_shapes=[pltpu.SemaphoreType.DMA],
        input_output_aliases={2: 0},
        compiler_params=_params(("arbitrary",), 32),
        name="moe_dispatch",
    )(dest.reshape(nt, 1, EXPERT_TOPK * tm), h, jnp.zeros((n_rows, d), h.dtype))


def _expert_kernel(bmap_ref, be_ref, nu_ref, xs_ref, wg_ref, wu_ref, wd_ref, ys_ref, xb_scr):
    del bmap_ref, be_ref
    i = pl.program_id(0)
    j = pl.program_id(1)

    @pl.when((i >= nu_ref[0]) & (j == 0))
    def _():
        ys_ref[...] = jnp.zeros(ys_ref.shape, F32)

    @pl.when(i < nu_ref[0])
    def _():
        @pl.when(j == 0)
        def _():
            xb_scr[...] = xs_ref[...].astype(BF16)
            ys_ref[...] = jnp.zeros(ys_ref.shape, F32)

        x = xb_scr[...]
        act = _silu(_dot(x, wg_ref[0])) * _dot(x, wu_ref[0])
        ys_ref[...] += _dot(act.astype(BF16), wd_ref[0])


def _expert_call(xs, blk_map, blk_e, n_used, wg, wu, wd, tm, tf=512):
    n_rows, d = xs.shape
    f = wg.shape[2]
    nf = f // tf

    def jj(i, j, nu):
        return jnp.where(i < nu[0], j, nf - 1)

    grid_spec = pltpu.PrefetchScalarGridSpec(
        num_scalar_prefetch=3,
        grid=(n_rows // tm, nf),
        in_specs=[
            pl.BlockSpec((tm, d), lambda i, j, bm, be, nu: (bm[i], 0)),
            pl.BlockSpec((1, d, tf), lambda i, j, bm, be, nu: (be[i], 0, jj(i, j, nu))),
            pl.BlockSpec((1, d, tf), lambda i, j, bm, be, nu: (be[i], 0, jj(i, j, nu))),
            pl.BlockSpec((1, tf, d), lambda i, j, bm, be, nu: (be[i], jj(i, j, nu), 0)),
        ],
        out_specs=pl.BlockSpec((tm, d), lambda i, j, bm, be, nu: (i, 0)),
        scratch_shapes=[pltpu.VMEM((tm, d), BF16)],
    )
    return pl.pallas_call(
        _expert_kernel,
        grid_spec=grid_spec,
        out_shape=jax.ShapeDtypeStruct((n_rows, d), F32),
        compiler_params=_params(("arbitrary", "arbitrary"), 48),
        name="moe_experts",
    )(blk_map, blk_e, n_used, xs, wg, wu, wd)


def _combine_copy(dest_ref, ys_ref, ybuf, sem, r, k):
    d = dest_ref[0, 0, EXPERT_TOPK * r + k]
    return pltpu.make_async_copy(ys_ref.at[pl.ds(d, 1), :], ybuf.at[k, pl.ds(r, 1), :], sem)


def _combine_kernel(dest_ref, ys_ref, x_ref, wt_ref, g_ref, o_ref, ybuf, sem):
    tm = x_ref.shape[0]

    def issue(r, c):
        for k in range(EXPERT_TOPK):
            _combine_copy(dest_ref, ys_ref, ybuf, sem, r, k).start()
        return c

    def drain(r, c):
        for k in range(EXPERT_TOPK):
            _combine_copy(dest_ref, ys_ref, ybuf, sem, r, k).wait()
        return c

    lax.fori_loop(0, tm, issue, 0)
    lax.fori_loop(0, tm, drain, 0)
    wt = wt_ref[...]
    y = wt[:, 0:1] * ybuf[0]
    for k in range(1, EXPERT_TOPK):
        y = y + wt[:, k:k + 1] * ybuf[k]
    o_ref[...] = x_ref[...] + g_ref[0] * y


def _combine_call(ys, dest, x, wt, modr, row, g_chunk, seq, tm=256):
    t, d = x.shape
    nt = t // tm
    rpb = seq // tm
    return pl.pallas_call(
        _combine_kernel,
        grid=(nt,),
        in_specs=[
            pl.BlockSpec((1, 1, EXPERT_TOPK * tm), lambda i: (i, 0, 0), memory_space=pltpu.SMEM),
            pl.BlockSpec(memory_space=pl.ANY),
            pl.BlockSpec((tm, d), lambda i: (i, 0)),
            pl.BlockSpec((tm, LANES), lambda i: (i, 0)),
            _mod_spec(d, rpb, row, g_chunk),
        ],
        out_specs=pl.BlockSpec((tm, d), lambda i: (i, 0)),
        out_shape=jax.ShapeDtypeStruct((t, d), F32),
        scratch_shapes=[pltpu.VMEM((EXPERT_TOPK, tm, d), F32), pltpu.SemaphoreType.DMA],
        compiler_params=_params(("arbitrary",), 32),
        name="moe_combine",
    )(dest.reshape(nt, 1, EXPERT_TOPK * tm), ys, x, wt, modr)


def _routing_tables(top_idx, n_experts, tm, n_blocks):
    flat_e = top_idx.reshape(-1)
    onehot = (flat_e[:, None] == jnp.arange(n_experts, dtype=jnp.int32)[None, :]).astype(jnp.int32)
    csum = jnp.cumsum(onehot, axis=0)
    counts = csum[-1]
    pos = jnp.sum((csum - 1) * onehot, axis=1)
    nblk = (counts + tm - 1) // tm
    blk_end = jnp.cumsum(nblk)
    blk_start = blk_end - nblk
    dest = (blk_start[flat_e] * tm + pos).astype(jnp.int32)
    n_used = blk_end[-1].astype(jnp.int32)
    blk_map = jnp.minimum(jnp.arange(n_blocks, dtype=jnp.int32), n_used - 1)
    blk_e = jnp.minimum(jnp.searchsorted(blk_end, blk_map, side="right"),
                        n_experts - 1).astype(jnp.int32)
    return dest, blk_map, blk_e, n_used.reshape(1)


def _rope_tables(seq, head_dim):
    half = head_dim // 2
    inv_freq = jnp.exp(-math.log(ROPE_THETA) * jnp.arange(half, dtype=F32) / half)
    ang = jnp.arange(seq).astype(F32)[:, None] * inv_freq[None, :]
    cos, sin = jnp.cos(ang), jnp.sin(ang)
    return jnp.concatenate([cos, cos], axis=-1), jnp.concatenate([-sin, sin], axis=-1)


def kernel(x, c, mod_w, mod_b, norm_mix, norm_ffn, conv_in, conv_w, conv_out, ffn_gate, ffn_up,
           ffn_down, qkv_w, q_norm, k_norm, attn_out, router_w, router_b, exp_gate, exp_up,
           exp_down):
    bsz, seq, d = x.shape
    depth = mod_w.shape[0]
    assert depth == 2, "layer 0 = short-conv + dense FFN, layer 1 = MoBA + MoE"
    t = bsz * seq
    head_dim = d // N_HEADS
    n_experts = router_w.shape[-1]

    mod = _mod_call(c, mod_w, mod_b)
    modr = mod.reshape(depth * bsz, 1, 6 * d)
    xt = x.reshape(t, d)

    h = _normmod_call(xt, norm_mix[0], modr, 0, 1, 0, seq)
    bu = _convin_call(h, conv_in[0].astype(BF16), conv_w[0], seq)
    xt = _resmm_call(bu, conv_out[0].astype(BF16), xt, modr, 0, 2, seq)
    h = _normmod_call(xt, norm_ffn[0], modr, 0, 4, 3, seq)
    xt = _ffn_call(h, ffn_gate[0].astype(BF16), ffn_up[0].astype(BF16),
                   ffn_down[0].astype(BF16), xt, modr, 0, 5, seq)

    row = bsz
    h = _normmod_call(xt, norm_mix[1], modr, row, 1, 0, seq)
    w_qkv = qkv_w[0].astype(BF16)
    cos, sin = _rope_tables(seq, head_dim)
    q = _qkv_call(h, w_qkv, 0, seq, head_dim, q_norm[0], cos, sin, out_scale=head_dim ** -0.5)
    k = _qkv_call(h, w_qkv, 1, seq, head_dim, k_norm[0], cos, sin)
    v = _qkv_call(h, w_qkv, 2, seq, head_dim)
    o = _attn_call(q, k, v, seq, head_dim)
    xt = _resmm_call(o, attn_out[0].astype(BF16), xt, modr, row, 2, seq)

    hf, top_idx, top_w = _router_call(xt, norm_ffn[1], modr, row, 4, 3,
                                      router_w[0], router_b[0], seq)
    n_blocks = (t * EXPERT_TOPK) // EXPERT_ROWS + n_experts
    dest, blk_map, blk_e, n_used = _routing_tables(top_idx[:, :EXPERT_TOPK], n_experts,
                                                   EXPERT_ROWS, n_blocks)
    xs = _dispatch_call(hf, dest, n_blocks * EXPERT_ROWS)
    ys = _expert_call(xs, blk_map, blk_e, n_used, exp_gate[0].astype(BF16),
                      exp_up[0].astype(BF16), exp_down[0].astype(BF16), EXPERT_ROWS)
    xt = _combine_call(ys, dest, xt, top_w, modr, row, 5, seq)
    return xt.reshape(bsz, seq, d)
```

```python
import functools
import math

import jax
import jax.numpy as jnp
from jax import lax
from jax.experimental import pallas as pl
from jax.experimental.pallas import tpu as pltpu

N_HEADS = 16
CONV_WIDTH = 3
MOBA_BLOCK = 256
MOBA_TOPK = 3
ROPE_THETA = 10000.0
EXPERT_TOPK = 2
EXPERT_ROWS = 512
NORM_EPS = 1e-6
NEG_INF = -1e30

LANES = 128
SUBLANES = 8
MIB = 1024 * 1024

F32 = jnp.float32
BF16 = jnp.bfloat16

_NT_DIMS = (((1,), (1,)), ((), ()))


def _params(semantics, vmem_mib):
    return pltpu.CompilerParams(dimension_semantics=semantics,
                                vmem_limit_bytes=vmem_mib * MIB)


def _dot(a, b):
    return jnp.dot(a, b, preferred_element_type=F32)


def _silu(a):
    return a * (1.0 / (1.0 + jnp.exp(-a)))


def _mod_kernel(c_ref, w_ref, b_ref, o_ref):
    ca = _silu(c_ref[...])
    o_ref[0] = _dot(ca.astype(BF16), w_ref[0].astype(BF16)) + b_ref[0]


def _mod_call(c, mod_w, mod_b, tn=1024):
    depth, d, n = mod_w.shape
    bsz = c.shape[0]
    return pl.pallas_call(
        _mod_kernel,
        grid=(depth, n // tn),
        in_specs=[
            pl.BlockSpec((bsz, d), lambda l, j: (0, 0)),
            pl.BlockSpec((1, d, tn), lambda l, j: (l, 0, j)),
            pl.BlockSpec((1, 1, tn), lambda l, j: (l, 0, j)),
        ],
        out_specs=pl.BlockSpec((1, bsz, tn), lambda l, j: (l, 0, j)),
        out_shape=jax.ShapeDtypeStruct((depth, bsz, n), F32),
        compiler_params=_params(("arbitrary", "arbitrary"), 40),
        name="adaln_mod",
    )(c, mod_w, mod_b.reshape(depth, 1, n))


def _normmod(x, gain, scale, shift):
    ms = jnp.mean(x * x, axis=-1, keepdims=True)
    y = x * lax.rsqrt(ms + NORM_EPS) * gain
    return y * (1.0 + scale) + shift


def _normmod_kernel(x_ref, gain_ref, sc_ref, sh_ref, o_ref):
    o_ref[...] = _normmod(x_ref[...], gain_ref[...], sc_ref[0], sh_ref[0]).astype(o_ref.dtype)


def _mod_spec(d, rows_per_batch, row, chunk):
    return pl.BlockSpec((1, 1, d), lambda i, *_: (row + i // rows_per_batch, 0, chunk))


def _normmod_call(x, gain, modr, row, sc_chunk, sh_chunk, seq, tm=512):
    t, d = x.shape
    rpb = seq // tm
    return pl.pallas_call(
        _normmod_kernel,
        grid=(t // tm,),
        in_specs=[
            pl.BlockSpec((tm, d), lambda i: (i, 0)),
            pl.BlockSpec((1, d), lambda i: (0, 0)),
            _mod_spec(d, rpb, row, sc_chunk),
            _mod_spec(d, rpb, row, sh_chunk),
        ],
        out_specs=pl.BlockSpec((tm, d), lambda i: (i, 0)),
        out_shape=jax.ShapeDtypeStruct((t, d), BF16),
        compiler_params=_params(("arbitrary",), 32),
        name="normmod",
    )(x, gain.reshape(1, d), modr, modr)


def _convin_kernel(h_ref, wb_ref, wc_ref, wv_ref, cw_ref, o_ref, u_scr):
    seq = h_ref.shape[0]
    h = h_ref[...]
    u = _dot(h, wc_ref[...]) * _dot(h, wv_ref[...])
    u_scr[0:SUBLANES, :] = jnp.zeros((SUBLANES, u.shape[1]), F32)
    u_scr[SUBLANES:, :] = u
    cw = cw_ref[...]
    conv = cw[CONV_WIDTH - 1:CONV_WIDTH, :] * u
    for tap in range(1, CONV_WIDTH):
        conv = conv + (cw[CONV_WIDTH - 1 - tap:CONV_WIDTH - tap, :]
                       * u_scr[SUBLANES - tap:SUBLANES - tap + seq, :])
    o_ref[...] = (_dot(h, wb_ref[...]) * conv).astype(o_ref.dtype)


def _convin_call(h, w_in, conv_w, seq, tn=256):
    t, d = h.shape
    nj = d // tn
    return pl.pallas_call(
        _convin_kernel,
        grid=(nj, t // seq),
        in_specs=[
            pl.BlockSpec((seq, d), lambda j, b: (b, 0)),
            pl.BlockSpec((d, tn), lambda j, b: (0, j)),
            pl.BlockSpec((d, tn), lambda j, b: (0, j + nj)),
            pl.BlockSpec((d, tn), lambda j, b: (0, j + 2 * nj)),
            pl.BlockSpec((CONV_WIDTH, tn), lambda j, b: (0, j)),
        ],
        out_specs=pl.BlockSpec((seq, tn), lambda j, b: (b, j)),
        out_shape=jax.ShapeDtypeStruct((t, d), BF16),
        scratch_shapes=[pltpu.VMEM((seq + SUBLANES, tn), F32)],
        compiler_params=_params(("arbitrary", "arbitrary"), 48),
        name="conv_in",
    )(h, w_in, w_in, w_in, conv_w)


def _resmm_kernel(a_ref, w_ref, x_ref, g_ref, o_ref):
    o_ref[...] = x_ref[...] + g_ref[0] * _dot(a_ref[...], w_ref[...])


def _resmm_norm_kernel(a_ref, w_ref, x_ref, g_ref, gain_ref, sc_ref, sh_ref, o_ref, h_ref):
    xn = x_ref[...] + g_ref[0] * _dot(a_ref[...], w_ref[...])
    o_ref[...] = xn
    h_ref[...] = _normmod(xn, gain_ref[...], sc_ref[0], sh_ref[0]).astype(h_ref.dtype)


def _resmm_call(a, w, x, modr, row, g_chunk, seq, next_norm=None, tm=512):
    t, k = a.shape
    d = w.shape[1]
    rpb = seq // tm
    in_specs = [
        pl.BlockSpec((tm, k), lambda i: (i, 0)),
        pl.BlockSpec((k, d), lambda i: (0, 0)),
        pl.BlockSpec((tm, d), lambda i: (i, 0)),
        _mod_spec(d, rpb, row, g_chunk),
    ]
    row_spec = pl.BlockSpec((tm, d), lambda i: (i, 0))
    cp = _params(("arbitrary",), 52)
    if next_norm is None:
        return pl.pallas_call(
            _resmm_kernel, grid=(t // tm,), in_specs=in_specs, out_specs=row_spec,
            out_shape=jax.ShapeDtypeStruct((t, d), F32), compiler_params=cp,
            name="res_matmul")(a, w, x, modr)
    gain, nrow, sc_chunk, sh_chunk = next_norm
    in_specs += [pl.BlockSpec((1, d), lambda i: (0, 0)),
                 _mod_spec(d, rpb, nrow, sc_chunk), _mod_spec(d, rpb, nrow, sh_chunk)]
    return pl.pallas_call(
        _resmm_norm_kernel, grid=(t // tm,), in_specs=in_specs, out_specs=[row_spec, row_spec],
        out_shape=[jax.ShapeDtypeStruct((t, d), F32), jax.ShapeDtypeStruct((t, d), BF16)],
        compiler_params=cp, name="res_matmul_norm")(a, w, x, modr, gain.reshape(1, d), modr, modr)


def _ffn_kernel(h_ref, wg_ref, wu_ref, wd_ref, x_ref, g_ref, gain_ref, sc_ref, sh_ref,
                o_ref, hn_ref):
    j = pl.program_id(1)
    h = h_ref[...]
    act = _silu(_dot(h, wg_ref[...])) * _dot(h, wu_ref[...])
    part = _dot(act.astype(BF16), wd_ref[...])

    @pl.when(j == 0)
    def _():
        o_ref[...] = part

    @pl.when(j > 0)
    def _():
        o_ref[...] += part

    @pl.when(j == pl.num_programs(1) - 1)
    def _():
        xn = x_ref[...] + g_ref[0] * o_ref[...]
        o_ref[...] = xn
        hn_ref[...] = _normmod(xn, gain_ref[...], sc_ref[0], sh_ref[0]).astype(hn_ref.dtype)


def _ffn_call(h, wg, wu, wd, x, modr, row, g_chunk, seq, next_norm, tm=512, tf=512):
    t, d = h.shape
    f = wg.shape[1]
    rpb = seq // tm
    gain, nrow, sc_chunk, sh_chunk = next_norm
    row_spec = pl.BlockSpec((tm, d), lambda i, j: (i, 0))
    return pl.pallas_call(
        _ffn_kernel,
        grid=(t // tm, f // tf),
        in_specs=[
            row_spec,
            pl.BlockSpec((d, tf), lambda i, j: (0, j)),
            pl.BlockSpec((d, tf), lambda i, j: (0, j)),
            pl.BlockSpec((tf, d), lambda i, j: (j, 0)),
            row_spec,
            _mod_spec(d, rpb, row, g_chunk),
            pl.BlockSpec((1, d), lambda i, j: (0, 0)),
            _mod_spec(d, rpb, nrow, sc_chunk),
            _mod_spec(d, rpb, nrow, sh_chunk),
        ],
        out_specs=[row_spec, row_spec],
        out_shape=[jax.ShapeDtypeStruct((t, d), F32), jax.ShapeDtypeStruct((t, d), BF16)],
        compiler_params=_params(("arbitrary", "arbitrary"), 52),
        name="dense_ffn",
    )(h, wg, wu, wd, x, modr, gain.reshape(1, d), modr, modr)


def _qk_kernel(h_ref, w_ref, gain_ref, cos_ref, sin_ref, ones_ref, perm_ref, o_ref, *, head_dim):
    acc = _dot(h_ref[...], w_ref[...])
    pair = ones_ref.shape[0]
    gain = gain_ref[...]
    cos = cos_ref[...]
    sin = sin_ref[...]
    ones = ones_ref[...]
    perm = perm_ref[...]
    for p0 in range(0, acc.shape[1], pair):
        xp = acc[:, p0:p0 + pair]
        sq = xp * xp
        sq_a = sq.astype(BF16)
        sq_b = (sq - sq_a.astype(F32)).astype(BF16)
        ss = _dot(sq_a, ones) + _dot(sq_b, ones)
        inv = lax.rsqrt(ss * (1.0 / head_dim) + NORM_EPS)
        z = xp * gain
        z_a = z.astype(BF16)
        z_b = (z - z_a.astype(F32)).astype(BF16)
        rot = _dot(z_a, perm) + _dot(z_b, perm)
        out = (z * cos + rot * sin) * inv
        o_ref[:, p0:p0 + pair] = out.astype(o_ref.dtype)


def _v_kernel(h_ref, w_ref, o_ref):
    o_ref[...] = _dot(h_ref[...], w_ref[...]).astype(o_ref.dtype)


def _head_pair_matrices(head_dim):
    pair = 2 * head_dim
    src = jnp.arange(pair)[:, None]
    dst = jnp.arange(pair)[None, :]
    same_head = (src // head_dim) == (dst // head_dim)
    ones = same_head.astype(BF16)
    perm = (same_head & ((src % head_dim) == ((dst + head_dim // 2) % head_dim))).astype(BF16)
    return ones, perm


def _qkv_call(h, w_qkv, which, seq, head_dim, gain=None, cos=None, sin=None, tm=1024, tn=512):
    t, d = h.shape
    nj = d // tn
    off = which * nj
    rpb = seq // tm
    h_spec = pl.BlockSpec((tm, d), lambda j, i: (i, 0))
    w_spec = pl.BlockSpec((d, tn), lambda j, i: (0, j + off))
    out_spec = pl.BlockSpec((tm, tn), lambda j, i: (i, j))
    out_shape = jax.ShapeDtypeStruct((t, d), BF16)
    cp = _params(("arbitrary", "arbitrary"), 40)
    if gain is None:
        return pl.pallas_call(_v_kernel, grid=(nj, t // tm), in_specs=[h_spec, w_spec],
                              out_specs=out_spec, out_shape=out_shape, compiler_params=cp,
                              name="v_proj")(h, w_qkv)
    pair = 2 * head_dim
    ones, perm = _head_pair_matrices(head_dim)
    tab_spec = pl.BlockSpec((tm, pair), lambda j, i: (i % rpb, 0))
    const_spec = pl.BlockSpec((pair, pair), lambda j, i: (0, 0))
    return pl.pallas_call(
        functools.partial(_qk_kernel, head_dim=head_dim),
        grid=(nj, t // tm),
        in_specs=[h_spec, w_spec, pl.BlockSpec((1, pair), lambda j, i: (0, 0)),
                  tab_spec, tab_spec, const_spec, const_spec],
        out_specs=out_spec, out_shape=out_shape, compiler_params=cp,
        name="qk_proj",
    )(h, w_qkv, jnp.tile(gain.reshape(1, head_dim), (1, 2)), cos, sin, ones, perm)


def _attn_kernel(q_ref, k_ref, v_ref, o_ref, vt_scr, s_scr):
    seq, hd = q_ref.shape
    blk = MOBA_BLOCK
    nb = seq // blk

    rows = []
    for n in range(nb):
        vt_scr[n] = v_ref[n * blk:(n + 1) * blk, :].astype(F32).T.astype(BF16)
        kb = k_ref[n * blk:(n + 1) * blk, :].astype(F32)
        rows.append(jnp.sum(kb, axis=0, keepdims=True) * (1.0 / blk))
    km = jnp.concatenate(rows, axis=0)
    km_a = km.astype(BF16)
    rem = km - km_a.astype(F32)
    km_b = rem.astype(BF16)
    km_c = (rem - km_b.astype(F32)).astype(BF16)
    km_parts = jnp.concatenate([km_a, km_b, km_c, jnp.zeros_like(km_a)], axis=0)

    causal = (lax.broadcasted_iota(jnp.int32, (blk, blk), 0)
              <= lax.broadcasted_iota(jnp.int32, (blk, blk), 1))

    for qi in range(nb):
        qb = q_ref[qi * blk:(qi + 1) * blk, :]
        sel = None
        if qi > MOBA_TOPK:
            g3 = lax.dot_general(km_parts, qb, _NT_DIMS, preferred_element_type=F32)
            gate = g3[0:nb] + g3[nb:2 * nb] + g3[2 * nb:3 * nb]
            g_row = [gate[n:n + 1, :] for n in range(qi)]
            rank = [jnp.zeros((1, blk), F32) for _ in range(qi)]
            for a in range(qi):
                for b in range(a + 1, qi):
                    a_first = (g_row[a] >= g_row[b]).astype(F32)
                    rank[b] = rank[b] + a_first
                    rank[a] = rank[a] + (1.0 - a_first)
            sel = [rank[n] < MOBA_TOPK for n in range(qi)]

        m_run = None
        for n in range(qi + 1):
            s = lax.dot_general(k_ref[n * blk:(n + 1) * blk, :], qb, _NT_DIMS,
                                preferred_element_type=F32)
            if n == qi:
                s = jnp.where(causal, s, NEG_INF)
            elif sel is not None:
                s = jnp.where(sel[n], s, NEG_INF)
            s_scr[n] = s
            cm = jnp.max(s, axis=0, keepdims=True)
            m_run = cm if m_run is None else jnp.maximum(m_run, cm)

        l_run = None
        acc = None
        for n in range(qi + 1):
            p = jnp.exp(s_scr[n] - m_run)
            ps = jnp.sum(p, axis=0, keepdims=True)
            pv = _dot(vt_scr[n], p.astype(BF16))
            l_run = ps if l_run is None else l_run + ps
            acc = pv if acc is None else acc + pv
        out_t = acc * (1.0 / l_run)
        o_ref[qi * blk:(qi + 1) * blk, :] = out_t.T.astype(o_ref.dtype)


def _attn_call(q, k, v, seq, head_dim):
    t, d = q.shape
    nh = d // head_dim
    spec = pl.BlockSpec((seq, head_dim), lambda b, h: (b, h))
    return pl.pallas_call(
        _attn_kernel,
        grid=(t // seq, nh),
        in_specs=[spec, spec, spec],
        out_specs=spec,
        out_shape=jax.ShapeDtypeStruct((t, d), BF16),
        scratch_shapes=[pltpu.VMEM((seq // MOBA_BLOCK, head_dim, MOBA_BLOCK), BF16),
                        pltpu.VMEM((seq // MOBA_BLOCK, MOBA_BLOCK, MOBA_BLOCK), F32)],
        compiler_params=_params(("arbitrary", "arbitrary"), 32),
        name="moba_attention",
    )(q, k, v)


def _router_kernel(x_ref, gain_ref, sc_ref, sh_ref, wr_ref, br_ref, h_ref, idx_ref, wt_ref,
                   *, n_experts):
    h = _normmod(x_ref[...], gain_ref[...], sc_ref[0], sh_ref[0])
    h_ref[...] = h
    h_a = h.astype(BF16)
    h_b = (h - h_a.astype(F32)).astype(BF16)
    w = wr_ref[...]
    w_a = w.astype(BF16)
    w_b = (w - w_a.astype(F32)).astype(BF16)
    logits = _dot(h_a, w_a) + _dot(h_b, w_a) + _dot(h_a, w_b) + br_ref[...]
    lane = lax.broadcasted_iota(jnp.int32, logits.shape, 1)
    logits = jnp.where(lane < n_experts, logits, -jnp.inf)
    m1 = jnp.max(logits, axis=-1, keepdims=True)
    i1 = jnp.min(jnp.where(logits == m1, lane, LANES), axis=-1, keepdims=True)
    rest = jnp.where(lane == i1, -jnp.inf, logits)
    m2 = jnp.max(rest, axis=-1, keepdims=True)
    i2 = jnp.min(jnp.where(rest == m2, lane, LANES), axis=-1, keepdims=True)
    e2 = jnp.exp(m2 - m1)
    w1 = 1.0 / (1.0 + e2)
    w2 = e2 * w1
    idx_ref[...] = jnp.where(lane == 0, i1, jnp.where(lane == 1, i2, 0))
    wt_ref[...] = jnp.where(lane == 0, w1, jnp.where(lane == 1, w2, 0.0))


def _router_call(x, gain, modr, row, sc_chunk, sh_chunk, w_router, b_router, seq, tm=512):
    t, d = x.shape
    n_experts = w_router.shape[1]
    rpb = seq // tm
    wr = jnp.zeros((d, LANES), F32).at[:, :n_experts].set(w_router)
    br = jnp.zeros((1, LANES), F32).at[0, :n_experts].set(b_router)
    return pl.pallas_call(
        functools.partial(_router_kernel, n_experts=n_experts),
        grid=(t // tm,),
        in_specs=[
            pl.BlockSpec((tm, d), lambda i: (i, 0)),
            pl.BlockSpec((1, d), lambda i: (0, 0)),
            _mod_spec(d, rpb, row, sc_chunk),
            _mod_spec(d, rpb, row, sh_chunk),
            pl.BlockSpec((d, LANES), lambda i: (0, 0)),
            pl.BlockSpec((1, LANES), lambda i: (0, 0)),
        ],
        out_specs=[
            pl.BlockSpec((tm, d), lambda i: (i, 0)),
            pl.BlockSpec((tm, LANES), lambda i: (i, 0)),
            pl.BlockSpec((tm, LANES), lambda i: (i, 0)),
        ],
        out_shape=[
            jax.ShapeDtypeStruct((t, d), F32),
            jax.ShapeDtypeStruct((t, LANES), jnp.int32),
            jax.ShapeDtypeStruct((t, LANES), F32),
        ],
        compiler_params=_params(("arbitrary",), 40),
        name="moe_router",
    )(x, gain.reshape(1, d), modr, modr, wr, br)


def _dispatch_copy(dest_ref, h_ref, xs_ref, sem, r, k):
    d = dest_ref[0, 0, EXPERT_TOPK * r + k]
    return pltpu.make_async_copy(h_ref.at[pl.ds(r, 1), :], xs_ref.at[pl.ds(d, 1), :], sem)


def _dispatch_kernel(dest_ref, h_ref, xs_in_ref, xs_ref, sem):
    del xs_in_ref
    tm = h_ref.shape[0]

    def issue(r, c):
        for k in range(EXPERT_TOPK):
            _dispatch_copy(dest_ref, h_ref, xs_ref, sem, r, k).start()
        return c

    def drain(r, c):
        for k in range(EXPERT_TOPK):
            _dispatch_copy(dest_ref, h_ref, xs_ref, sem, r, k).wait()
        return c

    lax.fori_loop(0, tm, issue, 0)
    lax.fori_loop(0, tm, drain, 0)


def _dispatch_call(h, dest, n_rows, tm=256):
    t, d = h.shape
    nt = t // tm
    return pl.pallas_call(
        _dispatch_kernel,
        grid=(nt,),
        in_specs=[
            pl.BlockSpec((1, 1, EXPERT_TOPK * tm), lambda i: (i, 0, 0), memory_space=pltpu.SMEM),
            pl.BlockSpec((tm, d), lambda i: (i, 0)),
            pl.BlockSpec(memory_space=pl.ANY),
        ],
        out_specs=pl.BlockSpec(memory_space=pl.ANY),
        out_shape=jax.ShapeDtypeStruct((n_rows, d), h.dtype),
        scratch_shapes=[pltpu.SemaphoreType.DMA],
        input_output_aliases={2: 0},
        compiler_params=_params(("arbitrary",), 32),
        name="moe_dispatch",
    )(dest.reshape(nt, 1, EXPERT_TOPK * tm), h, jnp.zeros((n_rows, d), h.dtype))


def _expert_kernel(bmap_ref, be_ref, nu_ref, xs_ref, wg_ref, wu_ref, wd_ref, ys_ref, xb_scr):
    del bmap_ref, be_ref
    i = pl.program_id(0)
    j = pl.program_id(1)

    @pl.when((i >= nu_ref[0]) & (j == 0))
    def _():
        ys_ref[...] = jnp.zeros(ys_ref.shape, F32)

    @pl.when(i < nu_ref[0])
    def _():
        @pl.when(j == 0)
        def _():
            xb_scr[...] = xs_ref[...].astype(BF16)
            ys_ref[...] = jnp.zeros(ys_ref.shape, F32)

        x = xb_scr[...]
        act = _silu(_dot(x, wg_ref[0])) * _dot(x, wu_ref[0])
        ys_ref[...] += _dot(act.astype(BF16), wd_ref[0])


def _expert_call(xs, blk_map, blk_e, n_used, wg, wu, wd, tm, tf=512):
    n_rows, d = xs.shape
    f = wg.shape[2]
    nf = f // tf

    def jj(i, j, nu):
        return jnp.where(i < nu[0], j, nf - 1)

    grid_spec = pltpu.PrefetchScalarGridSpec(
        num_scalar_prefetch=3,
        grid=(n_rows // tm, nf),
        in_specs=[
            pl.BlockSpec((tm, d), lambda i, j, bm, be, nu: (bm[i], 0)),
            pl.BlockSpec((1, d, tf), lambda i, j, bm, be, nu: (be[i], 0, jj(i, j, nu))),
            pl.BlockSpec((1, d, tf), lambda i, j, bm, be, nu: (be[i], 0, jj(i, j, nu))),
            pl.BlockSpec((1, tf, d), lambda i, j, bm, be, nu: (be[i], jj(i, j, nu), 0)),
        ],
        out_specs=pl.BlockSpec((tm, d), lambda i, j, bm, be, nu: (i, 0)),
        scratch_shapes=[pltpu.VMEM((tm, d), BF16)],
    )
    return pl.pallas_call(
        _expert_kernel,
        grid_spec=grid_spec,
        out_shape=jax.ShapeDtypeStruct((n_rows, d), F32),
        compiler_params=_params(("arbitrary", "arbitrary"), 48),
        name="moe_experts",
    )(blk_map, blk_e, n_used, xs, wg, wu, wd)


def _combine_copy(dest_ref, ys_ref, ybuf, sem, r, k):
    d = dest_ref[0, 0, EXPERT_TOPK * r + k]
    return pltpu.make_async_copy(ys_ref.at[pl.ds(d, 1), :], ybuf.at[k, pl.ds(r, 1), :], sem)


def _combine_kernel(dest_ref, ys_ref, x_ref, wt_ref, g_ref, o_ref, ybuf, sem):
    tm = x_ref.shape[0]

    def issue(r, c):
        for k in range(EXPERT_TOPK):
            _combine_copy(dest_ref, ys_ref, ybuf, sem, r, k).start()
        return c

    def drain(r, c):
        for k in range(EXPERT_TOPK):
            _combine_copy(dest_ref, ys_ref, ybuf, sem, r, k).wait()
        return c

    lax.fori_loop(0, tm, issue, 0)
    lax.fori_loop(0, tm, drain, 0)
    wt = wt_ref[...]
    y = wt[:, 0:1] * ybuf[0]
    for k in range(1, EXPERT_TOPK):
        y = y + wt[:, k:k + 1] * ybuf[k]
    o_ref[...] = x_ref[...] + g_ref[0] * y


def _combine_call(ys, dest, x, wt, modr, row, g_chunk, seq, tm=256):
    t, d = x.shape
    nt = t // tm
    rpb = seq // tm
    return pl.pallas_call(
        _combine_kernel,
        grid=(nt,),
        in_specs=[
            pl.BlockSpec((1, 1, EXPERT_TOPK * tm), lambda i: (i, 0, 0), memory_space=pltpu.SMEM),
            pl.BlockSpec(memory_space=pl.ANY),
            pl.BlockSpec((tm, d), lambda i: (i, 0)),
            pl.BlockSpec((tm, LANES), lambda i: (i, 0)),
            _mod_spec(d, rpb, row, g_chunk),
        ],
        out_specs=pl.BlockSpec((tm, d), lambda i: (i, 0)),
        out_shape=jax.ShapeDtypeStruct((t, d), F32),
        scratch_shapes=[pltpu.VMEM((EXPERT_TOPK, tm, d), F32), pltpu.SemaphoreType.DMA],
        compiler_params=_params(("arbitrary",), 32),
        name="moe_combine",
    )(dest.reshape(nt, 1, EXPERT_TOPK * tm), ys, x, wt, modr)


def _routing_tables(top_idx, n_experts, tm, n_blocks):
    flat_e = top_idx.reshape(-1)
    onehot = (flat_e[:, None] == jnp.arange(n_experts, dtype=jnp.int32)[None, :]).astype(jnp.int32)
    csum = jnp.cumsum(onehot, axis=0)
    counts = csum[-1]
    pos = jnp.sum((csum - 1) * onehot, axis=1)
    nblk = (counts + tm - 1) // tm
    blk_end = jnp.cumsum(nblk)
    blk_start = blk_end - nblk
    dest = (blk_start[flat_e] * tm + pos).astype(jnp.int32)
    n_used = blk_end[-1].astype(jnp.int32)
    blk_map = jnp.minimum(jnp.arange(n_blocks, dtype=jnp.int32), n_used - 1)
    blk_e = jnp.minimum(jnp.searchsorted(blk_end, blk_map, side="right"),
                        n_experts - 1).astype(jnp.int32)
    return dest, blk_map, blk_e, n_used.reshape(1)


def _rope_tables(seq, head_dim, scale=1.0):
    half = head_dim // 2
    inv_freq = jnp.exp(-math.log(ROPE_THETA) * jnp.arange(half, dtype=F32) / half)
    ang = jnp.arange(seq).astype(F32)[:, None] * inv_freq[None, :]
    cos, sin = jnp.cos(ang) * scale, jnp.sin(ang) * scale
    return (jnp.concatenate([cos, cos, cos, cos], axis=-1),
            jnp.concatenate([-sin, sin, -sin, sin], axis=-1))


def kernel(x, c, mod_w, mod_b, norm_mix, norm_ffn, conv_in, conv_w, conv_out, ffn_gate, ffn_up,
           ffn_down, qkv_w, q_norm, k_norm, attn_out, router_w, router_b, exp_gate, exp_up,
           exp_down):
    bsz, seq, d = x.shape
    depth = mod_w.shape[0]
    assert depth == 2, "layer 0 = short-conv + dense FFN, layer 1 = MoBA + MoE"
    t = bsz * seq
    head_dim = d // N_HEADS
    n_experts = router_w.shape[-1]

    mod = _mod_call(c, mod_w, mod_b)
    modr = mod.reshape(depth * bsz, 1, 6 * d)
    xt = x.reshape(t, d)

    row = bsz
    h = _normmod_call(xt, norm_mix[0], modr, 0, 1, 0, seq)
    bu = _convin_call(h, conv_in[0].astype(BF16), conv_w[0], seq)
    xt, h = _resmm_call(bu, conv_out[0].astype(BF16), xt, modr, 0, 2, seq,
                        next_norm=(norm_ffn[0], 0, 4, 3))
    xt, h = _ffn_call(h, ffn_gate[0].astype(BF16), ffn_up[0].astype(BF16),
                      ffn_down[0].astype(BF16), xt, modr, 0, 5, seq,
                      next_norm=(norm_mix[1], row, 1, 0))

    w_qkv = qkv_w[0].astype(BF16)
    cos_q, sin_q = _rope_tables(seq, head_dim, head_dim ** -0.5)
    cos_k, sin_k = _rope_tables(seq, head_dim)
    q = _qkv_call(h, w_qkv, 0, seq, head_dim, q_norm[0], cos_q, sin_q)
    k = _qkv_call(h, w_qkv, 1, seq, head_dim, k_norm[0], cos_k, sin_k)
    v = _qkv_call(h, w_qkv, 2, seq, head_dim)
    o = _attn_call(q, k, v, seq, head_dim)
    xt = _resmm_call(o, attn_out[0].astype(BF16), xt, modr, row, 2, seq)

    hf, top_idx, top_w = _router_call(xt, norm_ffn[1], modr, row, 4, 3,
                                      router_w[0], router_b[0], seq)
    n_blocks = (t * EXPERT_TOPK) // EXPERT_ROWS + n_experts
    dest, blk_map, blk_e, n_used = _routing_tables(top_idx[:, :EXPERT_TOPK], n_experts,
                                                   EXPERT_ROWS, n_blocks)
    xs = _dispatch_call(hf, dest, n_blocks * EXPERT_ROWS)
    ys = _expert_call(xs, blk_map, blk_e, n_used, exp_gate[0].astype(BF16),
                      exp_up[0].astype(BF16), exp_down[0].astype(BF16), EXPERT_ROWS)
    xt = _combine_call(ys, dest, xt, top_w, modr, row, 5, seq)
    return xt.reshape(bsz, seq, d)
```

```python
import functools
import math

import jax
import jax.numpy as jnp
from jax import lax
from jax.experimental import pallas as pl
from jax.experimental.pallas import tpu as pltpu

N_HEADS = 16
CONV_WIDTH = 3
MOBA_BLOCK = 256
MOBA_TOPK = 3
ROPE_THETA = 10000.0
EXPERT_TOPK = 2
EXPERT_ROWS = 512
ROW_DMA_UNROLL = 8
NORM_EPS = 1e-6
NEG_INF = -1e30

LANES = 128
SUBLANES = 8
BF16_SUBLANES = 16
MIB = 1024 * 1024

F32 = jnp.float32
BF16 = jnp.bfloat16

_NT_DIMS = (((1,), (1,)), ((), ()))


def _params(semantics, vmem_mib):
    return pltpu.CompilerParams(dimension_semantics=semantics,
                                vmem_limit_bytes=vmem_mib * MIB)


def _dot(a, b):
    return jnp.dot(a, b, preferred_element_type=F32)


def _silu(a):
    return a * (1.0 / (1.0 + jnp.exp(-a)))


def _mod_kernel(c_ref, w_ref, b_ref, o_ref):
    ca = _silu(c_ref[...])
    o_ref[0] = _dot(ca.astype(BF16), w_ref[0].astype(BF16)) + b_ref[0]


def _mod_call(c, mod_w, mod_b, tn=1024):
    depth, d, n = mod_w.shape
    bsz = c.shape[0]
    return pl.pallas_call(
        _mod_kernel,
        grid=(depth, n // tn),
        in_specs=[
            pl.BlockSpec((bsz, d), lambda l, j: (0, 0)),
            pl.BlockSpec((1, d, tn), lambda l, j: (l, 0, j)),
            pl.BlockSpec((1, 1, tn), lambda l, j: (l, 0, j)),
        ],
        out_specs=pl.BlockSpec((1, bsz, tn), lambda l, j: (l, 0, j)),
        out_shape=jax.ShapeDtypeStruct((depth, bsz, n), F32),
        compiler_params=_params(("arbitrary", "arbitrary"), 40),
        name="adaln_mod",
    )(c, mod_w, mod_b.reshape(depth, 1, n))


def _normmod(x, gain, scale, shift):
    ms = jnp.mean(x * x, axis=-1, keepdims=True)
    y = x * lax.rsqrt(ms + NORM_EPS) * gain
    return y * (1.0 + scale) + shift


def _normmod_kernel(x_ref, gain_ref, sc_ref, sh_ref, o_ref):
    o_ref[...] = _normmod(x_ref[...], gain_ref[...], sc_ref[0], sh_ref[0]).astype(o_ref.dtype)


def _mod_spec(d, rows_per_batch, row, chunk):
    return pl.BlockSpec((1, 1, d), lambda i, *_: (row + i // rows_per_batch, 0, chunk))


def _normmod_call(x, gain, modr, row, sc_chunk, sh_chunk, seq, tm=512):
    t, d = x.shape
    rpb = seq // tm
    return pl.pallas_call(
        _normmod_kernel,
        grid=(t // tm,),
        in_specs=[
            pl.BlockSpec((tm, d), lambda i: (i, 0)),
            pl.BlockSpec((1, d), lambda i: (0, 0)),
            _mod_spec(d, rpb, row, sc_chunk),
            _mod_spec(d, rpb, row, sh_chunk),
        ],
        out_specs=pl.BlockSpec((tm, d), lambda i: (i, 0)),
        out_shape=jax.ShapeDtypeStruct((t, d), BF16),
        compiler_params=_params(("arbitrary",), 32),
        name="normmod",
    )(x, gain.reshape(1, d), modr, modr)


def _convin_kernel(h_ref, wb_ref, wc_ref, wv_ref, cw_ref, o_ref, u_scr):
    seq = h_ref.shape[0]
    h = h_ref[...]
    u = _dot(h, wc_ref[...]) * _dot(h, wv_ref[...])
    u_scr[0:SUBLANES, :] = jnp.zeros((SUBLANES, u.shape[1]), F32)
    u_scr[SUBLANES:, :] = u
    cw = cw_ref[...]
    conv = cw[CONV_WIDTH - 1:CONV_WIDTH, :] * u
    for tap in range(1, CONV_WIDTH):
        conv = conv + (cw[CONV_WIDTH - 1 - tap:CONV_WIDTH - tap, :]
                       * u_scr[SUBLANES - tap:SUBLANES - tap + seq, :])
    o_ref[...] = (_dot(h, wb_ref[...]) * conv).astype(o_ref.dtype)


def _convin_call(h, w_in, conv_w, seq, tn=256):
    t, d = h.shape
    nj = d // tn
    return pl.pallas_call(
        _convin_kernel,
        grid=(nj, t // seq),
        in_specs=[
            pl.BlockSpec((seq, d), lambda j, b: (b, 0)),
            pl.BlockSpec((d, tn), lambda j, b: (0, j)),
            pl.BlockSpec((d, tn), lambda j, b: (0, j + nj)),
            pl.BlockSpec((d, tn), lambda j, b: (0, j + 2 * nj)),
            pl.BlockSpec((CONV_WIDTH, tn), lambda j, b: (0, j)),
        ],
        out_specs=pl.BlockSpec((seq, tn), lambda j, b: (b, j)),
        out_shape=jax.ShapeDtypeStruct((t, d), BF16),
        scratch_shapes=[pltpu.VMEM((seq + SUBLANES, tn), F32)],
        compiler_params=_params(("arbitrary", "arbitrary"), 48),
        name="conv_in",
    )(h, w_in, w_in, w_in, conv_w)


def _resmm_kernel(a_ref, w_ref, x_ref, g_ref, o_ref):
    o_ref[...] = x_ref[...] + g_ref[0] * _dot(a_ref[...], w_ref[...])


def _resmm_norm_kernel(a_ref, w_ref, x_ref, g_ref, gain_ref, sc_ref, sh_ref, o_ref, h_ref):
    xn = x_ref[...] + g_ref[0] * _dot(a_ref[...], w_ref[...])
    o_ref[...] = xn
    h_ref[...] = _normmod(xn, gain_ref[...], sc_ref[0], sh_ref[0]).astype(h_ref.dtype)


def _resmm_call(a, w, x, modr, row, g_chunk, seq, next_norm=None, tm=512):
    t, k = a.shape
    d = w.shape[1]
    rpb = seq // tm
    in_specs = [
        pl.BlockSpec((tm, k), lambda i: (i, 0)),
        pl.BlockSpec((k, d), lambda i: (0, 0)),
        pl.BlockSpec((tm, d), lambda i: (i, 0)),
        _mod_spec(d, rpb, row, g_chunk),
    ]
    row_spec = pl.BlockSpec((tm, d), lambda i: (i, 0))
    cp = _params(("arbitrary",), 52)
    if next_norm is None:
        return pl.pallas_call(
            _resmm_kernel, grid=(t // tm,), in_specs=in_specs, out_specs=row_spec,
            out_shape=jax.ShapeDtypeStruct((t, d), F32), compiler_params=cp,
            name="res_matmul")(a, w, x, modr)
    gain, nrow, sc_chunk, sh_chunk = next_norm
    in_specs += [pl.BlockSpec((1, d), lambda i: (0, 0)),
                 _mod_spec(d, rpb, nrow, sc_chunk), _mod_spec(d, rpb, nrow, sh_chunk)]
    return pl.pallas_call(
        _resmm_norm_kernel, grid=(t // tm,), in_specs=in_specs, out_specs=[row_spec, row_spec],
        out_shape=[jax.ShapeDtypeStruct((t, d), F32), jax.ShapeDtypeStruct((t, d), BF16)],
        compiler_params=cp, name="res_matmul_norm")(a, w, x, modr, gain.reshape(1, d), modr, modr)


def _ffn_kernel(h_ref, wg_ref, wu_ref, wd_ref, x_ref, g_ref, gain_ref, sc_ref, sh_ref,
                o_ref, hn_ref):
    j = pl.program_id(1)
    h = h_ref[...]
    act = _silu(_dot(h, wg_ref[...])) * _dot(h, wu_ref[...])
    part = _dot(act.astype(BF16), wd_ref[...])

    @pl.when(j == 0)
    def _():
        o_ref[...] = part

    @pl.when(j > 0)
    def _():
        o_ref[...] += part

    @pl.when(j == pl.num_programs(1) - 1)
    def _():
        xn = x_ref[...] + g_ref[0] * o_ref[...]
        o_ref[...] = xn
        hn_ref[...] = _normmod(xn, gain_ref[...], sc_ref[0], sh_ref[0]).astype(hn_ref.dtype)


def _ffn_call(h, wg, wu, wd, x, modr, row, g_chunk, seq, next_norm, tm=512, tf=512):
    t, d = h.shape
    f = wg.shape[1]
    rpb = seq // tm
    gain, nrow, sc_chunk, sh_chunk = next_norm
    row_spec = pl.BlockSpec((tm, d), lambda i, j: (i, 0))
    return pl.pallas_call(
        _ffn_kernel,
        grid=(t // tm, f // tf),
        in_specs=[
            row_spec,
            pl.BlockSpec((d, tf), lambda i, j: (0, j)),
            pl.BlockSpec((d, tf), lambda i, j: (0, j)),
            pl.BlockSpec((tf, d), lambda i, j: (j, 0)),
            row_spec,
            _mod_spec(d, rpb, row, g_chunk),
            pl.BlockSpec((1, d), lambda i, j: (0, 0)),
            _mod_spec(d, rpb, nrow, sc_chunk),
            _mod_spec(d, rpb, nrow, sh_chunk),
        ],
        out_specs=[row_spec, row_spec],
        out_shape=[jax.ShapeDtypeStruct((t, d), F32), jax.ShapeDtypeStruct((t, d), BF16)],
        compiler_params=_params(("arbitrary", "arbitrary"), 52),
        name="dense_ffn",
    )(h, wg, wu, wd, x, modr, gain.reshape(1, d), modr, modr)


def _qk_kernel(h_ref, w_ref, gain_ref, cos_ref, sin_ref, ones_ref, perm_ref, o_ref, *, head_dim):
    acc = _dot(h_ref[...], w_ref[...])
    pair = ones_ref.shape[0]
    gain = gain_ref[...]
    cos = cos_ref[...]
    sin = sin_ref[...]
    ones = ones_ref[...]
    perm = perm_ref[...]
    for p0 in range(0, acc.shape[1], pair):
        xp = acc[:, p0:p0 + pair]
        sq = xp * xp
        sq_a = sq.astype(BF16)
        sq_b = (sq - sq_a.astype(F32)).astype(BF16)
        ss = _dot(sq_a, ones) + _dot(sq_b, ones)
        inv = lax.rsqrt(ss * (1.0 / head_dim) + NORM_EPS)
        z = xp * gain
        z_a = z.astype(BF16)
        z_b = (z - z_a.astype(F32)).astype(BF16)
        rot = _dot(z_a, perm) + _dot(z_b, perm)
        out = (z * cos + rot * sin) * inv
        o_ref[:, p0:p0 + pair] = out.astype(o_ref.dtype)


def _v_kernel(h_ref, w_ref, o_ref):
    o_ref[...] = _dot(h_ref[...], w_ref[...]).astype(o_ref.dtype)


def _head_pair_matrices(head_dim):
    pair = 2 * head_dim
    src = jnp.arange(pair)[:, None]
    dst = jnp.arange(pair)[None, :]
    same_head = (src // head_dim) == (dst // head_dim)
    ones = same_head.astype(BF16)
    perm = (same_head & ((src % head_dim) == ((dst + head_dim // 2) % head_dim))).astype(BF16)
    return ones, perm


def _qkv_call(h, w_qkv, which, seq, head_dim, gain=None, cos=None, sin=None, tm=1024, tn=512):
    t, d = h.shape
    nj = d // tn
    off = which * nj
    rpb = seq // tm
    h_spec = pl.BlockSpec((tm, d), lambda j, i: (i, 0))
    w_spec = pl.BlockSpec((d, tn), lambda j, i: (0, j + off))
    out_spec = pl.BlockSpec((tm, tn), lambda j, i: (i, j))
    out_shape = jax.ShapeDtypeStruct((t, d), BF16)
    cp = _params(("arbitrary", "arbitrary"), 40)
    if gain is None:
        return pl.pallas_call(_v_kernel, grid=(nj, t // tm), in_specs=[h_spec, w_spec],
                              out_specs=out_spec, out_shape=out_shape, compiler_params=cp,
                              name="v_proj")(h, w_qkv)
    pair = 2 * head_dim
    ones, perm = _head_pair_matrices(head_dim)
    tab_spec = pl.BlockSpec((tm, pair), lambda j, i: (i % rpb, 0))
    const_spec = pl.BlockSpec((pair, pair), lambda j, i: (0, 0))
    return pl.pallas_call(
        functools.partial(_qk_kernel, head_dim=head_dim),
        grid=(nj, t // tm),
        in_specs=[h_spec, w_spec, pl.BlockSpec((1, pair), lambda j, i: (0, 0)),
                  tab_spec, tab_spec, const_spec, const_spec],
        out_specs=out_spec, out_shape=out_shape, compiler_params=cp,
        name="qk_proj",
    )(h, w_qkv, jnp.tile(gain.reshape(1, head_dim), (1, 2)), cos, sin, ones, perm)


def _attn_kernel(q_ref, k_ref, v_ref, o_ref, vt_scr, s_scr):
    seq, hd = q_ref.shape
    blk = MOBA_BLOCK
    nb = seq // blk

    rows = []
    for n in range(nb):
        vt_scr[n, 0:hd, :] = v_ref[n * blk:(n + 1) * blk, :].astype(F32).T.astype(BF16)
        vt_scr[n, hd:, :] = jnp.ones((vt_scr.shape[1] - hd, blk), BF16)
        kb = k_ref[n * blk:(n + 1) * blk, :].astype(F32)
        rows.append(jnp.sum(kb, axis=0, keepdims=True) * (1.0 / blk))
    km = jnp.concatenate(rows, axis=0)
    km_a = km.astype(BF16)
    rem = km - km_a.astype(F32)
    km_b = rem.astype(BF16)
    km_c = (rem - km_b.astype(F32)).astype(BF16)
    km_parts = jnp.concatenate([km_a, km_b, km_c, jnp.zeros_like(km_a)], axis=0)

    causal = (lax.broadcasted_iota(jnp.int32, (blk, blk), 0)
              <= lax.broadcasted_iota(jnp.int32, (blk, blk), 1))

    def masked_scores(qi):
        qb = q_ref[qi * blk:(qi + 1) * blk, :]
        sel = None
        if qi > MOBA_TOPK:
            g3 = lax.dot_general(km_parts, qb, _NT_DIMS, preferred_element_type=F32)
            gate = g3[0:nb] + g3[nb:2 * nb] + g3[2 * nb:3 * nb]
            g_row = [gate[n:n + 1, :] for n in range(qi)]
            rank = [jnp.zeros((1, blk), F32) for _ in range(qi)]
            for a in range(qi):
                for b in range(a + 1, qi):
                    a_first = (g_row[a] >= g_row[b]).astype(F32)
                    rank[b] = rank[b] + a_first
                    rank[a] = rank[a] + (1.0 - a_first)
            sel = [rank[n] < MOBA_TOPK for n in range(qi)]

        m_run = None
        for n in range(qi + 1):
            s = lax.dot_general(k_ref[n * blk:(n + 1) * blk, :], qb, _NT_DIMS,
                                preferred_element_type=F32)
            if n == qi:
                s = jnp.where(causal, s, NEG_INF)
            elif sel is not None:
                s = jnp.where(sel[n], s, NEG_INF)
            s_scr[qi % 2, n] = s
            cm = jnp.max(s, axis=0, keepdims=True)
            m_run = cm if m_run is None else jnp.maximum(m_run, cm)
        return m_run

    def weighted_values(qi, m_run):
        acc = None
        for n in range(qi + 1):
            p = jnp.exp2(s_scr[qi % 2, n] - m_run)
            pv = _dot(vt_scr[n], p.astype(BF16))
            acc = pv if acc is None else acc + pv
        out_t = acc[0:hd] * (1.0 / acc[hd:hd + 1])
        o_ref[qi * blk:(qi + 1) * blk, :] = out_t.T.astype(o_ref.dtype)

    m_next = masked_scores(0)
    for qi in range(nb):
        m_cur = m_next
        if qi + 1 < nb:
            m_next = masked_scores(qi + 1)
        weighted_values(qi, m_cur)


def _attn_call(q, k, v, seq, head_dim):
    t, d = q.shape
    nh = d // head_dim
    nb = seq // MOBA_BLOCK
    spec = pl.BlockSpec((seq, head_dim), lambda b, h: (b, h))
    return pl.pallas_call(
        _attn_kernel,
        grid=(t // seq, nh),
        in_specs=[spec, spec, spec],
        out_specs=spec,
        out_shape=jax.ShapeDtypeStruct((t, d), BF16),
        scratch_shapes=[pltpu.VMEM((nb, head_dim + BF16_SUBLANES, MOBA_BLOCK), BF16),
                        pltpu.VMEM((2, nb, MOBA_BLOCK, MOBA_BLOCK), F32)],
        compiler_params=_params(("arbitrary", "arbitrary"), 32),
        name="moba_attention",
    )(q, k, v)


def _router_kernel(x_ref, gain_ref, sc_ref, sh_ref, wr_ref, br_ref, h_ref, idx_ref, wt_ref,
                   *, n_experts):
    h = _normmod(x_ref[...], gain_ref[...], sc_ref[0], sh_ref[0])
    h_ref[...] = h
    h_a = h.astype(BF16)
    h_b = (h - h_a.astype(F32)).astype(BF16)
    w = wr_ref[...]
    w_a = w.astype(BF16)
    w_b = (w - w_a.astype(F32)).astype(BF16)
    logits = _dot(h_a, w_a) + _dot(h_b, w_a) + _dot(h_a, w_b) + br_ref[...]
    lane = lax.broadcasted_iota(jnp.int32, logits.shape, 1)
    logits = jnp.where(lane < n_experts, logits, -jnp.inf)
    m1 = jnp.max(logits, axis=-1, keepdims=True)
    i1 = jnp.min(jnp.where(logits == m1, lane, LANES), axis=-1, keepdims=True)
    rest = jnp.where(lane == i1, -jnp.inf, logits)
    m2 = jnp.max(rest, axis=-1, keepdims=True)
    i2 = jnp.min(jnp.where(rest == m2, lane, LANES), axis=-1, keepdims=True)
    e2 = jnp.exp(m2 - m1)
    w1 = 1.0 / (1.0 + e2)
    w2 = e2 * w1
    idx_ref[...] = jnp.where(lane == 0, i1, jnp.where(lane == 1, i2, 0))
    wt_ref[...] = jnp.where(lane == 0, w1, jnp.where(lane == 1, w2, 0.0))


def _router_call(x, gain, modr, row, sc_chunk, sh_chunk, w_router, b_router, seq, tm=512):
    t, d = x.shape
    n_experts = w_router.shape[1]
    rpb = seq // tm
    wr = jnp.zeros((d, LANES), F32).at[:, :n_experts].set(w_router)
    br = jnp.zeros((1, LANES), F32).at[0, :n_experts].set(b_router)
    return pl.pallas_call(
        functools.partial(_router_kernel, n_experts=n_experts),
        grid=(t // tm,),
        in_specs=[
            pl.BlockSpec((tm, d), lambda i: (i, 0)),
            pl.BlockSpec((1, d), lambda i: (0, 0)),
            _mod_spec(d, rpb, row, sc_chunk),
            _mod_spec(d, rpb, row, sh_chunk),
            pl.BlockSpec((d, LANES), lambda i: (0, 0)),
            pl.BlockSpec((1, LANES), lambda i: (0, 0)),
        ],
        out_specs=[
            pl.BlockSpec((tm, d), lambda i: (i, 0)),
            pl.BlockSpec((tm, LANES), lambda i: (i, 0)),
            pl.BlockSpec((tm, LANES), lambda i: (i, 0)),
        ],
        out_shape=[
            jax.ShapeDtypeStruct((t, d), F32),
            jax.ShapeDtypeStruct((t, LANES), jnp.int32),
            jax.ShapeDtypeStruct((t, LANES), F32),
        ],
        compiler_params=_params(("arbitrary",), 40),
        name="moe_router",
    )(x, gain.reshape(1, d), modr, modr, wr, br)


def _dispatch_kernel(dest_ref, h_ref, xs_in_ref, xs_ref, sem):
    del xs_in_ref
    tm = h_ref.shape[0]

    def issue(r, c):
        for k in range(EXPERT_TOPK):
            d = dest_ref[0, 0, EXPERT_TOPK * r + k]
            pltpu.make_async_copy(h_ref.at[pl.ds(r, 1), :], xs_ref.at[pl.ds(d, 1), :],
                                  sem).start(priority=k % 2)
        return c

    lax.fori_loop(0, tm, issue, 0, unroll=ROW_DMA_UNROLL)
    for k in range(EXPERT_TOPK):
        pltpu.make_async_copy(h_ref, xs_ref.at[pl.ds(0, tm), :], sem).wait()


def _dispatch_call(h, dest, n_rows, tm=256):
    t, d = h.shape
    nt = t // tm
    return pl.pallas_call(
        _dispatch_kernel,
        grid=(nt,),
        in_specs=[
            pl.BlockSpec((1, 1, EXPERT_TOPK * tm), lambda i: (i, 0, 0), memory_space=pltpu.SMEM),
            pl.BlockSpec((tm, d), lambda i: (i, 0)),
            pl.BlockSpec(memory_space=pl.ANY),
        ],
        out_specs=pl.BlockSpec(memory_space=pl.ANY),
        out_shape=jax.ShapeDtypeStruct((n_rows, d), h.dtype),
        scratch_shapes=[pltpu.SemaphoreType.DMA],
        input_output_aliases={2: 0},
        compiler_params=_params(("arbitrary",), 32),
        name="moe_dispatch",
    )(dest.reshape(nt, 1, EXPERT_TOPK * tm), h, jnp.zeros((n_rows, d), h.dtype))


BLOCK_COMPUTE, BLOCK_FIRST, BLOCK_UNUSED = 0, 1, 2


def _expert_kernel(kind_ref, xmap_ref, we_ref, fmap_ref, xs_ref, wg_ref, wu_ref, wd_ref,
                   yfirst_ref, ys_ref, xb_scr):
    del xmap_ref, we_ref, fmap_ref
    i = pl.program_id(0)
    j = pl.program_id(1)
    kind = kind_ref[i]

    @pl.when((kind == BLOCK_UNUSED) & (j == 0))
    def _():
        ys_ref[...] = jnp.zeros(ys_ref.shape, F32)

    @pl.when((kind == BLOCK_FIRST) & (j == 0))
    def _():
        ys_ref[...] = yfirst_ref[...]

    @pl.when(kind == BLOCK_COMPUTE)
    def _():
        @pl.when(j == 0)
        def _():
            xb_scr[...] = xs_ref[...].astype(BF16)
            ys_ref[...] = jnp.zeros(ys_ref.shape, F32)

        x = xb_scr[...]
        act = _silu(_dot(x, wg_ref[0])) * _dot(x, wu_ref[0])
        ys_ref[...] += _dot(act.astype(BF16), wd_ref[0])


def _expert_call(xs, tables, y_first, wg, wu, wd, tm, tf=512):
    n_rows, d = xs.shape
    f = wg.shape[2]
    nf = f // tf

    def wj(i, j, kind):
        return jnp.where(kind[i] == BLOCK_COMPUTE, j, nf - 1)

    grid_spec = pltpu.PrefetchScalarGridSpec(
        num_scalar_prefetch=4,
        grid=(n_rows // tm, nf),
        in_specs=[
            pl.BlockSpec((tm, d), lambda i, j, kind, xm, we, fm: (xm[i], 0)),
            pl.BlockSpec((1, d, tf), lambda i, j, kind, xm, we, fm: (we[i], 0, wj(i, j, kind))),
            pl.BlockSpec((1, d, tf), lambda i, j, kind, xm, we, fm: (we[i], 0, wj(i, j, kind))),
            pl.BlockSpec((1, tf, d), lambda i, j, kind, xm, we, fm: (we[i], wj(i, j, kind), 0)),
            pl.BlockSpec((tm, d), lambda i, j, kind, xm, we, fm: (fm[i], 0)),
        ],
        out_specs=pl.BlockSpec((tm, d), lambda i, j, kind, xm, we, fm: (i, 0)),
        scratch_shapes=[pltpu.VMEM((tm, d), BF16)],
    )
    return pl.pallas_call(
        _expert_kernel,
        grid_spec=grid_spec,
        out_shape=jax.ShapeDtypeStruct((n_rows, d), F32),
        compiler_params=_params(("arbitrary", "arbitrary"), 52),
        name="moe_experts",
    )(tables["kind"], tables["x_map"], tables["w_expert"], tables["first_map"],
      xs, wg, wu, wd, y_first)


def _expert_first_kernel(fb_ref, has_ref, xs_ref, wg_ref, wu_ref, wd_ref,
                         ys_ref, wgb_ref, wub_ref, wdb_ref, xb_scr):
    del fb_ref
    e = pl.program_id(0)
    j = pl.program_id(1)
    wg = wg_ref[0].astype(BF16)
    wu = wu_ref[0].astype(BF16)
    wd = wd_ref[0].astype(BF16)
    wgb_ref[0] = wg
    wub_ref[0] = wu
    wdb_ref[0] = wd

    @pl.when(j == 0)
    def _():
        xb_scr[...] = xs_ref[...].astype(BF16)
        ys_ref[...] = jnp.zeros(ys_ref.shape, F32)

    @pl.when(has_ref[e] > 0)
    def _():
        x = xb_scr[...]
        act = _silu(_dot(x, wg)) * _dot(x, wu)
        ys_ref[...] += _dot(act.astype(BF16), wd)


def _expert_first_call(xs, tables, wg, wu, wd, tm, tf=256):
    d = xs.shape[1]
    n_experts, _, f = wg.shape
    grid_spec = pltpu.PrefetchScalarGridSpec(
        num_scalar_prefetch=2,
        grid=(n_experts, f // tf),
        in_specs=[
            pl.BlockSpec((tm, d), lambda e, j, fb, has: (fb[e], 0)),
            pl.BlockSpec((1, d, tf), lambda e, j, fb, has: (e, 0, j)),
            pl.BlockSpec((1, d, tf), lambda e, j, fb, has: (e, 0, j)),
            pl.BlockSpec((1, tf, d), lambda e, j, fb, has: (e, j, 0)),
        ],
        out_specs=[
            pl.BlockSpec((tm, d), lambda e, j, fb, has: (e, 0)),
            pl.BlockSpec((1, d, tf), lambda e, j, fb, has: (e, 0, j)),
            pl.BlockSpec((1, d, tf), lambda e, j, fb, has: (e, 0, j)),
            pl.BlockSpec((1, tf, d), lambda e, j, fb, has: (e, j, 0)),
        ],
        scratch_shapes=[pltpu.VMEM((tm, d), BF16)],
    )
    return pl.pallas_call(
        _expert_first_kernel,
        grid_spec=grid_spec,
        out_shape=[
            jax.ShapeDtypeStruct((n_experts * tm, d), F32),
            jax.ShapeDtypeStruct(wg.shape, BF16),
            jax.ShapeDtypeStruct(wu.shape, BF16),
            jax.ShapeDtypeStruct(wd.shape, BF16),
        ],
        compiler_params=_params(("arbitrary", "arbitrary"), 52),
        name="moe_experts_first",
    )(tables["first_block"], tables["has_rows"], xs, wg, wu, wd)


def _combine_kernel(dest_ref, dest_next_ref, ys_ref, x_ref, wt_ref, g_ref, o_ref, ybuf, sems):
    tm = x_ref.shape[0]
    i = pl.program_id(0)
    slot = i % 2

    def start_gathers(d_ref, dst_slot):
        def issue(r, c):
            for k in range(EXPERT_TOPK):
                d = d_ref[0, 0, EXPERT_TOPK * r + k]
                pltpu.make_async_copy(ys_ref.at[pl.ds(d, 1), :],
                                      ybuf.at[dst_slot, k, pl.ds(r, 1), :],
                                      sems.at[dst_slot]).start(priority=k % 2)
            return c

        lax.fori_loop(0, tm, issue, 0, unroll=ROW_DMA_UNROLL)

    @pl.when(i == 0)
    def _():
        start_gathers(dest_ref, 0)

    @pl.when(i + 1 < pl.num_programs(0))
    def _():
        start_gathers(dest_next_ref, 1 - slot)

    for k in range(EXPERT_TOPK):
        pltpu.make_async_copy(ys_ref.at[pl.ds(0, tm), :], ybuf.at[slot, k], sems.at[slot]).wait()
    wt = wt_ref[...]
    y = wt[:, 0:1] * ybuf[slot, 0]
    for k in range(1, EXPERT_TOPK):
        y = y + wt[:, k:k + 1] * ybuf[slot, k]
    o_ref[...] = x_ref[...] + g_ref[0] * y


def _combine_call(ys, dest, x, wt, modr, row, g_chunk, seq, tm=256):
    t, d = x.shape
    nt = t // tm
    rpb = seq // tm
    dest3 = dest.reshape(nt, 1, EXPERT_TOPK * tm)
    return pl.pallas_call(
        _combine_kernel,
        grid=(nt,),
        in_specs=[
            pl.BlockSpec((1, 1, EXPERT_TOPK * tm), lambda i: (i, 0, 0), memory_space=pltpu.SMEM),
            pl.BlockSpec((1, 1, EXPERT_TOPK * tm), lambda i: (jnp.minimum(i + 1, nt - 1), 0, 0),
                         memory_space=pltpu.SMEM),
            pl.BlockSpec(memory_space=pl.ANY),
            pl.BlockSpec((tm, d), lambda i: (i, 0)),
            pl.BlockSpec((tm, LANES), lambda i: (i, 0)),
            _mod_spec(d, rpb, row, g_chunk),
        ],
        out_specs=pl.BlockSpec((tm, d), lambda i: (i, 0)),
        out_shape=jax.ShapeDtypeStruct((t, d), F32),
        scratch_shapes=[pltpu.VMEM((2, EXPERT_TOPK, tm, d), F32), pltpu.SemaphoreType.DMA((2,))],
        compiler_params=_params(("arbitrary",), 40),
        name="moe_combine",
    )(dest3, dest3, ys, x, wt, modr)


def _routing_tables(top_idx, n_experts, tm, n_blocks):
    flat_e = top_idx.reshape(-1)
    onehot = (flat_e[:, None] == jnp.arange(n_experts, dtype=jnp.int32)[None, :]).astype(jnp.int32)
    csum = jnp.cumsum(onehot, axis=0)
    counts = csum[-1]
    pos = jnp.sum((csum - 1) * onehot, axis=1)
    nblk = (counts + tm - 1) // tm
    blk_end = jnp.cumsum(nblk)
    blk_start = blk_end - nblk
    dest = (blk_start[flat_e] * tm + pos).astype(jnp.int32)
    n_used = blk_end[-1]

    blocks = jnp.arange(n_blocks, dtype=jnp.int32)
    x_map = jnp.minimum(blocks, n_used - 1)
    blk_e = jnp.minimum(jnp.sum(blk_end[None, :] <= x_map[:, None], axis=1), n_experts - 1)
    is_first = (blocks == blk_start[blk_e]) & (blocks < n_used)
    kind = jnp.where(blocks >= n_used, BLOCK_UNUSED,
                     jnp.where(is_first, BLOCK_FIRST, BLOCK_COMPUTE))
    last_compute = lax.cummax(jnp.where(kind == BLOCK_COMPUTE, blocks, -1))
    w_expert = blk_e[jnp.maximum(last_compute, 0)]
    first_map = lax.cummax(jnp.where(kind == BLOCK_FIRST, blk_e, 0))
    i32 = lambda a: a.astype(jnp.int32)
    tables = dict(kind=i32(kind), x_map=i32(x_map), w_expert=i32(w_expert),
                  first_map=i32(first_map),
                  first_block=i32(jnp.minimum(blk_start, n_blocks - 1)),
                  has_rows=i32(nblk > 0))
    return dest, tables


def _rope_tables(seq, head_dim, scale=1.0):
    half = head_dim // 2
    inv_freq = jnp.exp(-math.log(ROPE_THETA) * jnp.arange(half, dtype=F32) / half)
    ang = jnp.arange(seq).astype(F32)[:, None] * inv_freq[None, :]
    cos, sin = jnp.cos(ang) * scale, jnp.sin(ang) * scale
    return (jnp.concatenate([cos, cos, cos, cos], axis=-1),
            jnp.concatenate([-sin, sin, -sin, sin], axis=-1))


def kernel(x, c, mod_w, mod_b, norm_mix, norm_ffn, conv_in, conv_w, conv_out, ffn_gate, ffn_up,
           ffn_down, qkv_w, q_norm, k_norm, attn_out, router_w, router_b, exp_gate, exp_up,
           exp_down):
    bsz, seq, d = x.shape
    depth = mod_w.shape[0]
    assert depth == 2, "layer 0 = short-conv + dense FFN, layer 1 = MoBA + MoE"
    t = bsz * seq
    head_dim = d // N_HEADS
    n_experts = router_w.shape[-1]

    mod = _mod_call(c, mod_w, mod_b)
    modr = mod.reshape(depth * bsz, 1, 6 * d)
    xt = x.reshape(t, d)

    row = bsz
    h = _normmod_call(xt, norm_mix[0], modr, 0, 1, 0, seq)
    bu = _convin_call(h, conv_in[0].astype(BF16), conv_w[0], seq)
    xt, h = _resmm_call(bu, conv_out[0].astype(BF16), xt, modr, 0, 2, seq,
                        next_norm=(norm_ffn[0], 0, 4, 3))
    xt, h = _ffn_call(h, ffn_gate[0].astype(BF16), ffn_up[0].astype(BF16),
                      ffn_down[0].astype(BF16), xt, modr, 0, 5, seq,
                      next_norm=(norm_mix[1], row, 1, 0))

    w_qkv = qkv_w[0].astype(BF16)
    cos_q, sin_q = _rope_tables(seq, head_dim, head_dim ** -0.5 * math.log2(math.e))
    cos_k, sin_k = _rope_tables(seq, head_dim)
    q = _qkv_call(h, w_qkv, 0, seq, head_dim, q_norm[0], cos_q, sin_q)
    k = _qkv_call(h, w_qkv, 1, seq, head_dim, k_norm[0], cos_k, sin_k)
    v = _qkv_call(h, w_qkv, 2, seq, head_dim)
    o = _attn_call(q, k, v, seq, head_dim)
    xt = _resmm_call(o, attn_out[0].astype(BF16), xt, modr, row, 2, seq)

    hf, top_idx, top_w = _router_call(xt, norm_ffn[1], modr, row, 4, 3,
                                      router_w[0], router_b[0], seq)
    n_blocks = (t * EXPERT_TOPK) // EXPERT_ROWS + n_experts
    dest, tables = _routing_tables(top_idx[:, :EXPERT_TOPK], n_experts, EXPERT_ROWS, n_blocks)
    xs = _dispatch_call(hf, dest, n_blocks * EXPERT_ROWS)
    y_first, wg, wu, wd = _expert_first_call(xs, tables, exp_gate[0], exp_up[0], exp_down[0],
                                             EXPERT_ROWS)
    ys = _expert_call(xs, tables, y_first, wg, wu, wd, EXPERT_ROWS)
    xt = _combine_call(ys, dest, xt, top_w, modr, row, 5, seq)
    return xt.reshape(bsz, seq, d)
```

```python
import functools
import math

import jax
import jax.numpy as jnp
from jax import lax
from jax.experimental import pallas as pl
from jax.experimental.pallas import tpu as pltpu

N_HEADS = 16
CONV_WIDTH = 3
MOBA_BLOCK = 256
MOBA_TOPK = 3
ROPE_THETA = 10000.0
EXPERT_TOPK = 2
EXPERT_ROWS = 512
ROW_DMA_UNROLL = 8
NORM_EPS = 1e-6
NEG_INF = -1e30

LANES = 128
SUBLANES = 8
BF16_SUBLANES = 16
MIB = 1024 * 1024

F32 = jnp.float32
BF16 = jnp.bfloat16

_NT_DIMS = (((1,), (1,)), ((), ()))


def _params(semantics, vmem_mib):
    return pltpu.CompilerParams(dimension_semantics=semantics,
                                vmem_limit_bytes=vmem_mib * MIB)


def _dot(a, b):
    return jnp.dot(a, b, preferred_element_type=F32)


def _silu(a):
    return a * (1.0 / (1.0 + jnp.exp(-a)))


def _mod_kernel(c_ref, w_ref, b_ref, o_ref):
    ca = _silu(c_ref[...])
    o_ref[0] = _dot(ca.astype(BF16), w_ref[0].astype(BF16)) + b_ref[0]


def _mod_call(c, mod_w, mod_b, tn=1024):
    depth, d, n = mod_w.shape
    bsz = c.shape[0]
    return pl.pallas_call(
        _mod_kernel,
        grid=(depth, n // tn),
        in_specs=[
            pl.BlockSpec((bsz, d), lambda l, j: (0, 0)),
            pl.BlockSpec((1, d, tn), lambda l, j: (l, 0, j)),
            pl.BlockSpec((1, 1, tn), lambda l, j: (l, 0, j)),
        ],
        out_specs=pl.BlockSpec((1, bsz, tn), lambda l, j: (l, 0, j)),
        out_shape=jax.ShapeDtypeStruct((depth, bsz, n), F32),
        compiler_params=_params(("arbitrary", "arbitrary"), 40),
        name="adaln_mod",
    )(c, mod_w, mod_b.reshape(depth, 1, n))


def _normmod(x, gain, scale, shift):
    ms = jnp.mean(x * x, axis=-1, keepdims=True)
    y = x * lax.rsqrt(ms + NORM_EPS) * gain
    return y * (1.0 + scale) + shift


def _normmod_kernel(x_ref, gain_ref, sc_ref, sh_ref, o_ref):
    o_ref[...] = _normmod(x_ref[...], gain_ref[...], sc_ref[0], sh_ref[0]).astype(o_ref.dtype)


def _mod_spec(d, rows_per_batch, row, chunk):
    return pl.BlockSpec((1, 1, d), lambda i, *_: (row + i // rows_per_batch, 0, chunk))


def _normmod_call(x, gain, modr, row, sc_chunk, sh_chunk, seq, tm=512):
    t, d = x.shape
    rpb = seq // tm
    return pl.pallas_call(
        _normmod_kernel,
        grid=(t // tm,),
        in_specs=[
            pl.BlockSpec((tm, d), lambda i: (i, 0)),
            pl.BlockSpec((1, d), lambda i: (0, 0)),
            _mod_spec(d, rpb, row, sc_chunk),
            _mod_spec(d, rpb, row, sh_chunk),
        ],
        out_specs=pl.BlockSpec((tm, d), lambda i: (i, 0)),
        out_shape=jax.ShapeDtypeStruct((t, d), BF16),
        compiler_params=_params(("arbitrary",), 32),
        name="normmod",
    )(x, gain.reshape(1, d), modr, modr)


def _convin_kernel(h_ref, wb_ref, wc_ref, wv_ref, cw_ref, o_ref, u_scr):
    seq = h_ref.shape[0]
    h = h_ref[...]
    u = _dot(h, wc_ref[...]) * _dot(h, wv_ref[...])
    u_scr[0:SUBLANES, :] = jnp.zeros((SUBLANES, u.shape[1]), F32)
    u_scr[SUBLANES:, :] = u
    cw = cw_ref[...]
    conv = cw[CONV_WIDTH - 1:CONV_WIDTH, :] * u
    for tap in range(1, CONV_WIDTH):
        conv = conv + (cw[CONV_WIDTH - 1 - tap:CONV_WIDTH - tap, :]
                       * u_scr[SUBLANES - tap:SUBLANES - tap + seq, :])
    o_ref[...] = (_dot(h, wb_ref[...]) * conv).astype(o_ref.dtype)


def _convin_call(h, w_in, conv_w, seq, tn=256):
    t, d = h.shape
    nj = d // tn
    return pl.pallas_call(
        _convin_kernel,
        grid=(nj, t // seq),
        in_specs=[
            pl.BlockSpec((seq, d), lambda j, b: (b, 0)),
            pl.BlockSpec((d, tn), lambda j, b: (0, j)),
            pl.BlockSpec((d, tn), lambda j, b: (0, j + nj)),
            pl.BlockSpec((d, tn), lambda j, b: (0, j + 2 * nj)),
            pl.BlockSpec((CONV_WIDTH, tn), lambda j, b: (0, j)),
        ],
        out_specs=pl.BlockSpec((seq, tn), lambda j, b: (b, j)),
        out_shape=jax.ShapeDtypeStruct((t, d), BF16),
        scratch_shapes=[pltpu.VMEM((seq + SUBLANES, tn), F32)],
        compiler_params=_params(("arbitrary", "arbitrary"), 48),
        name="conv_in",
    )(h, w_in, w_in, w_in, conv_w)


def _resmm_kernel(a_ref, w_ref, x_ref, g_ref, o_ref):
    o_ref[...] = x_ref[...] + g_ref[0] * _dot(a_ref[...], w_ref[...])


def _resmm_norm_kernel(a_ref, w_ref, x_ref, g_ref, gain_ref, sc_ref, sh_ref, o_ref, h_ref):
    xn = x_ref[...] + g_ref[0] * _dot(a_ref[...], w_ref[...])
    o_ref[...] = xn
    h_ref[...] = _normmod(xn, gain_ref[...], sc_ref[0], sh_ref[0]).astype(h_ref.dtype)


def _resmm_call(a, w, x, modr, row, g_chunk, seq, next_norm=None, tm=512):
    t, k = a.shape
    d = w.shape[1]
    rpb = seq // tm
    in_specs = [
        pl.BlockSpec((tm, k), lambda i: (i, 0)),
        pl.BlockSpec((k, d), lambda i: (0, 0)),
        pl.BlockSpec((tm, d), lambda i: (i, 0)),
        _mod_spec(d, rpb, row, g_chunk),
    ]
    row_spec = pl.BlockSpec((tm, d), lambda i: (i, 0))
    cp = _params(("arbitrary",), 52)
    if next_norm is None:
        return pl.pallas_call(
            _resmm_kernel, grid=(t // tm,), in_specs=in_specs, out_specs=row_spec,
            out_shape=jax.ShapeDtypeStruct((t, d), F32), compiler_params=cp,
            name="res_matmul")(a, w, x, modr)
    gain, nrow, sc_chunk, sh_chunk = next_norm
    in_specs += [pl.BlockSpec((1, d), lambda i: (0, 0)),
                 _mod_spec(d, rpb, nrow, sc_chunk), _mod_spec(d, rpb, nrow, sh_chunk)]
    return pl.pallas_call(
        _resmm_norm_kernel, grid=(t // tm,), in_specs=in_specs, out_specs=[row_spec, row_spec],
        out_shape=[jax.ShapeDtypeStruct((t, d), F32), jax.ShapeDtypeStruct((t, d), BF16)],
        compiler_params=cp, name="res_matmul_norm")(a, w, x, modr, gain.reshape(1, d), modr, modr)


def _ffn_kernel(h_ref, wg_ref, wu_ref, wd_ref, x_ref, g_ref, gain_ref, sc_ref, sh_ref,
                o_ref, hn_ref):
    j = pl.program_id(1)
    h = h_ref[...]
    act = _silu(_dot(h, wg_ref[...])) * _dot(h, wu_ref[...])
    part = _dot(act.astype(BF16), wd_ref[...])

    @pl.when(j == 0)
    def _():
        o_ref[...] = part

    @pl.when(j > 0)
    def _():
        o_ref[...] += part

    @pl.when(j == pl.num_programs(1) - 1)
    def _():
        xn = x_ref[...] + g_ref[0] * o_ref[...]
        o_ref[...] = xn
        hn_ref[...] = _normmod(xn, gain_ref[...], sc_ref[0], sh_ref[0]).astype(hn_ref.dtype)


def _ffn_call(h, wg, wu, wd, x, modr, row, g_chunk, seq, next_norm, tm=512, tf=512):
    t, d = h.shape
    f = wg.shape[1]
    rpb = seq // tm
    gain, nrow, sc_chunk, sh_chunk = next_norm
    row_spec = pl.BlockSpec((tm, d), lambda i, j: (i, 0))
    return pl.pallas_call(
        _ffn_kernel,
        grid=(t // tm, f // tf),
        in_specs=[
            row_spec,
            pl.BlockSpec((d, tf), lambda i, j: (0, j)),
            pl.BlockSpec((d, tf), lambda i, j: (0, j)),
            pl.BlockSpec((tf, d), lambda i, j: (j, 0)),
            row_spec,
            _mod_spec(d, rpb, row, g_chunk),
            pl.BlockSpec((1, d), lambda i, j: (0, 0)),
            _mod_spec(d, rpb, nrow, sc_chunk),
            _mod_spec(d, rpb, nrow, sh_chunk),
        ],
        out_specs=[row_spec, row_spec],
        out_shape=[jax.ShapeDtypeStruct((t, d), F32), jax.ShapeDtypeStruct((t, d), BF16)],
        compiler_params=_params(("arbitrary", "arbitrary"), 52),
        name="dense_ffn",
    )(h, wg, wu, wd, x, modr, gain.reshape(1, d), modr, modr)


def _qk_kernel(h_ref, w_ref, gain_ref, cos_ref, sin_ref, ones_ref, perm_ref, o_ref, *, head_dim):
    acc = _dot(h_ref[...], w_ref[...])
    pair = ones_ref.shape[0]
    gain = gain_ref[...]
    cos = cos_ref[...]
    sin = sin_ref[...]
    ones = ones_ref[...]
    perm = perm_ref[...]
    for p0 in range(0, acc.shape[1], pair):
        xp = acc[:, p0:p0 + pair]
        sq = xp * xp
        sq_a = sq.astype(BF16)
        sq_b = (sq - sq_a.astype(F32)).astype(BF16)
        ss = _dot(sq_a, ones) + _dot(sq_b, ones)
        inv = lax.rsqrt(ss * (1.0 / head_dim) + NORM_EPS)
        z = xp * gain
        z_a = z.astype(BF16)
        z_b = (z - z_a.astype(F32)).astype(BF16)
        rot = _dot(z_a, perm) + _dot(z_b, perm)
        out = (z * cos + rot * sin) * inv
        o_ref[:, p0:p0 + pair] = out.astype(o_ref.dtype)


def _v_kernel(h_ref, w_ref, o_ref):
    o_ref[...] = _dot(h_ref[...], w_ref[...]).astype(o_ref.dtype)


def _head_pair_matrices(head_dim):
    pair = 2 * head_dim
    src = jnp.arange(pair)[:, None]
    dst = jnp.arange(pair)[None, :]
    same_head = (src // head_dim) == (dst // head_dim)
    ones = same_head.astype(BF16)
    perm = (same_head & ((src % head_dim) == ((dst + head_dim // 2) % head_dim))).astype(BF16)
    return ones, perm


def _qkv_call(h, w_qkv, which, seq, head_dim, gain=None, cos=None, sin=None, tm=1024, tn=512):
    t, d = h.shape
    nj = d // tn
    off = which * nj
    rpb = seq // tm
    h_spec = pl.BlockSpec((tm, d), lambda j, i: (i, 0))
    w_spec = pl.BlockSpec((d, tn), lambda j, i: (0, j + off))
    out_spec = pl.BlockSpec((tm, tn), lambda j, i: (i, j))
    out_shape = jax.ShapeDtypeStruct((t, d), BF16)
    cp = _params(("arbitrary", "arbitrary"), 40)
    if gain is None:
        return pl.pallas_call(_v_kernel, grid=(nj, t // tm), in_specs=[h_spec, w_spec],
                              out_specs=out_spec, out_shape=out_shape, compiler_params=cp,
                              name="v_proj")(h, w_qkv)
    pair = 2 * head_dim
    ones, perm = _head_pair_matrices(head_dim)
    tab_spec = pl.BlockSpec((tm, pair), lambda j, i: (i % rpb, 0))
    const_spec = pl.BlockSpec((pair, pair), lambda j, i: (0, 0))
    return pl.pallas_call(
        functools.partial(_qk_kernel, head_dim=head_dim),
        grid=(nj, t // tm),
        in_specs=[h_spec, w_spec, pl.BlockSpec((1, pair), lambda j, i: (0, 0)),
                  tab_spec, tab_spec, const_spec, const_spec],
        out_specs=out_spec, out_shape=out_shape, compiler_params=cp,
        name="qk_proj",
    )(h, w_qkv, jnp.tile(gain.reshape(1, head_dim), (1, 2)), cos, sin, ones, perm)


def _attn_kernel(q_ref, k_ref, v_ref, o_ref, vt_scr, s_scr):
    seq, hd = q_ref.shape
    blk = MOBA_BLOCK
    nb = seq // blk

    rows = []
    for n in range(nb):
        vt_scr[n, 0:hd, :] = v_ref[n * blk:(n + 1) * blk, :].astype(F32).T.astype(BF16)
        vt_scr[n, hd:, :] = jnp.ones((vt_scr.shape[1] - hd, blk), BF16)
        kb = k_ref[n * blk:(n + 1) * blk, :].astype(F32)
        rows.append(jnp.sum(kb, axis=0, keepdims=True) * (1.0 / blk))
    km = jnp.concatenate(rows, axis=0)
    km_a = km.astype(BF16)
    rem = km - km_a.astype(F32)
    km_b = rem.astype(BF16)
    km_c = (rem - km_b.astype(F32)).astype(BF16)
    km_parts = jnp.concatenate([km_a, km_b, km_c, jnp.zeros_like(km_a)], axis=0)

    causal = (lax.broadcasted_iota(jnp.int32, (blk, blk), 0)
              <= lax.broadcasted_iota(jnp.int32, (blk, blk), 1))

    def masked_scores(qi):
        qb = q_ref[qi * blk:(qi + 1) * blk, :]
        sel = None
        if qi > MOBA_TOPK:
            g3 = lax.dot_general(km_parts, qb, _NT_DIMS, preferred_element_type=F32)
            gate = g3[0:nb] + g3[nb:2 * nb] + g3[2 * nb:3 * nb]
            g_row = [gate[n:n + 1, :] for n in range(qi)]
            rank = [jnp.zeros((1, blk), F32) for _ in range(qi)]
            for a in range(qi):
                for b in range(a + 1, qi):
                    a_first = (g_row[a] >= g_row[b]).astype(F32)
                    rank[b] = rank[b] + a_first
                    rank[a] = rank[a] + (1.0 - a_first)
            sel = [rank[n] < MOBA_TOPK for n in range(qi)]

        m_run = None
        for n in range(qi + 1):
            s = lax.dot_general(k_ref[n * blk:(n + 1) * blk, :], qb, _NT_DIMS,
                                preferred_element_type=F32)
            if n == qi:
                s = jnp.where(causal, s, NEG_INF)
            elif sel is not None:
                s = jnp.where(sel[n], s, NEG_INF)
            s_scr[qi % 2, n] = s
            cm = jnp.max(s, axis=0, keepdims=True)
            m_run = cm if m_run is None else jnp.maximum(m_run, cm)
        return m_run

    def weighted_values(qi, m_run):
        acc = None
        for n in range(qi + 1):
            p = jnp.exp2(s_scr[qi % 2, n] - m_run)
            pv = _dot(vt_scr[n], p.astype(BF16))
            acc = pv if acc is None else acc + pv
        out_t = acc[0:hd] * (1.0 / acc[hd:hd + 1])
        o_ref[qi * blk:(qi + 1) * blk, :] = out_t.T.astype(o_ref.dtype)

    m_next = masked_scores(0)
    for qi in range(nb):
        m_cur = m_next
        if qi + 1 < nb:
            m_next = masked_scores(qi + 1)
        weighted_values(qi, m_cur)


def _attn_call(q, k, v, seq, head_dim):
    t, d = q.shape
    nh = d // head_dim
    nb = seq // MOBA_BLOCK
    spec = pl.BlockSpec((seq, head_dim), lambda b, h: (b, h))
    return pl.pallas_call(
        _attn_kernel,
        grid=(t // seq, nh),
        in_specs=[spec, spec, spec],
        out_specs=spec,
        out_shape=jax.ShapeDtypeStruct((t, d), BF16),
        scratch_shapes=[pltpu.VMEM((nb, head_dim + BF16_SUBLANES, MOBA_BLOCK), BF16),
                        pltpu.VMEM((2, nb, MOBA_BLOCK, MOBA_BLOCK), F32)],
        compiler_params=_params(("arbitrary", "arbitrary"), 32),
        name="moba_attention",
    )(q, k, v)


def _router_kernel(x_ref, gain_ref, sc_ref, sh_ref, wr_ref, br_ref, h_ref, idx_ref, wt_ref,
                   *, n_experts):
    h = _normmod(x_ref[...], gain_ref[...], sc_ref[0], sh_ref[0])
    h_ref[...] = h
    h_a = h.astype(BF16)
    h_b = (h - h_a.astype(F32)).astype(BF16)
    w = wr_ref[...]
    w_a = w.astype(BF16)
    w_b = (w - w_a.astype(F32)).astype(BF16)
    logits = _dot(h_a, w_a) + _dot(h_b, w_a) + _dot(h_a, w_b) + br_ref[...]
    lane = lax.broadcasted_iota(jnp.int32, logits.shape, 1)
    logits = jnp.where(lane < n_experts, logits, -jnp.inf)
    m1 = jnp.max(logits, axis=-1, keepdims=True)
    i1 = jnp.min(jnp.where(logits == m1, lane, LANES), axis=-1, keepdims=True)
    rest = jnp.where(lane == i1, -jnp.inf, logits)
    m2 = jnp.max(rest, axis=-1, keepdims=True)
    i2 = jnp.min(jnp.where(rest == m2, lane, LANES), axis=-1, keepdims=True)
    e2 = jnp.exp(m2 - m1)
    w1 = 1.0 / (1.0 + e2)
    w2 = e2 * w1
    idx_ref[...] = jnp.where(lane == 0, i1, jnp.where(lane == 1, i2, 0))
    wt_ref[...] = jnp.where(lane == 0, w1, jnp.where(lane == 1, w2, 0.0))


def _router_call(x, gain, modr, row, sc_chunk, sh_chunk, w_router, b_router, seq, tm=512):
    t, d = x.shape
    n_experts = w_router.shape[1]
    rpb = seq // tm
    wr = jnp.zeros((d, LANES), F32).at[:, :n_experts].set(w_router)
    br = jnp.zeros((1, LANES), F32).at[0, :n_experts].set(b_router)
    return pl.pallas_call(
        functools.partial(_router_kernel, n_experts=n_experts),
        grid=(t // tm,),
        in_specs=[
            pl.BlockSpec((tm, d), lambda i: (i, 0)),
            pl.BlockSpec((1, d), lambda i: (0, 0)),
            _mod_spec(d, rpb, row, sc_chunk),
            _mod_spec(d, rpb, row, sh_chunk),
            pl.BlockSpec((d, LANES), lambda i: (0, 0)),
            pl.BlockSpec((1, LANES), lambda i: (0, 0)),
        ],
        out_specs=[
            pl.BlockSpec((tm, d), lambda i: (i, 0)),
            pl.BlockSpec((tm, LANES), lambda i: (i, 0)),
            pl.BlockSpec((tm, LANES), lambda i: (i, 0)),
        ],
        out_shape=[
            jax.ShapeDtypeStruct((t, d), F32),
            jax.ShapeDtypeStruct((t, LANES), jnp.int32),
            jax.ShapeDtypeStruct((t, LANES), F32),
        ],
        compiler_params=_params(("arbitrary",), 40),
        name="moe_router",
    )(x, gain.reshape(1, d), modr, modr, wr, br)


def _dispatch_kernel(zb_ref, nz_ref, dest_ref, h_ref, xs_ref, zbuf, sem, zsem):
    tm = h_ref.shape[0]
    zrows = zbuf.shape[0]

    @pl.when(pl.program_id(0) == 0)
    def _():
        zbuf[...] = jnp.zeros(zbuf.shape, zbuf.dtype)

        def zero_copy(k):
            row0 = pl.multiple_of(zb_ref[k] * zrows, zrows)
            return pltpu.make_async_copy(zbuf, xs_ref.at[pl.ds(row0, zrows), :], zsem)

        def start(k, c):
            zero_copy(k).start()
            return c

        def wait(k, c):
            zero_copy(k).wait()
            return c

        lax.fori_loop(0, nz_ref[0], start, 0)
        lax.fori_loop(0, nz_ref[0], wait, 0)

    def issue(r, c):
        for k in range(EXPERT_TOPK):
            d = dest_ref[0, 0, EXPERT_TOPK * r + k]
            pltpu.make_async_copy(h_ref.at[pl.ds(r, 1), :], xs_ref.at[pl.ds(d, 1), :],
                                  sem).start(priority=k % 2)
        return c

    lax.fori_loop(0, tm, issue, 0, unroll=ROW_DMA_UNROLL)
    for k in range(EXPERT_TOPK):
        pltpu.make_async_copy(h_ref, xs_ref.at[pl.ds(0, tm), :], sem).wait()


def _dispatch_call(h, dest, tables, n_rows, block_rows, tm=256):
    t, d = h.shape
    nt = t // tm
    grid_spec = pltpu.PrefetchScalarGridSpec(
        num_scalar_prefetch=2,
        grid=(nt,),
        in_specs=[
            pl.BlockSpec((1, 1, EXPERT_TOPK * tm), lambda i, zb, nz: (i, 0, 0),
                         memory_space=pltpu.SMEM),
            pl.BlockSpec((tm, d), lambda i, zb, nz: (i, 0)),
        ],
        out_specs=pl.BlockSpec(memory_space=pl.ANY),
        scratch_shapes=[pltpu.VMEM((block_rows, d), h.dtype), pltpu.SemaphoreType.DMA,
                        pltpu.SemaphoreType.DMA],
    )
    return pl.pallas_call(
        _dispatch_kernel,
        grid_spec=grid_spec,
        out_shape=jax.ShapeDtypeStruct((n_rows, d), h.dtype),
        compiler_params=_params(("arbitrary",), 32),
        name="moe_dispatch",
    )(tables["zero_blocks"], tables["n_zero"], dest.reshape(nt, 1, EXPERT_TOPK * tm), h)


BLOCK_COMPUTE, BLOCK_TAIL, BLOCK_UNUSED = 0, 1, 2


def _expert_kernel(kind_ref, xmap_ref, we_ref, tmap_ref, xs_ref, wg_ref, wu_ref, wd_ref,
                   ytail_ref, ys_ref, xb_scr):
    del xmap_ref, we_ref, tmap_ref
    i = pl.program_id(0)
    j = pl.program_id(1)
    kind = kind_ref[i]

    @pl.when((kind == BLOCK_UNUSED) & (j == 0))
    def _():
        ys_ref[...] = jnp.zeros(ys_ref.shape, F32)

    @pl.when((kind == BLOCK_TAIL) & (j == 0))
    def _():
        ys_ref[...] = ytail_ref[...]

    @pl.when(kind == BLOCK_COMPUTE)
    def _():
        @pl.when(j == 0)
        def _():
            xb_scr[...] = xs_ref[...].astype(BF16)
            ys_ref[...] = jnp.zeros(ys_ref.shape, F32)

        x = xb_scr[...]
        act = _silu(_dot(x, wg_ref[0])) * _dot(x, wu_ref[0])
        ys_ref[...] += _dot(act.astype(BF16), wd_ref[0])


def _expert_call(xs, tables, y_tail, wg, wu, wd, tm, tf=512):
    n_rows, d = xs.shape
    f = wg.shape[2]
    nf = f // tf

    def wj(i, j, kind):
        return jnp.where(kind[i] == BLOCK_COMPUTE, j, nf - 1)

    grid_spec = pltpu.PrefetchScalarGridSpec(
        num_scalar_prefetch=4,
        grid=(n_rows // tm, nf),
        in_specs=[
            pl.BlockSpec((tm, d), lambda i, j, kind, xm, we, fm: (xm[i], 0)),
            pl.BlockSpec((1, d, tf), lambda i, j, kind, xm, we, fm: (we[i], 0, wj(i, j, kind))),
            pl.BlockSpec((1, d, tf), lambda i, j, kind, xm, we, fm: (we[i], 0, wj(i, j, kind))),
            pl.BlockSpec((1, tf, d), lambda i, j, kind, xm, we, fm: (we[i], wj(i, j, kind), 0)),
            pl.BlockSpec((tm, d), lambda i, j, kind, xm, we, fm: (fm[i], 0)),
        ],
        out_specs=pl.BlockSpec((tm, d), lambda i, j, kind, xm, we, fm: (i, 0)),
        scratch_shapes=[pltpu.VMEM((tm, d), BF16)],
    )
    return pl.pallas_call(
        _expert_kernel,
        grid_spec=grid_spec,
        out_shape=jax.ShapeDtypeStruct((n_rows, d), F32),
        compiler_params=_params(("arbitrary", "arbitrary"), 52),
        name="moe_experts",
    )(tables["kind"], tables["x_map"], tables["w_expert"], tables["tail_map"],
      xs, wg, wu, wd, y_tail)


TAIL_BLOCKS = 2


def _expert_tail_kernel(tb_ref, nt_ref, *refs):
    del tb_ref
    xs_refs = refs[:TAIL_BLOCKS]
    wg_ref, wu_ref, wd_ref, ys_ref, wgb_ref, wub_ref, wdb_ref, xb_scr = refs[TAIL_BLOCKS:]
    tm = xs_refs[0].shape[0]
    e = pl.program_id(0)
    j = pl.program_id(1)

    @pl.when(j == 0)
    def _():
        for k, xs_ref in enumerate(xs_refs):
            xb_scr[k * tm:(k + 1) * tm, :] = xs_ref[...].astype(BF16)
        ys_ref[...] = jnp.zeros(ys_ref.shape, F32)

    x = xb_scr[...]
    wg = wg_ref[0].astype(BF16)
    wu = wu_ref[0].astype(BF16)
    act = _silu(_dot(x, wg)) * _dot(x, wu)
    wd = wd_ref[0].astype(BF16)
    part = _dot(act.astype(BF16), wd)
    for k in range(TAIL_BLOCKS):
        rows = slice(k * tm, (k + 1) * tm)
        ys_ref[rows, :] += jnp.where(nt_ref[e] >= TAIL_BLOCKS - k, part[rows, :], 0.0)
    wgb_ref[0] = wg
    wub_ref[0] = wu
    wdb_ref[0] = wd


def _expert_tail_call(xs, tables, wg, wu, wd, tm, tf=256):
    d = xs.shape[1]
    n_experts, _, f = wg.shape

    def xs_spec(k):
        return pl.BlockSpec((tm, d), lambda e, j, tb, nt: (tb[TAIL_BLOCKS * e + k], 0),
                            pipeline_mode=pl.Buffered(1))

    w_in = lambda e, j, tb, nt: (e, 0, j)
    w_out = lambda e, j, tb, nt: (e, j, 0)
    grid_spec = pltpu.PrefetchScalarGridSpec(
        num_scalar_prefetch=2,
        grid=(n_experts, f // tf),
        in_specs=[xs_spec(k) for k in range(TAIL_BLOCKS)] + [
            pl.BlockSpec((1, d, tf), w_in),
            pl.BlockSpec((1, d, tf), w_in),
            pl.BlockSpec((1, tf, d), w_out),
        ],
        out_specs=[
            pl.BlockSpec((TAIL_BLOCKS * tm, d), lambda e, j, tb, nt: (e, 0)),
            pl.BlockSpec((1, d, tf), w_in),
            pl.BlockSpec((1, d, tf), w_in),
            pl.BlockSpec((1, tf, d), w_out),
        ],
        scratch_shapes=[pltpu.VMEM((TAIL_BLOCKS * tm, d), BF16)],
    )
    return pl.pallas_call(
        _expert_tail_kernel,
        grid_spec=grid_spec,
        out_shape=[
            jax.ShapeDtypeStruct((n_experts * TAIL_BLOCKS * tm, d), F32),
            jax.ShapeDtypeStruct(wg.shape, BF16),
            jax.ShapeDtypeStruct(wu.shape, BF16),
            jax.ShapeDtypeStruct(wd.shape, BF16),
        ],
        compiler_params=_params(("arbitrary", "arbitrary"), 58),
        name="moe_experts_tail",
    )(tables["tail_block"], tables["n_tail"], *([xs] * TAIL_BLOCKS), wg, wu, wd)


def _combine_kernel(dest_ref, dest_next_ref, ys_ref, x_ref, wt_ref, g_ref, o_ref, ybuf, sems):
    tm = x_ref.shape[0]
    i = pl.program_id(0)
    slot = i % 2

    def start_gathers(d_ref, dst_slot):
        def issue(r, c):
            for k in range(EXPERT_TOPK):
                d = d_ref[0, 0, EXPERT_TOPK * r + k]
                pltpu.make_async_copy(ys_ref.at[pl.ds(d, 1), :],
                                      ybuf.at[dst_slot, k, pl.ds(r, 1), :],
                                      sems.at[dst_slot]).start(priority=k % 2)
            return c

        lax.fori_loop(0, tm, issue, 0, unroll=ROW_DMA_UNROLL)

    @pl.when(i == 0)
    def _():
        start_gathers(dest_ref, 0)

    @pl.when(i + 1 < pl.num_programs(0))
    def _():
        start_gathers(dest_next_ref, 1 - slot)

    for k in range(EXPERT_TOPK):
        pltpu.make_async_copy(ys_ref.at[pl.ds(0, tm), :], ybuf.at[slot, k], sems.at[slot]).wait()
    wt = wt_ref[...]
    y = wt[:, 0:1] * ybuf[slot, 0]
    for k in range(1, EXPERT_TOPK):
        y = y + wt[:, k:k + 1] * ybuf[slot, k]
    o_ref[...] = x_ref[...] + g_ref[0] * y


def _combine_call(ys, dest, x, wt, modr, row, g_chunk, seq, tm=256):
    t, d = x.shape
    nt = t // tm
    rpb = seq // tm
    dest3 = dest.reshape(nt, 1, EXPERT_TOPK * tm)
    return pl.pallas_call(
        _combine_kernel,
        grid=(nt,),
        in_specs=[
            pl.BlockSpec((1, 1, EXPERT_TOPK * tm), lambda i: (i, 0, 0), memory_space=pltpu.SMEM),
            pl.BlockSpec((1, 1, EXPERT_TOPK * tm), lambda i: (jnp.minimum(i + 1, nt - 1), 0, 0),
                         memory_space=pltpu.SMEM),
            pl.BlockSpec(memory_space=pl.ANY),
            pl.BlockSpec((tm, d), lambda i: (i, 0)),
            pl.BlockSpec((tm, LANES), lambda i: (i, 0)),
            _mod_spec(d, rpb, row, g_chunk),
        ],
        out_specs=pl.BlockSpec((tm, d), lambda i: (i, 0)),
        out_shape=jax.ShapeDtypeStruct((t, d), F32),
        scratch_shapes=[pltpu.VMEM((2, EXPERT_TOPK, tm, d), F32), pltpu.SemaphoreType.DMA((2,))],
        compiler_params=_params(("arbitrary",), 40),
        name="moe_combine",
    )(dest3, dest3, ys, x, wt, modr)


def _routing_tables(top_idx, n_experts, tm, n_blocks):
    flat_e = top_idx.reshape(-1)
    onehot = (flat_e[:, None] == jnp.arange(n_experts, dtype=jnp.int32)[None, :]).astype(jnp.int32)
    csum = jnp.cumsum(onehot, axis=0)
    counts = csum[-1]
    pos = jnp.sum((csum - 1) * onehot, axis=1)
    nblk = (counts + tm - 1) // tm
    blk_end = jnp.cumsum(nblk)
    blk_start = blk_end - nblk
    dest = (blk_start[flat_e] * tm + pos).astype(jnp.int32)
    n_used = blk_end[-1]

    blocks = jnp.arange(n_blocks, dtype=jnp.int32)
    x_map = jnp.minimum(blocks, n_used - 1)
    blk_e = jnp.minimum(jnp.sum(blk_end[None, :] <= x_map[:, None], axis=1), n_experts - 1)
    from_end = blk_end[blk_e] - blocks
    is_tail = (from_end <= TAIL_BLOCKS) & (blocks < n_used)
    kind = jnp.where(blocks >= n_used, BLOCK_UNUSED,
                     jnp.where(is_tail, BLOCK_TAIL, BLOCK_COMPUTE))
    last_compute = lax.cummax(jnp.where(kind == BLOCK_COMPUTE, blocks, -1))
    w_expert = blk_e[jnp.maximum(last_compute, 0)]
    tail_map = lax.cummax(jnp.where(is_tail, TAIL_BLOCKS * (blk_e + 1) - from_end, 0))
    slot_from_end = TAIL_BLOCKS - jnp.arange(TAIL_BLOCKS, dtype=jnp.int32)
    tail_block = jnp.clip(blk_end[:, None] - slot_from_end[None, :], 0, n_blocks - 1)
    cand = jnp.concatenate([blk_end - 1, blocks])
    cand_ok = jnp.concatenate([nblk > 0, blocks >= n_used])
    zero_blocks = jnp.maximum(cand[jnp.argsort(~cand_ok, stable=True)], 0)
    i32 = lambda a: a.astype(jnp.int32)
    tables = dict(kind=i32(kind), x_map=i32(x_map), w_expert=i32(w_expert),
                  tail_map=i32(tail_map), tail_block=i32(tail_block.reshape(-1)),
                  n_tail=i32(jnp.minimum(nblk, TAIL_BLOCKS)),
                  zero_blocks=i32(zero_blocks), n_zero=i32(jnp.sum(cand_ok)).reshape(1))
    return dest, tables


def _rope_tables(seq, head_dim, scale=1.0):
    half = head_dim // 2
    inv_freq = jnp.exp(-math.log(ROPE_THETA) * jnp.arange(half, dtype=F32) / half)
    ang = jnp.arange(seq).astype(F32)[:, None] * inv_freq[None, :]
    cos, sin = jnp.cos(ang) * scale, jnp.sin(ang) * scale
    return (jnp.concatenate([cos, cos, cos, cos], axis=-1),
            jnp.concatenate([-sin, sin, -sin, sin], axis=-1))


def kernel(x, c, mod_w, mod_b, norm_mix, norm_ffn, conv_in, conv_w, conv_out, ffn_gate, ffn_up,
           ffn_down, qkv_w, q_norm, k_norm, attn_out, router_w, router_b, exp_gate, exp_up,
           exp_down):
    bsz, seq, d = x.shape
    depth = mod_w.shape[0]
    assert depth == 2, "layer 0 = short-conv + dense FFN, layer 1 = MoBA + MoE"
    t = bsz * seq
    head_dim = d // N_HEADS
    n_experts = router_w.shape[-1]

    mod = _mod_call(c, mod_w, mod_b)
    modr = mod.reshape(depth * bsz, 1, 6 * d)
    xt = x.reshape(t, d)

    row = bsz
    h = _normmod_call(xt, norm_mix[0], modr, 0, 1, 0, seq)
    bu = _convin_call(h, conv_in[0].astype(BF16), conv_w[0], seq)
    xt, h = _resmm_call(bu, conv_out[0].astype(BF16), xt, modr, 0, 2, seq,
                        next_norm=(norm_ffn[0], 0, 4, 3))
    xt, h = _ffn_call(h, ffn_gate[0].astype(BF16), ffn_up[0].astype(BF16),
                      ffn_down[0].astype(BF16), xt, modr, 0, 5, seq,
                      next_norm=(norm_mix[1], row, 1, 0))

    w_qkv = qkv_w[0].astype(BF16)
    cos_q, sin_q = _rope_tables(seq, head_dim, head_dim ** -0.5 * math.log2(math.e))
    cos_k, sin_k = _rope_tables(seq, head_dim)
    q = _qkv_call(h, w_qkv, 0, seq, head_dim, q_norm[0], cos_q, sin_q)
    k = _qkv_call(h, w_qkv, 1, seq, head_dim, k_norm[0], cos_k, sin_k)
    v = _qkv_call(h, w_qkv, 2, seq, head_dim)
    o = _attn_call(q, k, v, seq, head_dim)
    xt = _resmm_call(o, attn_out[0].astype(BF16), xt, modr, row, 2, seq)

    hf, top_idx, top_w = _router_call(xt, norm_ffn[1], modr, row, 4, 3,
                                      router_w[0], router_b[0], seq)
    n_blocks = (t * EXPERT_TOPK) // EXPERT_ROWS + n_experts
    dest, tables = _routing_tables(top_idx[:, :EXPERT_TOPK], n_experts, EXPERT_ROWS, n_blocks)
    xs = _dispatch_call(hf, dest, tables, n_blocks * EXPERT_ROWS, EXPERT_ROWS)
    y_tail, wg, wu, wd = _expert_tail_call(xs, tables, exp_gate[0], exp_up[0], exp_down[0],
                                           EXPERT_ROWS)
    ys = _expert_call(xs, tables, y_tail, wg, wu, wd, EXPERT_ROWS)
    xt = _combine_call(ys, dest, xt, top_w, modr, row, 5, seq)
    return xt.reshape(bsz, seq, d)
```

```python
import functools
import math

import jax
import jax.numpy as jnp
from jax import lax
from jax.experimental import pallas as pl
from jax.experimental.pallas import tpu as pltpu

N_HEADS = 16
CONV_WIDTH = 3
MOBA_BLOCK = 256
MOBA_TOPK = 3
ROPE_THETA = 10000.0
EXPERT_TOPK = 2
EXPERT_ROWS = 512
ROW_DMA_UNROLL = 8
NORM_EPS = 1e-6
NEG_INF = -1e30

LANES = 128
SUBLANES = 8
BF16_SUBLANES = 16
MIB = 1024 * 1024

F32 = jnp.float32
BF16 = jnp.bfloat16

_NT_DIMS = (((1,), (1,)), ((), ()))


def _params(semantics, vmem_mib):
    return pltpu.CompilerParams(dimension_semantics=semantics,
                                vmem_limit_bytes=vmem_mib * MIB)


def _dot(a, b):
    return jnp.dot(a, b, preferred_element_type=F32)


def _silu(a):
    return a * (1.0 / (1.0 + jnp.exp(-a)))


def _mod_kernel(c_ref, w_ref, b_ref, o_ref):
    ca = _silu(c_ref[...])
    o_ref[0] = _dot(ca.astype(BF16), w_ref[0].astype(BF16)) + b_ref[0]


def _mod_call(c, mod_w, mod_b, tn=1024):
    depth, d, n = mod_w.shape
    bsz = c.shape[0]
    return pl.pallas_call(
        _mod_kernel,
        grid=(depth, n // tn),
        in_specs=[
            pl.BlockSpec((bsz, d), lambda l, j: (0, 0)),
            pl.BlockSpec((1, d, tn), lambda l, j: (l, 0, j)),
            pl.BlockSpec((1, 1, tn), lambda l, j: (l, 0, j)),
        ],
        out_specs=pl.BlockSpec((1, bsz, tn), lambda l, j: (l, 0, j)),
        out_shape=jax.ShapeDtypeStruct((depth, bsz, n), F32),
        compiler_params=_params(("arbitrary", "arbitrary"), 40),
        name="adaln_mod",
    )(c, mod_w, mod_b.reshape(depth, 1, n))


def _normmod(x, gain, scale, shift):
    ms = jnp.mean(x * x, axis=-1, keepdims=True)
    y = x * lax.rsqrt(ms + NORM_EPS) * gain
    return y * (1.0 + scale) + shift


def _normmod_kernel(x_ref, gain_ref, sc_ref, sh_ref, o_ref):
    o_ref[...] = _normmod(x_ref[...], gain_ref[...], sc_ref[0], sh_ref[0]).astype(o_ref.dtype)


def _mod_spec(d, rows_per_batch, row, chunk):
    return pl.BlockSpec((1, 1, d), lambda i, *_: (row + i // rows_per_batch, 0, chunk))


def _normmod_call(x, gain, modr, row, sc_chunk, sh_chunk, seq, tm=512):
    t, d = x.shape
    rpb = seq // tm
    return pl.pallas_call(
        _normmod_kernel,
        grid=(t // tm,),
        in_specs=[
            pl.BlockSpec((tm, d), lambda i: (i, 0)),
            pl.BlockSpec((1, d), lambda i: (0, 0)),
            _mod_spec(d, rpb, row, sc_chunk),
            _mod_spec(d, rpb, row, sh_chunk),
        ],
        out_specs=pl.BlockSpec((tm, d), lambda i: (i, 0)),
        out_shape=jax.ShapeDtypeStruct((t, d), BF16),
        compiler_params=_params(("arbitrary",), 32),
        name="normmod",
    )(x, gain.reshape(1, d), modr, modr)


def _convin_kernel(h_ref, wb_ref, wc_ref, wv_ref, cw_ref, o_ref, u_scr):
    seq = h_ref.shape[0]
    h = h_ref[...]
    u = _dot(h, wc_ref[...]) * _dot(h, wv_ref[...])
    u_scr[0:SUBLANES, :] = jnp.zeros((SUBLANES, u.shape[1]), F32)
    u_scr[SUBLANES:, :] = u
    cw = cw_ref[...]
    conv = cw[CONV_WIDTH - 1:CONV_WIDTH, :] * u
    for tap in range(1, CONV_WIDTH):
        conv = conv + (cw[CONV_WIDTH - 1 - tap:CONV_WIDTH - tap, :]
                       * u_scr[SUBLANES - tap:SUBLANES - tap + seq, :])
    o_ref[...] = (_dot(h, wb_ref[...]) * conv).astype(o_ref.dtype)


def _convin_call(h, w_in, conv_w, seq, tn=256):
    t, d = h.shape
    nj = d // tn
    return pl.pallas_call(
        _convin_kernel,
        grid=(nj, t // seq),
        in_specs=[
            pl.BlockSpec((seq, d), lambda j, b: (b, 0)),
            pl.BlockSpec((d, tn), lambda j, b: (0, j)),
            pl.BlockSpec((d, tn), lambda j, b: (0, j + nj)),
            pl.BlockSpec((d, tn), lambda j, b: (0, j + 2 * nj)),
            pl.BlockSpec((CONV_WIDTH, tn), lambda j, b: (0, j)),
        ],
        out_specs=pl.BlockSpec((seq, tn), lambda j, b: (b, j)),
        out_shape=jax.ShapeDtypeStruct((t, d), BF16),
        scratch_shapes=[pltpu.VMEM((seq + SUBLANES, tn), F32)],
        compiler_params=_params(("arbitrary", "arbitrary"), 48),
        name="conv_in",
    )(h, w_in, w_in, w_in, conv_w)


def _resmm_kernel(a_ref, w_ref, x_ref, g_ref, o_ref):
    o_ref[...] = x_ref[...] + g_ref[0] * _dot(a_ref[...], w_ref[...])


def _resmm_norm_kernel(a_ref, w_ref, x_ref, g_ref, gain_ref, sc_ref, sh_ref, o_ref, h_ref):
    xn = x_ref[...] + g_ref[0] * _dot(a_ref[...], w_ref[...])
    o_ref[...] = xn
    h_ref[...] = _normmod(xn, gain_ref[...], sc_ref[0], sh_ref[0]).astype(h_ref.dtype)


def _resmm_call(a, w, x, modr, row, g_chunk, seq, next_norm=None, tm=512):
    t, k = a.shape
    d = w.shape[1]
    rpb = seq // tm
    in_specs = [
        pl.BlockSpec((tm, k), lambda i: (i, 0)),
        pl.BlockSpec((k, d), lambda i: (0, 0)),
        pl.BlockSpec((tm, d), lambda i: (i, 0)),
        _mod_spec(d, rpb, row, g_chunk),
    ]
    row_spec = pl.BlockSpec((tm, d), lambda i: (i, 0))
    cp = _params(("arbitrary",), 52)
    if next_norm is None:
        return pl.pallas_call(
            _resmm_kernel, grid=(t // tm,), in_specs=in_specs, out_specs=row_spec,
            out_shape=jax.ShapeDtypeStruct((t, d), F32), compiler_params=cp,
            name="res_matmul")(a, w, x, modr)
    gain, nrow, sc_chunk, sh_chunk = next_norm
    in_specs += [pl.BlockSpec((1, d), lambda i: (0, 0)),
                 _mod_spec(d, rpb, nrow, sc_chunk), _mod_spec(d, rpb, nrow, sh_chunk)]
    return pl.pallas_call(
        _resmm_norm_kernel, grid=(t // tm,), in_specs=in_specs, out_specs=[row_spec, row_spec],
        out_shape=[jax.ShapeDtypeStruct((t, d), F32), jax.ShapeDtypeStruct((t, d), BF16)],
        compiler_params=cp, name="res_matmul_norm")(a, w, x, modr, gain.reshape(1, d), modr, modr)


def _ffn_kernel(h_ref, wg_ref, wu_ref, wd_ref, x_ref, g_ref, gain_ref, sc_ref, sh_ref,
                o_ref, hn_ref):
    j = pl.program_id(1)
    h = h_ref[...]
    act = _silu(_dot(h, wg_ref[...])) * _dot(h, wu_ref[...])
    part = _dot(act.astype(BF16), wd_ref[...])

    @pl.when(j == 0)
    def _():
        o_ref[...] = part

    @pl.when(j > 0)
    def _():
        o_ref[...] += part

    @pl.when(j == pl.num_programs(1) - 1)
    def _():
        xn = x_ref[...] + g_ref[0] * o_ref[...]
        o_ref[...] = xn
        hn_ref[...] = _normmod(xn, gain_ref[...], sc_ref[0], sh_ref[0]).astype(hn_ref.dtype)


def _ffn_call(h, wg, wu, wd, x, modr, row, g_chunk, seq, next_norm, tm=512, tf=512):
    t, d = h.shape
    f = wg.shape[1]
    rpb = seq // tm
    gain, nrow, sc_chunk, sh_chunk = next_norm
    row_spec = pl.BlockSpec((tm, d), lambda i, j: (i, 0))
    return pl.pallas_call(
        _ffn_kernel,
        grid=(t // tm, f // tf),
        in_specs=[
            row_spec,
            pl.BlockSpec((d, tf), lambda i, j: (0, j)),
            pl.BlockSpec((d, tf), lambda i, j: (0, j)),
            pl.BlockSpec((tf, d), lambda i, j: (j, 0)),
            row_spec,
            _mod_spec(d, rpb, row, g_chunk),
            pl.BlockSpec((1, d), lambda i, j: (0, 0)),
            _mod_spec(d, rpb, nrow, sc_chunk),
            _mod_spec(d, rpb, nrow, sh_chunk),
        ],
        out_specs=[row_spec, row_spec],
        out_shape=[jax.ShapeDtypeStruct((t, d), F32), jax.ShapeDtypeStruct((t, d), BF16)],
        compiler_params=_params(("arbitrary", "arbitrary"), 52),
        name="dense_ffn",
    )(h, wg, wu, wd, x, modr, gain.reshape(1, d), modr, modr)


def _qk_kernel(h_ref, w_ref, gain_ref, cos_ref, sin_ref, ones_ref, perm_ref, o_ref, *, head_dim):
    acc = _dot(h_ref[...], w_ref[...])
    pair = ones_ref.shape[0]
    gain = gain_ref[...]
    cos = cos_ref[...]
    sin = sin_ref[...]
    ones = ones_ref[...]
    perm = perm_ref[...]
    for p0 in range(0, acc.shape[1], pair):
        xp = acc[:, p0:p0 + pair]
        sq = xp * xp
        sq_a = sq.astype(BF16)
        sq_b = (sq - sq_a.astype(F32)).astype(BF16)
        ss = _dot(sq_a, ones) + _dot(sq_b, ones)
        inv = lax.rsqrt(ss * (1.0 / head_dim) + NORM_EPS)
        z = xp * gain
        z_a = z.astype(BF16)
        z_b = (z - z_a.astype(F32)).astype(BF16)
        rot = _dot(z_a, perm) + _dot(z_b, perm)
        out = (z * cos + rot * sin) * inv
        o_ref[:, p0:p0 + pair] = out.astype(o_ref.dtype)


def _v_kernel(h_ref, w_ref, o_ref):
    o_ref[...] = _dot(h_ref[...], w_ref[...]).astype(o_ref.dtype)


def _head_pair_matrices(head_dim):
    pair = 2 * head_dim
    src = jnp.arange(pair)[:, None]
    dst = jnp.arange(pair)[None, :]
    same_head = (src // head_dim) == (dst // head_dim)
    ones = same_head.astype(BF16)
    perm = (same_head & ((src % head_dim) == ((dst + head_dim // 2) % head_dim))).astype(BF16)
    return ones, perm


def _qkv_call(h, w_qkv, which, seq, head_dim, gain=None, cos=None, sin=None, tm=1024, tn=512):
    t, d = h.shape
    nj = d // tn
    off = which * nj
    rpb = seq // tm
    h_spec = pl.BlockSpec((tm, d), lambda j, i: (i, 0))
    w_spec = pl.BlockSpec((d, tn), lambda j, i: (0, j + off))
    out_spec = pl.BlockSpec((tm, tn), lambda j, i: (i, j))
    out_shape = jax.ShapeDtypeStruct((t, d), BF16)
    cp = _params(("arbitrary", "arbitrary"), 40)
    if gain is None:
        return pl.pallas_call(_v_kernel, grid=(nj, t // tm), in_specs=[h_spec, w_spec],
                              out_specs=out_spec, out_shape=out_shape, compiler_params=cp,
                              name="v_proj")(h, w_qkv)
    pair = 2 * head_dim
    ones, perm = _head_pair_matrices(head_dim)
    tab_spec = pl.BlockSpec((tm, pair), lambda j, i: (i % rpb, 0))
    const_spec = pl.BlockSpec((pair, pair), lambda j, i: (0, 0))
    return pl.pallas_call(
        functools.partial(_qk_kernel, head_dim=head_dim),
        grid=(nj, t // tm),
        in_specs=[h_spec, w_spec, pl.BlockSpec((1, pair), lambda j, i: (0, 0)),
                  tab_spec, tab_spec, const_spec, const_spec],
        out_specs=out_spec, out_shape=out_shape, compiler_params=cp,
        name="qk_proj",
    )(h, w_qkv, jnp.tile(gain.reshape(1, head_dim), (1, 2)), cos, sin, ones, perm)


def _attn_kernel(q_ref, k_ref, v_ref, o_ref, vt_scr, s_scr):
    seq, hd = q_ref.shape
    blk = MOBA_BLOCK
    nb = seq // blk

    rows = []
    for n in range(nb):
        vt_scr[n, 0:hd, :] = v_ref[n * blk:(n + 1) * blk, :].astype(F32).T.astype(BF16)
        vt_scr[n, hd:, :] = jnp.ones((vt_scr.shape[1] - hd, blk), BF16)
        kb = k_ref[n * blk:(n + 1) * blk, :].astype(F32)
        rows.append(jnp.sum(kb, axis=0, keepdims=True) * (1.0 / blk))
    km = jnp.concatenate(rows, axis=0)
    km_a = km.astype(BF16)
    rem = km - km_a.astype(F32)
    km_b = rem.astype(BF16)
    km_c = (rem - km_b.astype(F32)).astype(BF16)
    km_parts = jnp.concatenate([km_a, km_b, km_c, jnp.zeros_like(km_a)], axis=0)

    causal = (lax.broadcasted_iota(jnp.int32, (blk, blk), 0)
              <= lax.broadcasted_iota(jnp.int32, (blk, blk), 1))

    def masked_scores(qi):
        qb = q_ref[qi * blk:(qi + 1) * blk, :]
        sel = None
        if qi > MOBA_TOPK:
            g3 = lax.dot_general(km_parts, qb, _NT_DIMS, preferred_element_type=F32)
            gate = g3[0:nb] + g3[nb:2 * nb] + g3[2 * nb:3 * nb]
            g_row = [gate[n:n + 1, :] for n in range(qi)]
            rank = [jnp.zeros((1, blk), F32) for _ in range(qi)]
            for a in range(qi):
                for b in range(a + 1, qi):
                    a_first = (g_row[a] >= g_row[b]).astype(F32)
                    rank[b] = rank[b] + a_first
                    rank[a] = rank[a] + (1.0 - a_first)
            sel = [rank[n] < MOBA_TOPK for n in range(qi)]

        m_run = None
        for n in range(qi + 1):
            s = lax.dot_general(k_ref[n * blk:(n + 1) * blk, :], qb, _NT_DIMS,
                                preferred_element_type=F32)
            if n == qi:
                s = jnp.where(causal, s, NEG_INF)
            elif sel is not None:
                s = jnp.where(sel[n], s, NEG_INF)
            s_scr[qi % 2, n] = s
            cm = jnp.max(s, axis=0, keepdims=True)
            m_run = cm if m_run is None else jnp.maximum(m_run, cm)
        return m_run

    def weighted_values(qi, m_run):
        acc = None
        for n in range(qi + 1):
            p = jnp.exp2(s_scr[qi % 2, n] - m_run)
            pv = _dot(vt_scr[n], p.astype(BF16))
            acc = pv if acc is None else acc + pv
        out_t = acc[0:hd] * (1.0 / acc[hd:hd + 1])
        o_ref[qi * blk:(qi + 1) * blk, :] = out_t.T.astype(o_ref.dtype)

    m_next = masked_scores(0)
    for qi in range(nb):
        m_cur = m_next
        if qi + 1 < nb:
            m_next = masked_scores(qi + 1)
        weighted_values(qi, m_cur)


def _attn_call(q, k, v, seq, head_dim):
    t, d = q.shape
    nh = d // head_dim
    nb = seq // MOBA_BLOCK
    spec = pl.BlockSpec((seq, head_dim), lambda b, h: (b, h))
    return pl.pallas_call(
        _attn_kernel,
        grid=(t // seq, nh),
        in_specs=[spec, spec, spec],
        out_specs=spec,
        out_shape=jax.ShapeDtypeStruct((t, d), BF16),
        scratch_shapes=[pltpu.VMEM((nb, head_dim + BF16_SUBLANES, MOBA_BLOCK), BF16),
                        pltpu.VMEM((2, nb, MOBA_BLOCK, MOBA_BLOCK), F32)],
        compiler_params=_params(("arbitrary", "arbitrary"), 32),
        name="moba_attention",
    )(q, k, v)


def _router_kernel(x_ref, gain_ref, sc_ref, sh_ref, wr_ref, br_ref, h_ref, idx_ref, wt_ref,
                   *, n_experts):
    h = _normmod(x_ref[...], gain_ref[...], sc_ref[0], sh_ref[0])
    h_ref[...] = h
    h_a = h.astype(BF16)
    h_b = (h - h_a.astype(F32)).astype(BF16)
    w = wr_ref[...]
    w_a = w.astype(BF16)
    w_b = (w - w_a.astype(F32)).astype(BF16)
    logits = _dot(h_a, w_a) + _dot(h_b, w_a) + _dot(h_a, w_b) + br_ref[...]
    lane = lax.broadcasted_iota(jnp.int32, logits.shape, 1)
    logits = jnp.where(lane < n_experts, logits, -jnp.inf)
    m1 = jnp.max(logits, axis=-1, keepdims=True)
    i1 = jnp.min(jnp.where(logits == m1, lane, LANES), axis=-1, keepdims=True)
    rest = jnp.where(lane == i1, -jnp.inf, logits)
    m2 = jnp.max(rest, axis=-1, keepdims=True)
    i2 = jnp.min(jnp.where(rest == m2, lane, LANES), axis=-1, keepdims=True)
    e2 = jnp.exp(m2 - m1)
    w1 = 1.0 / (1.0 + e2)
    w2 = e2 * w1
    idx_ref[...] = jnp.where(lane == 0, i1, jnp.where(lane == 1, i2, 0))
    wt_ref[...] = jnp.where(lane == 0, w1, jnp.where(lane == 1, w2, 0.0))


def _router_call(x, gain, modr, row, sc_chunk, sh_chunk, w_router, b_router, seq, tm=512):
    t, d = x.shape
    n_experts = w_router.shape[1]
    rpb = seq // tm
    wr = jnp.zeros((d, LANES), F32).at[:, :n_experts].set(w_router)
    br = jnp.zeros((1, LANES), F32).at[0, :n_experts].set(b_router)
    return pl.pallas_call(
        functools.partial(_router_kernel, n_experts=n_experts),
        grid=(t // tm,),
        in_specs=[
            pl.BlockSpec((tm, d), lambda i: (i, 0)),
            pl.BlockSpec((1, d), lambda i: (0, 0)),
            _mod_spec(d, rpb, row, sc_chunk),
            _mod_spec(d, rpb, row, sh_chunk),
            pl.BlockSpec((d, LANES), lambda i: (0, 0)),
            pl.BlockSpec((1, LANES), lambda i: (0, 0)),
        ],
        out_specs=[
            pl.BlockSpec((tm, d), lambda i: (i, 0)),
            pl.BlockSpec((tm, LANES), lambda i: (i, 0)),
            pl.BlockSpec((tm, LANES), lambda i: (i, 0)),
        ],
        out_shape=[
            jax.ShapeDtypeStruct((t, d), F32),
            jax.ShapeDtypeStruct((t, LANES), jnp.int32),
            jax.ShapeDtypeStruct((t, LANES), F32),
        ],
        compiler_params=_params(("arbitrary",), 40),
        name="moe_router",
    )(x, gain.reshape(1, d), modr, modr, wr, br)


def _dispatch_kernel(zb_ref, nz_ref, dest_ref, h_ref, xs_ref, zbuf, sem, zsem):
    tm = h_ref.shape[0]
    zrows = zbuf.shape[0]

    @pl.when(pl.program_id(0) == 0)
    def _():
        zbuf[...] = jnp.zeros(zbuf.shape, zbuf.dtype)

        def zero_copy(k):
            row0 = pl.multiple_of(zb_ref[k] * zrows, zrows)
            return pltpu.make_async_copy(zbuf, xs_ref.at[pl.ds(row0, zrows), :], zsem)

        def start(k, c):
            zero_copy(k).start()
            return c

        def wait(k, c):
            zero_copy(k).wait()
            return c

        lax.fori_loop(0, nz_ref[0], start, 0)
        lax.fori_loop(0, nz_ref[0], wait, 0)

    def issue(r, c):
        for k in range(EXPERT_TOPK):
            d = dest_ref[0, 0, EXPERT_TOPK * r + k]
            pltpu.make_async_copy(h_ref.at[pl.ds(r, 1), :], xs_ref.at[pl.ds(d, 1), :],
                                  sem).start(priority=k % 2)
        return c

    lax.fori_loop(0, tm, issue, 0, unroll=ROW_DMA_UNROLL)
    for k in range(EXPERT_TOPK):
        pltpu.make_async_copy(h_ref, xs_ref.at[pl.ds(0, tm), :], sem).wait()


def _dispatch_call(h, dest, tables, n_rows, block_rows, tm=256):
    t, d = h.shape
    nt = t // tm
    grid_spec = pltpu.PrefetchScalarGridSpec(
        num_scalar_prefetch=2,
        grid=(nt,),
        in_specs=[
            pl.BlockSpec((1, 1, EXPERT_TOPK * tm), lambda i, zb, nz: (i, 0, 0),
                         memory_space=pltpu.SMEM),
            pl.BlockSpec((tm, d), lambda i, zb, nz: (i, 0)),
        ],
        out_specs=pl.BlockSpec(memory_space=pl.ANY),
        scratch_shapes=[pltpu.VMEM((block_rows, d), h.dtype), pltpu.SemaphoreType.DMA,
                        pltpu.SemaphoreType.DMA],
    )
    return pl.pallas_call(
        _dispatch_kernel,
        grid_spec=grid_spec,
        out_shape=jax.ShapeDtypeStruct((n_rows, d), h.dtype),
        compiler_params=_params(("arbitrary",), 32),
        name="moe_dispatch",
    )(tables["zero_blocks"], tables["n_zero"], dest.reshape(nt, 1, EXPERT_TOPK * tm), h)


BLOCK_COMPUTE, BLOCK_TAIL, BLOCK_UNUSED = 0, 1, 2


def _expert_kernel(kind_ref, xmap_ref, we_ref, tmap_ref, xs_ref, wg_ref, wu_ref, wd_ref,
                   ytail_ref, ys_ref, xb_scr):
    del xmap_ref, we_ref, tmap_ref
    i = pl.program_id(0)
    j = pl.program_id(1)
    kind = kind_ref[i]

    @pl.when((kind == BLOCK_UNUSED) & (j == 0))
    def _():
        ys_ref[...] = jnp.zeros(ys_ref.shape, F32)

    @pl.when((kind == BLOCK_TAIL) & (j == 0))
    def _():
        ys_ref[...] = ytail_ref[...]

    @pl.when(kind == BLOCK_COMPUTE)
    def _():
        @pl.when(j == 0)
        def _():
            xb_scr[...] = xs_ref[...].astype(BF16)
            ys_ref[...] = jnp.zeros(ys_ref.shape, F32)

        x = xb_scr[...]
        act = _silu(_dot(x, wg_ref[0])) * _dot(x, wu_ref[0])
        ys_ref[...] += _dot(act.astype(BF16), wd_ref[0])


def _expert_call(xs, tables, y_tail, wg, wu, wd, tm, tf=1024):
    n_rows, d = xs.shape
    f = wg.shape[2]
    nf = f // tf

    def wj(i, j, kind):
        return jnp.where(kind[i] == BLOCK_COMPUTE, j, nf - 1)

    once = pl.Buffered(1)
    grid_spec = pltpu.PrefetchScalarGridSpec(
        num_scalar_prefetch=4,
        grid=(n_rows // tm, nf),
        in_specs=[
            pl.BlockSpec((tm, d), lambda i, j, kind, xm, we, fm: (xm[i], 0), pipeline_mode=once),
            pl.BlockSpec((1, d, tf), lambda i, j, kind, xm, we, fm: (we[i], 0, wj(i, j, kind))),
            pl.BlockSpec((1, d, tf), lambda i, j, kind, xm, we, fm: (we[i], 0, wj(i, j, kind))),
            pl.BlockSpec((1, tf, d), lambda i, j, kind, xm, we, fm: (we[i], wj(i, j, kind), 0)),
            pl.BlockSpec((tm, d), lambda i, j, kind, xm, we, fm: (fm[i], 0), pipeline_mode=once),
        ],
        out_specs=pl.BlockSpec((tm, d), lambda i, j, kind, xm, we, fm: (i, 0)),
        scratch_shapes=[pltpu.VMEM((tm, d), BF16)],
    )
    return pl.pallas_call(
        _expert_kernel,
        grid_spec=grid_spec,
        out_shape=jax.ShapeDtypeStruct((n_rows, d), F32),
        compiler_params=_params(("arbitrary", "arbitrary"), 52),
        name="moe_experts",
    )(tables["kind"], tables["x_map"], tables["w_expert"], tables["tail_map"],
      xs, wg, wu, wd, y_tail)


TAIL_BLOCKS = 2


def _expert_tail_kernel(tb_ref, nt_ref, short_ref, *refs):
    del tb_ref
    xs_refs = refs[:TAIL_BLOCKS]
    wg_ref, wu_ref, wd_ref, ys_ref, wgb_ref, wub_ref, wdb_ref, xb_scr = refs[TAIL_BLOCKS:]
    tm = xs_refs[0].shape[0]
    e = pl.program_id(0)
    j = pl.program_id(1)

    @pl.when(j == 0)
    def _():
        for k, xs_ref in enumerate(xs_refs):
            xb_scr[k * tm:(k + 1) * tm, :] = xs_ref[...].astype(BF16)
        ys_ref[...] = jnp.zeros(ys_ref.shape, F32)

    def run(n_rows):
        x = xb_scr[0:n_rows, :]
        wg = wg_ref[0].astype(BF16)
        wu = wu_ref[0].astype(BF16)
        act = _silu(_dot(x, wg)) * _dot(x, wu)
        wd = wd_ref[0].astype(BF16)
        part = _dot(act.astype(BF16), wd)
        for k in range(TAIL_BLOCKS):
            rows = slice(k * tm, min((k + 1) * tm, n_rows))
            ys_ref[rows, :] += jnp.where(nt_ref[e] >= TAIL_BLOCKS - k, part[rows, :], 0.0)
        wgb_ref[0] = wg
        wub_ref[0] = wu
        wdb_ref[0] = wd

    @pl.when(short_ref[e] > 0)
    def _():
        run(TAIL_BLOCKS * tm - tm // 2)

    @pl.when(short_ref[e] == 0)
    def _():
        run(TAIL_BLOCKS * tm)


def _expert_tail_call(xs, tables, wg, wu, wd, tm, tf=256):
    d = xs.shape[1]
    n_experts, _, f = wg.shape

    def xs_spec(k):
        return pl.BlockSpec((tm, d), lambda e, j, tb, *_: (tb[TAIL_BLOCKS * e + k], 0),
                            pipeline_mode=pl.Buffered(1))

    w_in = lambda e, j, *_: (e, 0, j)
    w_out = lambda e, j, *_: (e, j, 0)
    grid_spec = pltpu.PrefetchScalarGridSpec(
        num_scalar_prefetch=3,
        grid=(n_experts, f // tf),
        in_specs=[xs_spec(k) for k in range(TAIL_BLOCKS)] + [
            pl.BlockSpec((1, d, tf), w_in),
            pl.BlockSpec((1, d, tf), w_in),
            pl.BlockSpec((1, tf, d), w_out),
        ],
        out_specs=[
            pl.BlockSpec((TAIL_BLOCKS * tm, d), lambda e, j, *_: (e, 0)),
            pl.BlockSpec((1, d, tf), w_in),
            pl.BlockSpec((1, d, tf), w_in),
            pl.BlockSpec((1, tf, d), w_out),
        ],
        scratch_shapes=[pltpu.VMEM((TAIL_BLOCKS * tm, d), BF16)],
    )
    return pl.pallas_call(
        _expert_tail_kernel,
        grid_spec=grid_spec,
        out_shape=[
            jax.ShapeDtypeStruct((n_experts * TAIL_BLOCKS * tm, d), F32),
            jax.ShapeDtypeStruct(wg.shape, BF16),
            jax.ShapeDtypeStruct(wu.shape, BF16),
            jax.ShapeDtypeStruct(wd.shape, BF16),
        ],
        compiler_params=_params(("arbitrary", "arbitrary"), 58),
        name="moe_experts_tail",
    )(tables["tail_block"], tables["n_tail"], tables["tail_short"], *([xs] * TAIL_BLOCKS),
      wg, wu, wd)


def _combine_kernel(dest_ref, dest_next_ref, ys_ref, x_ref, wt_ref, g_ref, o_ref, ybuf, sems):
    tm = x_ref.shape[0]
    i = pl.program_id(0)
    slot = i % 2

    def start_gathers(d_ref, dst_slot):
        def issue(r, c):
            for k in range(EXPERT_TOPK):
                d = d_ref[0, 0, EXPERT_TOPK * r + k]
                pltpu.make_async_copy(ys_ref.at[pl.ds(d, 1), :],
                                      ybuf.at[dst_slot, k, pl.ds(r, 1), :],
                                      sems.at[dst_slot]).start(priority=k % 2)
            return c

        lax.fori_loop(0, tm, issue, 0, unroll=ROW_DMA_UNROLL)

    @pl.when(i == 0)
    def _():
        start_gathers(dest_ref, 0)

    @pl.when(i + 1 < pl.num_programs(0))
    def _():
        start_gathers(dest_next_ref, 1 - slot)

    for k in range(EXPERT_TOPK):
        pltpu.make_async_copy(ys_ref.at[pl.ds(0, tm), :], ybuf.at[slot, k], sems.at[slot]).wait()
    wt = wt_ref[...]
    y = wt[:, 0:1] * ybuf[slot, 0]
    for k in range(1, EXPERT_TOPK):
        y = y + wt[:, k:k + 1] * ybuf[slot, k]
    o_ref[...] = x_ref[...] + g_ref[0] * y


def _combine_call(ys, dest, x, wt, modr, row, g_chunk, seq, tm=256):
    t, d = x.shape
    nt = t // tm
    rpb = seq // tm
    dest3 = dest.reshape(nt, 1, EXPERT_TOPK * tm)
    return pl.pallas_call(
        _combine_kernel,
        grid=(nt,),
        in_specs=[
            pl.BlockSpec((1, 1, EXPERT_TOPK * tm), lambda i: (i, 0, 0), memory_space=pltpu.SMEM),
            pl.BlockSpec((1, 1, EXPERT_TOPK * tm), lambda i: (jnp.minimum(i + 1, nt - 1), 0, 0),
                         memory_space=pltpu.SMEM),
            pl.BlockSpec(memory_space=pl.ANY),
            pl.BlockSpec((tm, d), lambda i: (i, 0)),
            pl.BlockSpec((tm, LANES), lambda i: (i, 0)),
            _mod_spec(d, rpb, row, g_chunk),
        ],
        out_specs=pl.BlockSpec((tm, d), lambda i: (i, 0)),
        out_shape=jax.ShapeDtypeStruct((t, d), F32),
        scratch_shapes=[pltpu.VMEM((2, EXPERT_TOPK, tm, d), F32), pltpu.SemaphoreType.DMA((2,))],
        compiler_params=_params(("arbitrary",), 40),
        name="moe_combine",
    )(dest3, dest3, ys, x, wt, modr)


def _routing_tables(top_idx, n_experts, tm, n_blocks):
    flat_e = top_idx.reshape(-1)
    onehot = (flat_e[:, None] == jnp.arange(n_experts, dtype=jnp.int32)[None, :]).astype(jnp.int32)
    csum = jnp.cumsum(onehot, axis=0)
    counts = csum[-1]
    pos = jnp.sum((csum - 1) * onehot, axis=1)
    nblk = (counts + tm - 1) // tm
    blk_end = jnp.cumsum(nblk)
    blk_start = blk_end - nblk
    dest = (blk_start[flat_e] * tm + pos).astype(jnp.int32)
    n_used = blk_end[-1]

    blocks = jnp.arange(n_blocks, dtype=jnp.int32)
    x_map = jnp.minimum(blocks, n_used - 1)
    blk_e = jnp.minimum(jnp.sum(blk_end[None, :] <= x_map[:, None], axis=1), n_experts - 1)
    from_end = blk_end[blk_e] - blocks
    is_tail = (from_end <= TAIL_BLOCKS) & (blocks < n_used)
    kind = jnp.where(blocks >= n_used, BLOCK_UNUSED,
                     jnp.where(is_tail, BLOCK_TAIL, BLOCK_COMPUTE))
    last_compute = lax.cummax(jnp.where(kind == BLOCK_COMPUTE, blocks, -1))
    w_expert = blk_e[jnp.maximum(last_compute, 0)]
    tail_map = lax.cummax(jnp.where(is_tail, TAIL_BLOCKS * (blk_e + 1) - from_end, 0))
    slot_from_end = TAIL_BLOCKS - jnp.arange(TAIL_BLOCKS, dtype=jnp.int32)
    tail_block = jnp.clip(blk_end[:, None] - slot_from_end[None, :], 0, n_blocks - 1)
    cand = jnp.concatenate([blk_end - 1, blocks])
    cand_ok = jnp.concatenate([nblk > 0, blocks >= n_used])
    zero_blocks = jnp.maximum(cand[jnp.argsort(~cand_ok, stable=True)], 0)
    i32 = lambda a: a.astype(jnp.int32)
    tables = dict(kind=i32(kind), x_map=i32(x_map), w_expert=i32(w_expert),
                  tail_map=i32(tail_map), tail_block=i32(tail_block.reshape(-1)),
                  n_tail=i32(jnp.minimum(nblk, TAIL_BLOCKS)),
                  tail_short=i32((nblk > 0) & (counts - tm * (nblk - 1) <= tm // 2)),
                  zero_blocks=i32(zero_blocks), n_zero=i32(jnp.sum(cand_ok)).reshape(1))
    return dest, tables


def _rope_tables(seq, head_dim, scale=1.0):
    half = head_dim // 2
    inv_freq = jnp.exp(-math.log(ROPE_THETA) * jnp.arange(half, dtype=F32) / half)
    ang = jnp.arange(seq).astype(F32)[:, None] * inv_freq[None, :]
    cos, sin = jnp.cos(ang) * scale, jnp.sin(ang) * scale
    return (jnp.concatenate([cos, cos, cos, cos], axis=-1),
            jnp.concatenate([-sin, sin, -sin, sin], axis=-1))


def kernel(x, c, mod_w, mod_b, norm_mix, norm_ffn, conv_in, conv_w, conv_out, ffn_gate, ffn_up,
           ffn_down, qkv_w, q_norm, k_norm, attn_out, router_w, router_b, exp_gate, exp_up,
           exp_down):
    bsz, seq, d = x.shape
    depth = mod_w.shape[0]
    assert depth == 2, "layer 0 = short-conv + dense FFN, layer 1 = MoBA + MoE"
    t = bsz * seq
    head_dim = d // N_HEADS
    n_experts = router_w.shape[-1]

    mod = _mod_call(c, mod_w, mod_b)
    modr = mod.reshape(depth * bsz, 1, 6 * d)
    xt = x.reshape(t, d)

    row = bsz
    h = _normmod_call(xt, norm_mix[0], modr, 0, 1, 0, seq)
    bu = _convin_call(h, conv_in[0].astype(BF16), conv_w[0], seq)
    xt, h = _resmm_call(bu, conv_out[0].astype(BF16), xt, modr, 0, 2, seq,
                        next_norm=(norm_ffn[0], 0, 4, 3))
    xt, h = _ffn_call(h, ffn_gate[0].astype(BF16), ffn_up[0].astype(BF16),
                      ffn_down[0].astype(BF16), xt, modr, 0, 5, seq,
                      next_norm=(norm_mix[1], row, 1, 0))

    w_qkv = qkv_w[0].astype(BF16)
    cos_q, sin_q = _rope_tables(seq, head_dim, head_dim ** -0.5 * math.log2(math.e))
    cos_k, sin_k = _rope_tables(seq, head_dim)
    q = _qkv_call(h, w_qkv, 0, seq, head_dim, q_norm[0], cos_q, sin_q)
    k = _qkv_call(h, w_qkv, 1, seq, head_dim, k_norm[0], cos_k, sin_k)
    v = _qkv_call(h, w_qkv, 2, seq, head_dim)
    o = _attn_call(q, k, v, seq, head_dim)
    xt = _resmm_call(o, attn_out[0].astype(BF16), xt, modr, row, 2, seq)

    hf, top_idx, top_w = _router_call(xt, norm_ffn[1], modr, row, 4, 3,
                                      router_w[0], router_b[0], seq)
    n_blocks = (t * EXPERT_TOPK) // EXPERT_ROWS + n_experts
    dest, tables = _routing_tables(top_idx[:, :EXPERT_TOPK], n_experts, EXPERT_ROWS, n_blocks)
    xs = _dispatch_call(hf, dest, tables, n_blocks * EXPERT_ROWS, EXPERT_ROWS)
    y_tail, wg, wu, wd = _expert_tail_call(xs, tables, exp_gate[0], exp_up[0], exp_down[0],
                                           EXPERT_ROWS)
    ys = _expert_call(xs, tables, y_tail, wg, wu, wd, EXPERT_ROWS)
    xt = _combine_call(ys, dest, xt, top_w, modr, row, 5, seq)
    return xt.reshape(bsz, seq, d)
```

```python
import functools
import math

import jax
import jax.numpy as jnp
from jax import lax
from jax.experimental import pallas as pl
from jax.experimental.pallas import tpu as pltpu

N_HEADS = 16
CONV_WIDTH = 3
MOBA_BLOCK = 256
MOBA_TOPK = 3
ROPE_THETA = 10000.0
EXPERT_TOPK = 2
EXPERT_ROWS = 512
ROW_DMA_UNROLL = 8
NORM_EPS = 1e-6
NEG_INF = -1e30

LANES = 128
SUBLANES = 8
BF16_SUBLANES = 16
MIB = 1024 * 1024

F32 = jnp.float32
BF16 = jnp.bfloat16

_NT_DIMS = (((1,), (1,)), ((), ()))


def _params(semantics, vmem_mib):
    return pltpu.CompilerParams(dimension_semantics=semantics,
                                vmem_limit_bytes=vmem_mib * MIB)


def _dot(a, b):
    return jnp.dot(a, b, preferred_element_type=F32)


def _silu(a):
    return a * (1.0 / (1.0 + jnp.exp(-a)))


def _cast_weight_once(w_ref, w_scr):
    @pl.when(pl.program_id(1) == 0)
    def _():
        w_scr[...] = w_ref[...].astype(BF16)


def _mod_kernel(c_ref, w_ref, b_ref, o_ref):
    ca = _silu(c_ref[...])
    o_ref[0] = _dot(ca.astype(BF16), w_ref[0].astype(BF16)) + b_ref[0]


def _mod_call(c, mod_w, mod_b, tn=1024):
    depth, d, n = mod_w.shape
    bsz = c.shape[0]
    return pl.pallas_call(
        _mod_kernel,
        grid=(depth, n // tn),
        in_specs=[
            pl.BlockSpec((bsz, d), lambda l, j: (0, 0)),
            pl.BlockSpec((1, d, tn), lambda l, j: (l, 0, j)),
            pl.BlockSpec((1, 1, tn), lambda l, j: (l, 0, j)),
        ],
        out_specs=pl.BlockSpec((1, bsz, tn), lambda l, j: (l, 0, j)),
        out_shape=jax.ShapeDtypeStruct((depth, bsz, n), F32),
        compiler_params=_params(("arbitrary", "arbitrary"), 40),
        name="adaln_mod",
    )(c, mod_w, mod_b.reshape(depth, 1, n))


def _normmod(x, gain, scale, shift):
    ms = jnp.mean(x * x, axis=-1, keepdims=True)
    y = x * lax.rsqrt(ms + NORM_EPS) * gain
    return y * (1.0 + scale) + shift


def _normmod_kernel(x_ref, gain_ref, sc_ref, sh_ref, o_ref):
    o_ref[...] = _normmod(x_ref[...], gain_ref[...], sc_ref[0], sh_ref[0]).astype(o_ref.dtype)


def _mod_spec(d, rows_per_batch, row, chunk):
    return pl.BlockSpec((1, 1, d), lambda i, *_: (row + i // rows_per_batch, 0, chunk))


def _normmod_call(x, gain, modr, row, sc_chunk, sh_chunk, seq, tm=512):
    t, d = x.shape
    rpb = seq // tm
    return pl.pallas_call(
        _normmod_kernel,
        grid=(t // tm,),
        in_specs=[
            pl.BlockSpec((tm, d), lambda i: (i, 0)),
            pl.BlockSpec((1, d), lambda i: (0, 0)),
            _mod_spec(d, rpb, row, sc_chunk),
            _mod_spec(d, rpb, row, sh_chunk),
        ],
        out_specs=pl.BlockSpec((tm, d), lambda i: (i, 0)),
        out_shape=jax.ShapeDtypeStruct((t, d), BF16),
        compiler_params=_params(("arbitrary",), 32),
        name="normmod",
    )(x, gain.reshape(1, d), modr, modr)


def _convin_kernel(h_ref, wb_ref, wc_ref, wv_ref, cw_ref, o_ref, u_scr, wb_scr, wc_scr, wv_scr):
    seq = h_ref.shape[0]
    for w_ref, w_scr in ((wb_ref, wb_scr), (wc_ref, wc_scr), (wv_ref, wv_scr)):
        _cast_weight_once(w_ref, w_scr)
    h = h_ref[...]
    u = _dot(h, wc_scr[...]) * _dot(h, wv_scr[...])
    u_scr[0:SUBLANES, :] = jnp.zeros((SUBLANES, u.shape[1]), F32)
    u_scr[SUBLANES:, :] = u
    cw = cw_ref[...]
    conv = cw[CONV_WIDTH - 1:CONV_WIDTH, :] * u
    for tap in range(1, CONV_WIDTH):
        conv = conv + (cw[CONV_WIDTH - 1 - tap:CONV_WIDTH - tap, :]
                       * u_scr[SUBLANES - tap:SUBLANES - tap + seq, :])
    o_ref[...] = (_dot(h, wb_scr[...]) * conv).astype(o_ref.dtype)


def _convin_call(h, w_in, conv_w, seq, tn=256):
    t, d = h.shape
    nj = d // tn
    return pl.pallas_call(
        _convin_kernel,
        grid=(nj, t // seq),
        in_specs=[
            pl.BlockSpec((seq, d), lambda j, b: (b, 0)),
            pl.BlockSpec((d, tn), lambda j, b: (0, j)),
            pl.BlockSpec((d, tn), lambda j, b: (0, j + nj)),
            pl.BlockSpec((d, tn), lambda j, b: (0, j + 2 * nj)),
            pl.BlockSpec((CONV_WIDTH, tn), lambda j, b: (0, j)),
        ],
        out_specs=pl.BlockSpec((seq, tn), lambda j, b: (b, j)),
        out_shape=jax.ShapeDtypeStruct((t, d), BF16),
        scratch_shapes=[pltpu.VMEM((seq + SUBLANES, tn), F32)] + [pltpu.VMEM((d, tn), BF16)] * 3,
        compiler_params=_params(("arbitrary", "arbitrary"), 54),
        name="conv_in",
    )(h, w_in, w_in, w_in, conv_w)


def _resmm_norm_kernel(a_ref, w_ref, x_ref, g_ref, gain_ref, sc_ref, sh_ref, o_ref, h_ref):
    xn = x_ref[...] + g_ref[0] * _dot(a_ref[...], w_ref[...])
    o_ref[...] = xn
    h_ref[...] = _normmod(xn, gain_ref[...], sc_ref[0], sh_ref[0]).astype(h_ref.dtype)


def _resmm_route_kernel(a_ref, w_ref, x_ref, g_ref, gain_ref, sc_ref, sh_ref, wr_ref, br_ref,
                        o_ref, h_ref, idx_ref, wt_ref, *, n_experts):
    xn = x_ref[...] + g_ref[0] * _dot(a_ref[...], w_ref[...])
    o_ref[...] = xn
    h = _normmod(xn, gain_ref[...], sc_ref[0], sh_ref[0])
    h_ref[...] = h
    idx_ref[...], wt_ref[...] = _route(h, wr_ref[...], br_ref[...], n_experts)


def _resmm_call(a, w, x, modr, row, g_chunk, seq, next_norm, router=None, tm=512):
    t, k = a.shape
    d = w.shape[1]
    rpb = seq // tm
    gain, nrow, sc_chunk, sh_chunk = next_norm
    in_specs = [
        pl.BlockSpec((tm, k), lambda i: (i, 0)),
        pl.BlockSpec((k, d), lambda i: (0, 0)),
        pl.BlockSpec((tm, d), lambda i: (i, 0)),
        _mod_spec(d, rpb, row, g_chunk),
        pl.BlockSpec((1, d), lambda i: (0, 0)),
        _mod_spec(d, rpb, nrow, sc_chunk),
        _mod_spec(d, rpb, nrow, sh_chunk),
    ]
    args = (a, w, x, modr, gain.reshape(1, d), modr, modr)
    row_spec = pl.BlockSpec((tm, d), lambda i: (i, 0))
    lane_spec = pl.BlockSpec((tm, LANES), lambda i: (i, 0))
    cp = _params(("arbitrary",), 56)
    if router is None:
        return pl.pallas_call(
            _resmm_norm_kernel, grid=(t // tm,), in_specs=in_specs,
            out_specs=[row_spec, row_spec],
            out_shape=[jax.ShapeDtypeStruct((t, d), F32), jax.ShapeDtypeStruct((t, d), BF16)],
            compiler_params=cp, name="res_matmul_norm")(*args)
    w_router, b_router = router
    n_experts = w_router.shape[1]
    wr = jnp.zeros((d, LANES), F32).at[:, :n_experts].set(w_router)
    br = jnp.zeros((1, LANES), F32).at[0, :n_experts].set(b_router)
    in_specs += [pl.BlockSpec((d, LANES), lambda i: (0, 0)),
                 pl.BlockSpec((1, LANES), lambda i: (0, 0))]
    return pl.pallas_call(
        functools.partial(_resmm_route_kernel, n_experts=n_experts), grid=(t // tm,),
        in_specs=in_specs, out_specs=[row_spec, row_spec, lane_spec, lane_spec],
        out_shape=[jax.ShapeDtypeStruct((t, d), F32), jax.ShapeDtypeStruct((t, d), F32),
                   jax.ShapeDtypeStruct((t, LANES), jnp.int32),
                   jax.ShapeDtypeStruct((t, LANES), F32)],
        compiler_params=cp, name="res_matmul_route")(*args, wr, br)


def _ffn_kernel(h_ref, wg_ref, wu_ref, wd_ref, x_ref, g_ref, gain_ref, sc_ref, sh_ref,
                o_ref, hn_ref):
    j = pl.program_id(1)
    h = h_ref[...]
    act = _silu(_dot(h, wg_ref[...])) * _dot(h, wu_ref[...])
    part = _dot(act.astype(BF16), wd_ref[...])

    @pl.when(j == 0)
    def _():
        o_ref[...] = part

    @pl.when(j > 0)
    def _():
        o_ref[...] += part

    @pl.when(j == pl.num_programs(1) - 1)
    def _():
        xn = x_ref[...] + g_ref[0] * o_ref[...]
        o_ref[...] = xn
        hn_ref[...] = _normmod(xn, gain_ref[...], sc_ref[0], sh_ref[0]).astype(hn_ref.dtype)


def _ffn_call(h, wg, wu, wd, x, modr, row, g_chunk, seq, next_norm, tm=512, tf=512):
    t, d = h.shape
    f = wg.shape[1]
    rpb = seq // tm
    gain, nrow, sc_chunk, sh_chunk = next_norm
    row_spec = pl.BlockSpec((tm, d), lambda i, j: (i, 0))
    return pl.pallas_call(
        _ffn_kernel,
        grid=(t // tm, f // tf),
        in_specs=[
            row_spec,
            pl.BlockSpec((d, tf), lambda i, j: (0, j)),
            pl.BlockSpec((d, tf), lambda i, j: (0, j)),
            pl.BlockSpec((tf, d), lambda i, j: (j, 0)),
            row_spec,
            _mod_spec(d, rpb, row, g_chunk),
            pl.BlockSpec((1, d), lambda i, j: (0, 0)),
            _mod_spec(d, rpb, nrow, sc_chunk),
            _mod_spec(d, rpb, nrow, sh_chunk),
        ],
        out_specs=[row_spec, row_spec],
        out_shape=[jax.ShapeDtypeStruct((t, d), F32), jax.ShapeDtypeStruct((t, d), BF16)],
        compiler_params=_params(("arbitrary", "arbitrary"), 52),
        name="dense_ffn",
    )(h, wg, wu, wd, x, modr, gain.reshape(1, d), modr, modr)


def _qk_kernel(h_ref, w_ref, gain_ref, cos_ref, sin_ref, ones_ref, perm_ref, o_ref, w_scr,
               *, head_dim):
    _cast_weight_once(w_ref, w_scr)
    acc = _dot(h_ref[...], w_scr[...])
    pair = ones_ref.shape[0]
    gain = gain_ref[...]
    cos = cos_ref[...]
    sin = sin_ref[...]
    ones = ones_ref[...]
    perm = perm_ref[...]
    for p0 in range(0, acc.shape[1], pair):
        xp = acc[:, p0:p0 + pair]
        ss = _dot((xp * xp).astype(BF16), ones)
        inv = lax.rsqrt(ss * (1.0 / head_dim) + NORM_EPS)
        z = xp * gain
        rot = _dot(z.astype(BF16), perm)
        out = (z * cos + rot * sin) * inv
        o_ref[:, p0:p0 + pair] = out.astype(o_ref.dtype)


def _v_kernel(h_ref, w_ref, o_ref, w_scr):
    _cast_weight_once(w_ref, w_scr)
    o_ref[...] = _dot(h_ref[...], w_scr[...]).astype(o_ref.dtype)


def _head_pair_matrices(head_dim):
    pair = 2 * head_dim
    src = jnp.arange(pair)[:, None]
    dst = jnp.arange(pair)[None, :]
    same_head = (src // head_dim) == (dst // head_dim)
    ones = same_head.astype(BF16)
    perm = (same_head & ((src % head_dim) == ((dst + head_dim // 2) % head_dim))).astype(BF16)
    return ones, perm


def _qkv_call(h, w_qkv, which, seq, head_dim, gain=None, cos=None, sin=None, tm=1024, tn=512):
    t, d = h.shape
    nj = d // tn
    off = which * nj
    rpb = seq // tm
    h_spec = pl.BlockSpec((tm, d), lambda j, i: (i, 0))
    w_spec = pl.BlockSpec((d, tn), lambda j, i: (0, j + off))
    out_spec = pl.BlockSpec((tm, tn), lambda j, i: (i, j))
    out_shape = jax.ShapeDtypeStruct((t, d), BF16)
    cp = _params(("arbitrary", "arbitrary"), 40)
    w_scratch = [pltpu.VMEM((d, tn), BF16)]
    if gain is None:
        return pl.pallas_call(_v_kernel, grid=(nj, t // tm), in_specs=[h_spec, w_spec],
                              out_specs=out_spec, out_shape=out_shape,
                              scratch_shapes=w_scratch, compiler_params=cp,
                              name="v_proj")(h, w_qkv)
    pair = 2 * head_dim
    ones, perm = _head_pair_matrices(head_dim)
    tab_spec = pl.BlockSpec((tm, pair), lambda j, i: (i % rpb, 0))
    const_spec = pl.BlockSpec((pair, pair), lambda j, i: (0, 0))
    return pl.pallas_call(
        functools.partial(_qk_kernel, head_dim=head_dim),
        grid=(nj, t // tm),
        in_specs=[h_spec, w_spec, pl.BlockSpec((1, pair), lambda j, i: (0, 0)),
                  tab_spec, tab_spec, const_spec, const_spec],
        out_specs=out_spec, out_shape=out_shape, scratch_shapes=w_scratch, compiler_params=cp,
        name="qk_proj",
    )(h, w_qkv, jnp.tile(gain.reshape(1, head_dim), (1, 2)), cos, sin, ones, perm)


def _attn_kernel(q_ref, k_ref, v_ref, o_ref, vt_scr, s_scr):
    seq, hd = q_ref.shape
    blk = MOBA_BLOCK
    nb = seq // blk

    rows = []
    for n in range(nb):
        vt_scr[n, 0:hd, :] = v_ref[n * blk:(n + 1) * blk, :].astype(F32).T.astype(BF16)
        vt_scr[n, hd:, :] = jnp.ones((vt_scr.shape[1] - hd, blk), BF16)
        kb = k_ref[n * blk:(n + 1) * blk, :].astype(F32)
        rows.append(jnp.sum(kb, axis=0, keepdims=True) * (1.0 / blk))
    km = jnp.concatenate(rows, axis=0)
    km_a = km.astype(BF16)
    rem = km - km_a.astype(F32)
    km_b = rem.astype(BF16)
    km_c = (rem - km_b.astype(F32)).astype(BF16)
    km_parts = jnp.concatenate([km_a, km_b, km_c, jnp.zeros_like(km_a)], axis=0)

    causal = (lax.broadcasted_iota(jnp.int32, (blk, blk), 0)
              <= lax.broadcasted_iota(jnp.int32, (blk, blk), 1))

    def masked_scores(qi):
        qb = q_ref[qi * blk:(qi + 1) * blk, :]
        sel = None
        if qi > MOBA_TOPK:
            g3 = lax.dot_general(km_parts, qb, _NT_DIMS, preferred_element_type=F32)
            gate = g3[0:nb] + g3[nb:2 * nb] + g3[2 * nb:3 * nb]
            g_row = [gate[n:n + 1, :] for n in range(qi)]
            rank = [jnp.zeros((1, blk), F32) for _ in range(qi)]
            for a in range(qi):
                for b in range(a + 1, qi):
                    a_first = (g_row[a] >= g_row[b]).astype(F32)
                    rank[b] = rank[b] + a_first
                    rank[a] = rank[a] + (1.0 - a_first)
            sel = [rank[n] < MOBA_TOPK for n in range(qi)]

        m_run = None
        for n in range(qi + 1):
            s = lax.dot_general(k_ref[n * blk:(n + 1) * blk, :], qb, _NT_DIMS,
                                preferred_element_type=F32)
            if n == qi:
                s = jnp.where(causal, s, NEG_INF)
            elif sel is not None:
                s = jnp.where(sel[n], s, NEG_INF)
            s_scr[qi % 2, n] = s
            cm = jnp.max(s, axis=0, keepdims=True)
            m_run = cm if m_run is None else jnp.maximum(m_run, cm)
        return m_run

    def weighted_values(qi, m_run):
        acc = None
        for n in range(qi + 1):
            p = jnp.exp2(s_scr[qi % 2, n] - m_run)
            pv = _dot(vt_scr[n], p.astype(BF16))
            acc = pv if acc is None else acc + pv
        out_t = acc[0:hd] * (1.0 / acc[hd:hd + 1])
        o_ref[qi * blk:(qi + 1) * blk, :] = out_t.T.astype(o_ref.dtype)

    m_next = masked_scores(0)
    for qi in range(nb):
        m_cur = m_next
        if qi + 1 < nb:
            m_next = masked_scores(qi + 1)
        weighted_values(qi, m_cur)


def _attn_call(q, k, v, seq, head_dim):
    t, d = q.shape
    nh = d // head_dim
    nb = seq // MOBA_BLOCK
    spec = pl.BlockSpec((seq, head_dim), lambda b, h: (b, h))
    return pl.pallas_call(
        _attn_kernel,
        grid=(t // seq, nh),
        in_specs=[spec, spec, spec],
        out_specs=spec,
        out_shape=jax.ShapeDtypeStruct((t, d), BF16),
        scratch_shapes=[pltpu.VMEM((nb, head_dim + BF16_SUBLANES, MOBA_BLOCK), BF16),
                        pltpu.VMEM((2, nb, MOBA_BLOCK, MOBA_BLOCK), F32)],
        compiler_params=_params(("arbitrary", "arbitrary"), 32),
        name="moba_attention",
    )(q, k, v)


def _route(h, w, bias, n_experts):
    h_a = h.astype(BF16)
    h_b = (h - h_a.astype(F32)).astype(BF16)
    w_a = w.astype(BF16)
    w_b = (w - w_a.astype(F32)).astype(BF16)
    logits = _dot(h_a, w_a) + _dot(h_b, w_a) + _dot(h_a, w_b) + bias
    lane = lax.broadcasted_iota(jnp.int32, logits.shape, 1)
    logits = jnp.where(lane < n_experts, logits, -jnp.inf)
    m1 = jnp.max(logits, axis=-1, keepdims=True)
    i1 = jnp.min(jnp.where(logits == m1, lane, LANES), axis=-1, keepdims=True)
    rest = jnp.where(lane == i1, -jnp.inf, logits)
    m2 = jnp.max(rest, axis=-1, keepdims=True)
    i2 = jnp.min(jnp.where(rest == m2, lane, LANES), axis=-1, keepdims=True)
    e2 = jnp.exp(m2 - m1)
    w1 = 1.0 / (1.0 + e2)
    w2 = e2 * w1
    return (jnp.where(lane == 0, i1, jnp.where(lane == 1, i2, 0)),
            jnp.where(lane == 0, w1, jnp.where(lane == 1, w2, 0.0)))


def _dispatch_kernel(zb_ref, nz_ref, dest_ref, h_ref, xs_ref, zbuf, sem, zsem):
    tm = h_ref.shape[0]
    zrows = zbuf.shape[0]

    @pl.when(pl.program_id(0) == 0)
    def _():
        zbuf[...] = jnp.zeros(zbuf.shape, zbuf.dtype)

        def zero_copy(k):
            row0 = pl.multiple_of(zb_ref[k] * zrows, zrows)
            return pltpu.make_async_copy(zbuf, xs_ref.at[pl.ds(row0, zrows), :], zsem)

        def start(k, c):
            zero_copy(k).start()
            return c

        def wait(k, c):
            zero_copy(k).wait()
            return c

        lax.fori_loop(0, nz_ref[0], start, 0)
        lax.fori_loop(0, nz_ref[0], wait, 0)

    def issue(r, c):
        for k in range(EXPERT_TOPK):
            d = dest_ref[0, 0, EXPERT_TOPK * r + k]
            pltpu.make_async_copy(h_ref.at[pl.ds(r, 1), :], xs_ref.at[pl.ds(d, 1), :],
                                  sem).start(priority=k % 2)
        return c

    lax.fori_loop(0, tm, issue, 0, unroll=ROW_DMA_UNROLL)
    for k in range(EXPERT_TOPK):
        pltpu.make_async_copy(h_ref, xs_ref.at[pl.ds(0, tm), :], sem).wait()


def _dispatch_call(h, dest, tables, n_rows, block_rows, tm=256):
    t, d = h.shape
    nt = t // tm
    grid_spec = pltpu.PrefetchScalarGridSpec(
        num_scalar_prefetch=2,
        grid=(nt,),
        in_specs=[
            pl.BlockSpec((1, 1, EXPERT_TOPK * tm), lambda i, zb, nz: (i, 0, 0),
                         memory_space=pltpu.SMEM),
            pl.BlockSpec((tm, d), lambda i, zb, nz: (i, 0)),
        ],
        out_specs=pl.BlockSpec(memory_space=pl.ANY),
        scratch_shapes=[pltpu.VMEM((block_rows, d), h.dtype), pltpu.SemaphoreType.DMA,
                        pltpu.SemaphoreType.DMA],
    )
    return pl.pallas_call(
        _dispatch_kernel,
        grid_spec=grid_spec,
        out_shape=jax.ShapeDtypeStruct((n_rows, d), h.dtype),
        compiler_params=_params(("arbitrary",), 32),
        name="moe_dispatch",
    )(tables["zero_blocks"], tables["n_zero"], dest.reshape(nt, 1, EXPERT_TOPK * tm), h)


BLOCK_COMPUTE, BLOCK_TAIL, BLOCK_UNUSED = 0, 1, 2


def _expert_kernel(kind_ref, xmap_ref, we_ref, tmap_ref, xs_ref, wg_ref, wu_ref, wd_ref,
                   ytail_ref, ys_ref, xb_scr):
    del xmap_ref, we_ref, tmap_ref
    i = pl.program_id(0)
    j = pl.program_id(1)
    kind = kind_ref[i]

    @pl.when((kind == BLOCK_UNUSED) & (j == 0))
    def _():
        ys_ref[...] = jnp.zeros(ys_ref.shape, F32)

    @pl.when((kind == BLOCK_TAIL) & (j == 0))
    def _():
        ys_ref[...] = ytail_ref[...]

    @pl.when(kind == BLOCK_COMPUTE)
    def _():
        @pl.when(j == 0)
        def _():
            xb_scr[...] = xs_ref[...].astype(BF16)
            ys_ref[...] = jnp.zeros(ys_ref.shape, F32)

        x = xb_scr[...]
        act = _silu(_dot(x, wg_ref[0])) * _dot(x, wu_ref[0])
        ys_ref[...] += _dot(act.astype(BF16), wd_ref[0])


def _expert_call(xs, tables, y_tail, wg, wu, wd, tm, tf=512):
    n_rows, d = xs.shape
    f = wg.shape[2]
    nf = f // tf

    def wj(i, j, kind):
        return jnp.where(kind[i] == BLOCK_COMPUTE, j, nf - 1)

    grid_spec = pltpu.PrefetchScalarGridSpec(
        num_scalar_prefetch=4,
        grid=(n_rows // tm, nf),
        in_specs=[
            pl.BlockSpec((tm, d), lambda i, j, kind, xm, we, fm: (xm[i], 0)),
            pl.BlockSpec((1, d, tf), lambda i, j, kind, xm, we, fm: (we[i], 0, wj(i, j, kind))),
            pl.BlockSpec((1, d, tf), lambda i, j, kind, xm, we, fm: (we[i], 0, wj(i, j, kind))),
            pl.BlockSpec((1, tf, d), lambda i, j, kind, xm, we, fm: (we[i], wj(i, j, kind), 0)),
            pl.BlockSpec((tm, d), lambda i, j, kind, xm, we, fm: (fm[i], 0)),
        ],
        out_specs=pl.BlockSpec((tm, d), lambda i, j, kind, xm, we, fm: (i, 0)),
        scratch_shapes=[pltpu.VMEM((tm, d), BF16)],
    )
    return pl.pallas_call(
        _expert_kernel,
        grid_spec=grid_spec,
        out_shape=jax.ShapeDtypeStruct((n_rows, d), F32),
        compiler_params=_params(("arbitrary", "arbitrary"), 52),
        name="moe_experts",
    )(tables["kind"], tables["x_map"], tables["w_expert"], tables["tail_map"],
      xs, wg, wu, wd, y_tail)


TAIL_BLOCKS = 2


def _expert_tail_kernel(tb_ref, nt_ref, *refs):
    del tb_ref
    xs_refs = refs[:TAIL_BLOCKS]
    wg_ref, wu_ref, wd_ref, ys_ref, wgb_ref, wub_ref, wdb_ref, xb_scr = refs[TAIL_BLOCKS:]
    tm = xs_refs[0].shape[0]
    e = pl.program_id(0)
    j = pl.program_id(1)

    @pl.when(j == 0)
    def _():
        for k, xs_ref in enumerate(xs_refs):
            xb_scr[k * tm:(k + 1) * tm, :] = xs_ref[...].astype(BF16)
        ys_ref[...] = jnp.zeros(ys_ref.shape, F32)

    x = xb_scr[...]
    wg = wg_ref[0].astype(BF16)
    wu = wu_ref[0].astype(BF16)
    act = _silu(_dot(x, wg)) * _dot(x, wu)
    wd = wd_ref[0].astype(BF16)
    part = _dot(act.astype(BF16), wd)
    for k in range(TAIL_BLOCKS):
        rows = slice(k * tm, (k + 1) * tm)
        ys_ref[rows, :] += jnp.where(nt_ref[e] >= TAIL_BLOCKS - k, part[rows, :], 0.0)
    wgb_ref[0] = wg
    wub_ref[0] = wu
    wdb_ref[0] = wd


def _expert_tail_call(xs, tables, wg, wu, wd, tm, tf=256):
    d = xs.shape[1]
    n_experts, _, f = wg.shape

    def xs_spec(k):
        return pl.BlockSpec((tm, d), lambda e, j, tb, nt: (tb[TAIL_BLOCKS * e + k], 0),
                            pipeline_mode=pl.Buffered(1))

    w_in = lambda e, j, tb, nt: (e, 0, j)
    w_out = lambda e, j, tb, nt: (e, j, 0)
    grid_spec = pltpu.PrefetchScalarGridSpec(
        num_scalar_prefetch=2,
        grid=(n_experts, f // tf),
        in_specs=[xs_spec(k) for k in range(TAIL_BLOCKS)] + [
            pl.BlockSpec((1, d, tf), w_in),
            pl.BlockSpec((1, d, tf), w_in),
            pl.BlockSpec((1, tf, d), w_out),
        ],
        out_specs=[
            pl.BlockSpec((TAIL_BLOCKS * tm, d), lambda e, j, tb, nt: (e, 0)),
            pl.BlockSpec((1, d, tf), w_in),
            pl.BlockSpec((1, d, tf), w_in),
            pl.BlockSpec((1, tf, d), w_out),
        ],
        scratch_shapes=[pltpu.VMEM((TAIL_BLOCKS * tm, d), BF16)],
    )
    return pl.pallas_call(
        _expert_tail_kernel,
        grid_spec=grid_spec,
        out_shape=[
            jax.ShapeDtypeStruct((n_experts * TAIL_BLOCKS * tm, d), F32),
            jax.ShapeDtypeStruct(wg.shape, BF16),
            jax.ShapeDtypeStruct(wu.shape, BF16),
            jax.ShapeDtypeStruct(wd.shape, BF16),
        ],
        compiler_params=_params(("arbitrary", "arbitrary"), 58),
        name="moe_experts_tail",
    )(tables["tail_block"], tables["n_tail"], *([xs] * TAIL_BLOCKS), wg, wu, wd)


def _combine_kernel(dest_ref, dest_next_ref, ys_ref, x_ref, wt_ref, g_ref, o_ref, ybuf, sems):
    tm = x_ref.shape[0]
    i = pl.program_id(0)
    slot = i % 2

    def start_gathers(d_ref, dst_slot):
        def issue(r, c):
            for k in range(EXPERT_TOPK):
                d = d_ref[0, 0, EXPERT_TOPK * r + k]
                pltpu.make_async_copy(ys_ref.at[pl.ds(d, 1), :],
                                      ybuf.at[dst_slot, k, pl.ds(r, 1), :],
                                      sems.at[dst_slot]).start(priority=k % 2)
            return c

        lax.fori_loop(0, tm, issue, 0, unroll=ROW_DMA_UNROLL)

    @pl.when(i == 0)
    def _():
        start_gathers(dest_ref, 0)

    @pl.when(i + 1 < pl.num_programs(0))
    def _():
        start_gathers(dest_next_ref, 1 - slot)

    for k in range(EXPERT_TOPK):
        pltpu.make_async_copy(ys_ref.at[pl.ds(0, tm), :], ybuf.at[slot, k], sems.at[slot]).wait()
    wt = wt_ref[...]
    y = wt[:, 0:1] * ybuf[slot, 0]
    for k in range(1, EXPERT_TOPK):
        y = y + wt[:, k:k + 1] * ybuf[slot, k]
    o_ref[...] = x_ref[...] + g_ref[0] * y


def _combine_call(ys, dest, x, wt, modr, row, g_chunk, seq, tm=256):
    t, d = x.shape
    nt = t // tm
    rpb = seq // tm
    dest3 = dest.reshape(nt, 1, EXPERT_TOPK * tm)
    return pl.pallas_call(
        _combine_kernel,
        grid=(nt,),
        in_specs=[
            pl.BlockSpec((1, 1, EXPERT_TOPK * tm), lambda i: (i, 0, 0), memory_space=pltpu.SMEM),
            pl.BlockSpec((1, 1, EXPERT_TOPK * tm), lambda i: (jnp.minimum(i + 1, nt - 1), 0, 0),
                         memory_space=pltpu.SMEM),
            pl.BlockSpec(memory_space=pl.ANY),
            pl.BlockSpec((tm, d), lambda i: (i, 0)),
            pl.BlockSpec((tm, LANES), lambda i: (i, 0)),
            _mod_spec(d, rpb, row, g_chunk),
        ],
        out_specs=pl.BlockSpec((tm, d), lambda i: (i, 0)),
        out_shape=jax.ShapeDtypeStruct((t, d), F32),
        scratch_shapes=[pltpu.VMEM((2, EXPERT_TOPK, tm, d), F32), pltpu.SemaphoreType.DMA((2,))],
        compiler_params=_params(("arbitrary",), 40),
        name="moe_combine",
    )(dest3, dest3, ys, x, wt, modr)


def _routing_tables(top_idx, n_experts, tm, n_blocks):
    flat_e = top_idx.reshape(-1)
    onehot = (flat_e[:, None] == jnp.arange(n_experts, dtype=jnp.int32)[None, :]).astype(jnp.int32)
    csum = jnp.cumsum(onehot, axis=0)
    counts = csum[-1]
    pos = jnp.sum((csum - 1) * onehot, axis=1)
    nblk = (counts + tm - 1) // tm
    blk_end = jnp.cumsum(nblk)
    blk_start = blk_end - nblk
    dest = (blk_start[flat_e] * tm + pos).astype(jnp.int32)
    n_used = blk_end[-1]

    blocks = jnp.arange(n_blocks, dtype=jnp.int32)
    x_map = jnp.minimum(blocks, n_used - 1)
    blk_e = jnp.minimum(jnp.sum(blk_end[None, :] <= x_map[:, None], axis=1), n_experts - 1)
    from_end = blk_end[blk_e] - blocks
    is_tail = (from_end <= TAIL_BLOCKS) & (blocks < n_used)
    kind = jnp.where(blocks >= n_used, BLOCK_UNUSED,
                     jnp.where(is_tail, BLOCK_TAIL, BLOCK_COMPUTE))
    last_compute = lax.cummax(jnp.where(kind == BLOCK_COMPUTE, blocks, -1))
    w_expert = blk_e[jnp.maximum(last_compute, 0)]
    tail_map = lax.cummax(jnp.where(is_tail, TAIL_BLOCKS * (blk_e + 1) - from_end, 0))
    slot_from_end = TAIL_BLOCKS - jnp.arange(TAIL_BLOCKS, dtype=jnp.int32)
    tail_block = jnp.clip(blk_end[:, None] - slot_from_end[None, :], 0, n_blocks - 1)
    cand = jnp.concatenate([blk_end - 1, blocks])
    cand_ok = jnp.concatenate([nblk > 0, blocks >= n_used])
    zero_blocks = jnp.maximum(cand[jnp.argsort(~cand_ok, stable=True)], 0)
    i32 = lambda a: a.astype(jnp.int32)
    tables = dict(kind=i32(kind), x_map=i32(x_map), w_expert=i32(w_expert),
                  tail_map=i32(tail_map), tail_block=i32(tail_block.reshape(-1)),
                  n_tail=i32(jnp.minimum(nblk, TAIL_BLOCKS)),
                  zero_blocks=i32(zero_blocks), n_zero=i32(jnp.sum(cand_ok)).reshape(1))
    return dest, tables


def _rope_tables(seq, head_dim, scale=1.0):
    half = head_dim // 2
    inv_freq = jnp.exp(-math.log(ROPE_THETA) * jnp.arange(half, dtype=F32) / half)
    ang = jnp.arange(seq).astype(F32)[:, None] * inv_freq[None, :]
    cos, sin = jnp.cos(ang) * scale, jnp.sin(ang) * scale
    return (jnp.concatenate([cos, cos, cos, cos], axis=-1),
            jnp.concatenate([-sin, sin, -sin, sin], axis=-1))


def kernel(x, c, mod_w, mod_b, norm_mix, norm_ffn, conv_in, conv_w, conv_out, ffn_gate, ffn_up,
           ffn_down, qkv_w, q_norm, k_norm, attn_out, router_w, router_b, exp_gate, exp_up,
           exp_down):
    bsz, seq, d = x.shape
    depth = mod_w.shape[0]
    assert depth == 2, "layer 0 = short-conv + dense FFN, layer 1 = MoBA + MoE"
    t = bsz * seq
    head_dim = d // N_HEADS
    n_experts = router_w.shape[-1]

    mod = _mod_call(c, mod_w, mod_b)
    modr = mod.reshape(depth * bsz, 1, 6 * d)
    xt = x.reshape(t, d)

    row = bsz
    h = _normmod_call(xt, norm_mix[0], modr, 0, 1, 0, seq)
    bu = _convin_call(h, conv_in[0], conv_w[0], seq)
    xt, h = _resmm_call(bu, conv_out[0].astype(BF16), xt, modr, 0, 2, seq,
                        next_norm=(norm_ffn[0], 0, 4, 3))
    xt, h = _ffn_call(h, ffn_gate[0].astype(BF16), ffn_up[0].astype(BF16),
                      ffn_down[0].astype(BF16), xt, modr, 0, 5, seq,
                      next_norm=(norm_mix[1], row, 1, 0))

    w_qkv = qkv_w[0]
    cos_q, sin_q = _rope_tables(seq, head_dim, head_dim ** -0.5 * math.log2(math.e))
    cos_k, sin_k = _rope_tables(seq, head_dim)
    q = _qkv_call(h, w_qkv, 0, seq, head_dim, q_norm[0], cos_q, sin_q)
    k = _qkv_call(h, w_qkv, 1, seq, head_dim, k_norm[0], cos_k, sin_k)
    v = _qkv_call(h, w_qkv, 2, seq, head_dim)
    o = _attn_call(q, k, v, seq, head_dim)
    xt, hf, top_idx, top_w = _resmm_call(o, attn_out[0].astype(BF16), xt, modr, row, 2, seq,
                                         next_norm=(norm_ffn[1], row, 4, 3),
                                         router=(router_w[0], router_b[0]))
    n_blocks = (t * EXPERT_TOPK) // EXPERT_ROWS + n_experts
    dest, tables = _routing_tables(top_idx[:, :EXPERT_TOPK], n_experts, EXPERT_ROWS, n_blocks)
    xs = _dispatch_call(hf, dest, tables, n_blocks * EXPERT_ROWS, EXPERT_ROWS)
    y_tail, wg, wu, wd = _expert_tail_call(xs, tables, exp_gate[0], exp_up[0], exp_down[0],
                                           EXPERT_ROWS)
    ys = _expert_call(xs, tables, y_tail, wg, wu, wd, EXPERT_ROWS)
    xt = _combine_call(ys, dest, xt, top_w, modr, row, 5, seq)
    return xt.reshape(bsz, seq, d)
```

```python
import functools
import math

import jax
import jax.numpy as jnp
from jax import lax
from jax.experimental import pallas as pl
from jax.experimental.pallas import tpu as pltpu

N_HEADS = 16
CONV_WIDTH = 3
MOBA_BLOCK = 256
MOBA_TOPK = 3
ROPE_THETA = 10000.0
EXPERT_TOPK = 2
EXPERT_ROWS = 512
EXPERT_FF_TILE = 512
ROW_DMA_UNROLL = 8
NORM_EPS = 1e-6
NEG_INF = -1e30

LANES = 128
SUBLANES = 8
BF16_SUBLANES = 16
MIB = 1024 * 1024

F32 = jnp.float32
BF16 = jnp.bfloat16

_NT_DIMS = (((1,), (1,)), ((), ()))


def _params(semantics, vmem_mib):
    return pltpu.CompilerParams(dimension_semantics=semantics,
                                vmem_limit_bytes=vmem_mib * MIB)


def _dot(a, b):
    return jnp.dot(a, b, preferred_element_type=F32)


def _silu(a):
    return a * (1.0 / (1.0 + jnp.exp(-a)))


def _cast_weight_once(w_ref, w_scr):
    @pl.when(pl.program_id(1) == 0)
    def _():
        w_scr[...] = w_ref[...].astype(BF16)


def _mod_kernel(c_ref, w_ref, b_ref, o_ref):
    ca = _silu(c_ref[...])
    o_ref[0] = _dot(ca.astype(BF16), w_ref[0].astype(BF16)) + b_ref[0]


def _mod_call(c, mod_w, mod_b, tn=1024):
    depth, d, n = mod_w.shape
    bsz = c.shape[0]
    return pl.pallas_call(
        _mod_kernel,
        grid=(depth, n // tn),
        in_specs=[
            pl.BlockSpec((bsz, d), lambda l, j: (0, 0)),
            pl.BlockSpec((1, d, tn), lambda l, j: (l, 0, j)),
            pl.BlockSpec((1, 1, tn), lambda l, j: (l, 0, j)),
        ],
        out_specs=pl.BlockSpec((1, bsz, tn), lambda l, j: (l, 0, j)),
        out_shape=jax.ShapeDtypeStruct((depth, bsz, n), F32),
        compiler_params=_params(("arbitrary", "arbitrary"), 40),
        name="adaln_mod",
    )(c, mod_w, mod_b.reshape(depth, 1, n))


def _normmod(x, gain, scale, shift):
    ms = jnp.mean(x * x, axis=-1, keepdims=True)
    y = x * lax.rsqrt(ms + NORM_EPS) * gain
    return y * (1.0 + scale) + shift


def _normmod_kernel(x_ref, gain_ref, sc_ref, sh_ref, o_ref):
    o_ref[...] = _normmod(x_ref[...], gain_ref[...], sc_ref[0], sh_ref[0]).astype(o_ref.dtype)


def _mod_spec(d, rows_per_batch, row, chunk):
    return pl.BlockSpec((1, 1, d), lambda i, *_: (row + i // rows_per_batch, 0, chunk))


def _normmod_call(x, gain, modr, row, sc_chunk, sh_chunk, seq, tm=512):
    t, d = x.shape
    rpb = seq // tm
    return pl.pallas_call(
        _normmod_kernel,
        grid=(t // tm,),
        in_specs=[
            pl.BlockSpec((tm, d), lambda i: (i, 0)),
            pl.BlockSpec((1, d), lambda i: (0, 0)),
            _mod_spec(d, rpb, row, sc_chunk),
            _mod_spec(d, rpb, row, sh_chunk),
        ],
        out_specs=pl.BlockSpec((tm, d), lambda i: (i, 0)),
        out_shape=jax.ShapeDtypeStruct((t, d), BF16),
        compiler_params=_params(("arbitrary",), 32),
        name="normmod",
    )(x, gain.reshape(1, d), modr, modr)


def _convin_kernel(h_ref, wb_ref, wc_ref, wv_ref, cw_ref, o_ref, u_scr, wb_scr, wc_scr, wv_scr):
    seq = h_ref.shape[0]
    for w_ref, w_scr in ((wb_ref, wb_scr), (wc_ref, wc_scr), (wv_ref, wv_scr)):
        _cast_weight_once(w_ref, w_scr)
    h = h_ref[...]
    u = _dot(h, wc_scr[...]) * _dot(h, wv_scr[...])
    u_scr[0:SUBLANES, :] = jnp.zeros((SUBLANES, u.shape[1]), F32)
    u_scr[SUBLANES:, :] = u
    cw = cw_ref[...]
    conv = cw[CONV_WIDTH - 1:CONV_WIDTH, :] * u
    for tap in range(1, CONV_WIDTH):
        conv = conv + (cw[CONV_WIDTH - 1 - tap:CONV_WIDTH - tap, :]
                       * u_scr[SUBLANES - tap:SUBLANES - tap + seq, :])
    o_ref[...] = (_dot(h, wb_scr[...]) * conv).astype(o_ref.dtype)


def _convin_call(h, w_in, conv_w, seq, tn=256):
    t, d = h.shape
    nj = d // tn
    return pl.pallas_call(
        _convin_kernel,
        grid=(nj, t // seq),
        in_specs=[
            pl.BlockSpec((seq, d), lambda j, b: (b, 0)),
            pl.BlockSpec((d, tn), lambda j, b: (0, j)),
            pl.BlockSpec((d, tn), lambda j, b: (0, j + nj)),
            pl.BlockSpec((d, tn), lambda j, b: (0, j + 2 * nj)),
            pl.BlockSpec((CONV_WIDTH, tn), lambda j, b: (0, j)),
        ],
        out_specs=pl.BlockSpec((seq, tn), lambda j, b: (b, j)),
        out_shape=jax.ShapeDtypeStruct((t, d), BF16),
        scratch_shapes=[pltpu.VMEM((seq + SUBLANES, tn), F32)] + [pltpu.VMEM((d, tn), BF16)] * 3,
        compiler_params=_params(("arbitrary", "arbitrary"), 54),
        name="conv_in",
    )(h, w_in, w_in, w_in, conv_w)


def _resmm_norm_kernel(a_ref, w_ref, x_ref, g_ref, gain_ref, sc_ref, sh_ref, o_ref, h_ref):
    xn = x_ref[...] + g_ref[0] * _dot(a_ref[...], w_ref[...])
    o_ref[...] = xn
    h_ref[...] = _normmod(xn, gain_ref[...], sc_ref[0], sh_ref[0]).astype(h_ref.dtype)


def _resmm_route_kernel(a_ref, w_ref, x_ref, g_ref, gain_ref, sc_ref, sh_ref, wr_ref, br_ref,
                        o_ref, h_ref, idx_ref, wt_ref, *, n_experts):
    xn = x_ref[...] + g_ref[0] * _dot(a_ref[...], w_ref[...])
    o_ref[...] = xn
    h = _normmod(xn, gain_ref[...], sc_ref[0], sh_ref[0])
    h_ref[...] = h
    idx_ref[...], wt_ref[...] = _route(h, wr_ref[...], br_ref[...], n_experts)


def _resmm_call(a, w, x, modr, row, g_chunk, seq, next_norm, router=None, tm=512):
    t, k = a.shape
    d = w.shape[1]
    rpb = seq // tm
    gain, nrow, sc_chunk, sh_chunk = next_norm
    in_specs = [
        pl.BlockSpec((tm, k), lambda i: (i, 0)),
        pl.BlockSpec((k, d), lambda i: (0, 0)),
        pl.BlockSpec((tm, d), lambda i: (i, 0)),
        _mod_spec(d, rpb, row, g_chunk),
        pl.BlockSpec((1, d), lambda i: (0, 0)),
        _mod_spec(d, rpb, nrow, sc_chunk),
        _mod_spec(d, rpb, nrow, sh_chunk),
    ]
    args = (a, w, x, modr, gain.reshape(1, d), modr, modr)
    row_spec = pl.BlockSpec((tm, d), lambda i: (i, 0))
    lane_spec = pl.BlockSpec((tm, LANES), lambda i: (i, 0))
    cp = _params(("arbitrary",), 56)
    if router is None:
        return pl.pallas_call(
            _resmm_norm_kernel, grid=(t // tm,), in_specs=in_specs,
            out_specs=[row_spec, row_spec],
            out_shape=[jax.ShapeDtypeStruct((t, d), F32), jax.ShapeDtypeStruct((t, d), BF16)],
            compiler_params=cp, name="res_matmul_norm")(*args)
    w_router, b_router = router
    n_experts = w_router.shape[1]
    wr = jnp.zeros((d, LANES), F32).at[:, :n_experts].set(w_router)
    br = jnp.zeros((1, LANES), F32).at[0, :n_experts].set(b_router)
    in_specs += [pl.BlockSpec((d, LANES), lambda i: (0, 0)),
                 pl.BlockSpec((1, LANES), lambda i: (0, 0))]
    return pl.pallas_call(
        functools.partial(_resmm_route_kernel, n_experts=n_experts), grid=(t // tm,),
        in_specs=in_specs, out_specs=[row_spec, row_spec, lane_spec, lane_spec],
        out_shape=[jax.ShapeDtypeStruct((t, d), F32), jax.ShapeDtypeStruct((t, d), F32),
                   jax.ShapeDtypeStruct((t, LANES), jnp.int32),
                   jax.ShapeDtypeStruct((t, LANES), F32)],
        compiler_params=cp, name="res_matmul_route")(*args, wr, br)


def _ffn_kernel(h_ref, wg_ref, wu_ref, wd_ref, x_ref, g_ref, gain_ref, sc_ref, sh_ref,
                o_ref, hn_ref):
    j = pl.program_id(1)
    h = h_ref[...]
    act = _silu(_dot(h, wg_ref[...])) * _dot(h, wu_ref[...])
    part = _dot(act.astype(BF16), wd_ref[...])

    @pl.when(j == 0)
    def _():
        o_ref[...] = part

    @pl.when(j > 0)
    def _():
        o_ref[...] += part

    @pl.when(j == pl.num_programs(1) - 1)
    def _():
        xn = x_ref[...] + g_ref[0] * o_ref[...]
        o_ref[...] = xn
        hn_ref[...] = _normmod(xn, gain_ref[...], sc_ref[0], sh_ref[0]).astype(hn_ref.dtype)


def _ffn_call(h, wg, wu, wd, x, modr, row, g_chunk, seq, next_norm, tm=512, tf=512):
    t, d = h.shape
    f = wg.shape[1]
    rpb = seq // tm
    gain, nrow, sc_chunk, sh_chunk = next_norm
    row_spec = pl.BlockSpec((tm, d), lambda i, j: (i, 0))
    return pl.pallas_call(
        _ffn_kernel,
        grid=(t // tm, f // tf),
        in_specs=[
            row_spec,
            pl.BlockSpec((d, tf), lambda i, j: (0, j)),
            pl.BlockSpec((d, tf), lambda i, j: (0, j)),
            pl.BlockSpec((tf, d), lambda i, j: (j, 0)),
            row_spec,
            _mod_spec(d, rpb, row, g_chunk),
            pl.BlockSpec((1, d), lambda i, j: (0, 0)),
            _mod_spec(d, rpb, nrow, sc_chunk),
            _mod_spec(d, rpb, nrow, sh_chunk),
        ],
        out_specs=[row_spec, row_spec],
        out_shape=[jax.ShapeDtypeStruct((t, d), F32), jax.ShapeDtypeStruct((t, d), BF16)],
        compiler_params=_params(("arbitrary", "arbitrary"), 52),
        name="dense_ffn",
    )(h, wg, wu, wd, x, modr, gain.reshape(1, d), modr, modr)


def _qk_kernel(h_ref, w_ref, gain_ref, cos_ref, sin_ref, ones_ref, perm_ref, o_ref, w_scr,
               *, head_dim):
    _cast_weight_once(w_ref, w_scr)
    acc = _dot(h_ref[...], w_scr[...])
    pair = ones_ref.shape[0]
    gain = gain_ref[...]
    cos = cos_ref[...]
    sin = sin_ref[...]
    ones = ones_ref[...]
    perm = perm_ref[...]
    for p0 in range(0, acc.shape[1], pair):
        xp = acc[:, p0:p0 + pair]
        ss = _dot((xp * xp).astype(BF16), ones)
        inv = lax.rsqrt(ss * (1.0 / head_dim) + NORM_EPS)
        z = xp * gain
        rot = _dot(z.astype(BF16), perm)
        out = (z * cos + rot * sin) * inv
        o_ref[:, p0:p0 + pair] = out.astype(o_ref.dtype)


def _v_kernel(h_ref, w_ref, o_ref, w_scr):
    _cast_weight_once(w_ref, w_scr)
    o_ref[...] = _dot(h_ref[...], w_scr[...]).astype(o_ref.dtype)


def _head_pair_matrices(head_dim):
    pair = 2 * head_dim
    src = jnp.arange(pair)[:, None]
    dst = jnp.arange(pair)[None, :]
    same_head = (src // head_dim) == (dst // head_dim)
    ones = same_head.astype(BF16)
    perm = (same_head & ((src % head_dim) == ((dst + head_dim // 2) % head_dim))).astype(BF16)
    return ones, perm


def _qkv_call(h, w_qkv, which, seq, head_dim, gain=None, cos=None, sin=None, tm=1024, tn=512):
    t, d = h.shape
    nj = d // tn
    off = which * nj
    rpb = seq // tm
    h_spec = pl.BlockSpec((tm, d), lambda j, i: (i, 0))
    w_spec = pl.BlockSpec((d, tn), lambda j, i: (0, j + off))
    out_spec = pl.BlockSpec((tm, tn), lambda j, i: (i, j))
    out_shape = jax.ShapeDtypeStruct((t, d), BF16)
    cp = _params(("arbitrary", "arbitrary"), 40)
    w_scratch = [pltpu.VMEM((d, tn), BF16)]
    if gain is None:
        return pl.pallas_call(_v_kernel, grid=(nj, t // tm), in_specs=[h_spec, w_spec],
                              out_specs=out_spec, out_shape=out_shape,
                              scratch_shapes=w_scratch, compiler_params=cp,
                              name="v_proj")(h, w_qkv)
    pair = 2 * head_dim
    ones, perm = _head_pair_matrices(head_dim)
    tab_spec = pl.BlockSpec((tm, pair), lambda j, i: (i % rpb, 0))
    const_spec = pl.BlockSpec((pair, pair), lambda j, i: (0, 0))
    return pl.pallas_call(
        functools.partial(_qk_kernel, head_dim=head_dim),
        grid=(nj, t // tm),
        in_specs=[h_spec, w_spec, pl.BlockSpec((1, pair), lambda j, i: (0, 0)),
                  tab_spec, tab_spec, const_spec, const_spec],
        out_specs=out_spec, out_shape=out_shape, scratch_shapes=w_scratch, compiler_params=cp,
        name="qk_proj",
    )(h, w_qkv, jnp.tile(gain.reshape(1, head_dim), (1, 2)), cos, sin, ones, perm)


def _attn_kernel(q_ref, k_ref, v_ref, o_ref, vt_scr, s_scr):
    seq, hd = q_ref.shape
    blk = MOBA_BLOCK
    nb = seq // blk

    rows = []
    for n in range(nb):
        vt_scr[n, 0:hd, :] = v_ref[n * blk:(n + 1) * blk, :].astype(F32).T.astype(BF16)
        vt_scr[n, hd:, :] = jnp.ones((vt_scr.shape[1] - hd, blk), BF16)
        kb = k_ref[n * blk:(n + 1) * blk, :].astype(F32)
        rows.append(jnp.sum(kb, axis=0, keepdims=True) * (1.0 / blk))
    km = jnp.concatenate(rows, axis=0)
    km_a = km.astype(BF16)
    rem = km - km_a.astype(F32)
    km_b = rem.astype(BF16)
    km_c = (rem - km_b.astype(F32)).astype(BF16)
    km_parts = jnp.concatenate([km_a, km_b, km_c, jnp.zeros_like(km_a)], axis=0)

    causal = (lax.broadcasted_iota(jnp.int32, (blk, blk), 0)
              <= lax.broadcasted_iota(jnp.int32, (blk, blk), 1))

    def masked_scores(qi):
        qb = q_ref[qi * blk:(qi + 1) * blk, :]
        sel = None
        if qi > MOBA_TOPK:
            g3 = lax.dot_general(km_parts, qb, _NT_DIMS, preferred_element_type=F32)
            gate = g3[0:nb] + g3[nb:2 * nb] + g3[2 * nb:3 * nb]
            g_row = [gate[n:n + 1, :] for n in range(qi)]
            rank = [jnp.zeros((1, blk), F32) for _ in range(qi)]
            for a in range(qi):
                for b in range(a + 1, qi):
                    a_first = (g_row[a] >= g_row[b]).astype(F32)
                    rank[b] = rank[b] + a_first
                    rank[a] = rank[a] + (1.0 - a_first)
            sel = [rank[n] < MOBA_TOPK for n in range(qi)]

        m_run = None
        for n in range(qi + 1):
            s = lax.dot_general(k_ref[n * blk:(n + 1) * blk, :], qb, _NT_DIMS,
                                preferred_element_type=F32)
            if n == qi:
                s = jnp.where(causal, s, NEG_INF)
            elif sel is not None:
                s = jnp.where(sel[n], s, NEG_INF)
            s_scr[qi % 2, n] = s
            cm = jnp.max(s, axis=0, keepdims=True)
            m_run = cm if m_run is None else jnp.maximum(m_run, cm)
        return m_run

    def weighted_values(qi, m_run):
        acc = None
        for n in range(qi + 1):
            p = jnp.exp2(s_scr[qi % 2, n] - m_run)
            pv = _dot(vt_scr[n], p.astype(BF16))
            acc = pv if acc is None else acc + pv
        out_t = acc[0:hd] * (1.0 / acc[hd:hd + 1])
        o_ref[qi * blk:(qi + 1) * blk, :] = out_t.T.astype(o_ref.dtype)

    m_next = masked_scores(0)
    for qi in range(nb):
        m_cur = m_next
        if qi + 1 < nb:
            m_next = masked_scores(qi + 1)
        weighted_values(qi, m_cur)


def _attn_call(q, k, v, seq, head_dim):
    t, d = q.shape
    nh = d // head_dim
    nb = seq // MOBA_BLOCK
    spec = pl.BlockSpec((seq, head_dim), lambda b, h: (b, h))
    return pl.pallas_call(
        _attn_kernel,
        grid=(t // seq, nh),
        in_specs=[spec, spec, spec],
        out_specs=spec,
        out_shape=jax.ShapeDtypeStruct((t, d), BF16),
        scratch_shapes=[pltpu.VMEM((nb, head_dim + BF16_SUBLANES, MOBA_BLOCK), BF16),
                        pltpu.VMEM((2, nb, MOBA_BLOCK, MOBA_BLOCK), F32)],
        compiler_params=_params(("arbitrary", "arbitrary"), 32),
        name="moba_attention",
    )(q, k, v)


def _route(h, w, bias, n_experts):
    h_a = h.astype(BF16)
    h_b = (h - h_a.astype(F32)).astype(BF16)
    w_a = w.astype(BF16)
    w_b = (w - w_a.astype(F32)).astype(BF16)
    logits = _dot(h_a, w_a) + _dot(h_b, w_a) + _dot(h_a, w_b) + bias
    lane = lax.broadcasted_iota(jnp.int32, logits.shape, 1)
    logits = jnp.where(lane < n_experts, logits, -jnp.inf)
    m1 = jnp.max(logits, axis=-1, keepdims=True)
    i1 = jnp.min(jnp.where(logits == m1, lane, LANES), axis=-1, keepdims=True)
    rest = jnp.where(lane == i1, -jnp.inf, logits)
    m2 = jnp.max(rest, axis=-1, keepdims=True)
    i2 = jnp.min(jnp.where(rest == m2, lane, LANES), axis=-1, keepdims=True)
    e2 = jnp.exp(m2 - m1)
    w1 = 1.0 / (1.0 + e2)
    w2 = e2 * w1
    return (jnp.where(lane == 0, i1, jnp.where(lane == 1, i2, 0)),
            jnp.where(lane == 0, w1, jnp.where(lane == 1, w2, 0.0)))


def _dispatch_kernel(zb_ref, nz_ref, dest_ref, h_ref, xs_ref, zbuf, sem, zsem):
    tm = h_ref.shape[0]
    zrows = zbuf.shape[0]

    @pl.when(pl.program_id(0) == 0)
    def _():
        zbuf[...] = jnp.zeros(zbuf.shape, zbuf.dtype)

        def zero_copy(k):
            row0 = pl.multiple_of(zb_ref[k] * zrows, zrows)
            return pltpu.make_async_copy(zbuf, xs_ref.at[pl.ds(row0, zrows), :], zsem)

        def start(k, c):
            zero_copy(k).start()
            return c

        def wait(k, c):
            zero_copy(k).wait()
            return c

        lax.fori_loop(0, nz_ref[0], start, 0)
        lax.fori_loop(0, nz_ref[0], wait, 0)

    def issue(r, c):
        for k in range(EXPERT_TOPK):
            d = dest_ref[0, 0, EXPERT_TOPK * r + k]
            pltpu.make_async_copy(h_ref.at[pl.ds(r, 1), :], xs_ref.at[pl.ds(d, 1), :],
                                  sem).start(priority=k % 2)
        return c

    lax.fori_loop(0, tm, issue, 0, unroll=ROW_DMA_UNROLL)
    for k in range(EXPERT_TOPK):
        pltpu.make_async_copy(h_ref, xs_ref.at[pl.ds(0, tm), :], sem).wait()


def _dispatch_call(h, dest, tables, n_rows, block_rows, tm=256):
    t, d = h.shape
    nt = t // tm
    grid_spec = pltpu.PrefetchScalarGridSpec(
        num_scalar_prefetch=2,
        grid=(nt,),
        in_specs=[
            pl.BlockSpec((1, 1, EXPERT_TOPK * tm), lambda i, zb, nz: (i, 0, 0),
                         memory_space=pltpu.SMEM),
            pl.BlockSpec((tm, d), lambda i, zb, nz: (i, 0)),
        ],
        out_specs=pl.BlockSpec(memory_space=pl.ANY),
        scratch_shapes=[pltpu.VMEM((block_rows, d), h.dtype), pltpu.SemaphoreType.DMA,
                        pltpu.SemaphoreType.DMA],
    )
    return pl.pallas_call(
        _dispatch_kernel,
        grid_spec=grid_spec,
        out_shape=jax.ShapeDtypeStruct((n_rows, d), h.dtype),
        compiler_params=_params(("arbitrary",), 32),
        name="moe_dispatch",
    )(tables["zero_blocks"], tables["n_zero"], dest.reshape(nt, 1, EXPERT_TOPK * tm), h)


BLOCK_COMPUTE, BLOCK_TAIL, BLOCK_UNUSED = 0, 1, 2


def _expert_kernel(kind_ref, tile_ref, blk_ref, xmap_ref, we_ref, tmap_ref,
                   xs_ref, wg_ref, wu_ref, wd_ref, ytail_ref, ys_ref, xb_scr):
    del blk_ref, xmap_ref, we_ref, tmap_ref
    s = pl.program_id(0)
    kind = kind_ref[s]

    @pl.when(kind == BLOCK_UNUSED)
    def _():
        ys_ref[...] = jnp.zeros(ys_ref.shape, F32)

    @pl.when(kind == BLOCK_TAIL)
    def _():
        ys_ref[...] = ytail_ref[...]

    @pl.when(kind == BLOCK_COMPUTE)
    def _():
        @pl.when(tile_ref[s] == 0)
        def _():
            xb_scr[...] = xs_ref[...].astype(BF16)
            ys_ref[...] = jnp.zeros(ys_ref.shape, F32)

        x = xb_scr[...]
        act = _silu(_dot(x, wg_ref[0])) * _dot(x, wu_ref[0])
        ys_ref[...] += _dot(act.astype(BF16), wd_ref[0])


def _expert_call(xs, tables, y_tail, wg, wu, wd, tm, tf):
    n_rows, d = xs.shape
    grid_spec = pltpu.PrefetchScalarGridSpec(
        num_scalar_prefetch=6,
        grid=(tables["n_items"],),
        in_specs=[
            pl.BlockSpec((tm, d), lambda s, kind, tile, blk, xm, we, tmap: (xm[s], 0)),
            pl.BlockSpec((1, d, tf), lambda s, kind, tile, blk, xm, we, tmap: (we[s], 0, tile[s])),
            pl.BlockSpec((1, d, tf), lambda s, kind, tile, blk, xm, we, tmap: (we[s], 0, tile[s])),
            pl.BlockSpec((1, tf, d), lambda s, kind, tile, blk, xm, we, tmap: (we[s], tile[s], 0)),
            pl.BlockSpec((tm, d), lambda s, kind, tile, blk, xm, we, tmap: (tmap[s], 0)),
        ],
        out_specs=pl.BlockSpec((tm, d), lambda s, kind, tile, blk, xm, we, tmap: (blk[s], 0)),
        scratch_shapes=[pltpu.VMEM((tm, d), BF16)],
    )
    return pl.pallas_call(
        _expert_kernel,
        grid_spec=grid_spec,
        out_shape=jax.ShapeDtypeStruct((n_rows, d), F32),
        compiler_params=_params(("arbitrary",), 52),
        name="moe_experts",
    )(tables["item_kind"], tables["item_tile"], tables["item_block"], tables["item_x"],
      tables["item_expert"], tables["item_tail"], xs, wg, wu, wd, y_tail)


TAIL_BLOCKS = 2


def _expert_tail_kernel(tb_ref, nt_ref, *refs):
    del tb_ref
    xs_refs = refs[:TAIL_BLOCKS]
    wg_ref, wu_ref, wd_ref, ys_ref, wgb_ref, wub_ref, wdb_ref, xb_scr = refs[TAIL_BLOCKS:]
    tm = xs_refs[0].shape[0]
    e = pl.program_id(0)
    j = pl.program_id(1)

    @pl.when(j == 0)
    def _():
        for k, xs_ref in enumerate(xs_refs):
            xb_scr[k * tm:(k + 1) * tm, :] = xs_ref[...].astype(BF16)
        ys_ref[...] = jnp.zeros(ys_ref.shape, F32)

    x = xb_scr[...]
    wg = wg_ref[0].astype(BF16)
    wu = wu_ref[0].astype(BF16)
    act = _silu(_dot(x, wg)) * _dot(x, wu)
    wd = wd_ref[0].astype(BF16)
    part = _dot(act.astype(BF16), wd)
    for k in range(TAIL_BLOCKS):
        rows = slice(k * tm, (k + 1) * tm)
        ys_ref[rows, :] += jnp.where(nt_ref[e] >= TAIL_BLOCKS - k, part[rows, :], 0.0)
    wgb_ref[0] = wg
    wub_ref[0] = wu
    wdb_ref[0] = wd


def _expert_tail_call(xs, tables, wg, wu, wd, tm, tf=256):
    d = xs.shape[1]
    n_experts, _, f = wg.shape

    def xs_spec(k):
        return pl.BlockSpec((tm, d), lambda e, j, tb, nt: (tb[TAIL_BLOCKS * e + k], 0),
                            pipeline_mode=pl.Buffered(1))

    w_in = lambda e, j, tb, nt: (e, 0, j)
    w_out = lambda e, j, tb, nt: (e, j, 0)
    grid_spec = pltpu.PrefetchScalarGridSpec(
        num_scalar_prefetch=2,
        grid=(n_experts, f // tf),
        in_specs=[xs_spec(k) for k in range(TAIL_BLOCKS)] + [
            pl.BlockSpec((1, d, tf), w_in),
            pl.BlockSpec((1, d, tf), w_in),
            pl.BlockSpec((1, tf, d), w_out),
        ],
        out_specs=[
            pl.BlockSpec((TAIL_BLOCKS * tm, d), lambda e, j, tb, nt: (e, 0)),
            pl.BlockSpec((1, d, tf), w_in),
            pl.BlockSpec((1, d, tf), w_in),
            pl.BlockSpec((1, tf, d), w_out),
        ],
        scratch_shapes=[pltpu.VMEM((TAIL_BLOCKS * tm, d), BF16)],
    )
    return pl.pallas_call(
        _expert_tail_kernel,
        grid_spec=grid_spec,
        out_shape=[
            jax.ShapeDtypeStruct((n_experts * TAIL_BLOCKS * tm, d), F32),
            jax.ShapeDtypeStruct(wg.shape, BF16),
            jax.ShapeDtypeStruct(wu.shape, BF16),
            jax.ShapeDtypeStruct(wd.shape, BF16),
        ],
        compiler_params=_params(("arbitrary", "arbitrary"), 58),
        name="moe_experts_tail",
    )(tables["tail_block"], tables["n_tail"], *([xs] * TAIL_BLOCKS), wg, wu, wd)


def _combine_kernel(dest_ref, dest_next_ref, ys_ref, x_ref, wt_ref, g_ref, o_ref, ybuf, sems):
    tm = x_ref.shape[0]
    i = pl.program_id(0)
    slot = i % 2

    def start_gathers(d_ref, dst_slot):
        def issue(r, c):
            for k in range(EXPERT_TOPK):
                d = d_ref[0, 0, EXPERT_TOPK * r + k]
                pltpu.make_async_copy(ys_ref.at[pl.ds(d, 1), :],
                                      ybuf.at[dst_slot, k, pl.ds(r, 1), :],
                                      sems.at[dst_slot]).start(priority=k % 2)
            return c

        lax.fori_loop(0, tm, issue, 0, unroll=ROW_DMA_UNROLL)

    @pl.when(i == 0)
    def _():
        start_gathers(dest_ref, 0)

    @pl.when(i + 1 < pl.num_programs(0))
    def _():
        start_gathers(dest_next_ref, 1 - slot)

    for k in range(EXPERT_TOPK):
        pltpu.make_async_copy(ys_ref.at[pl.ds(0, tm), :], ybuf.at[slot, k], sems.at[slot]).wait()
    wt = wt_ref[...]
    y = wt[:, 0:1] * ybuf[slot, 0]
    for k in range(1, EXPERT_TOPK):
        y = y + wt[:, k:k + 1] * ybuf[slot, k]
    o_ref[...] = x_ref[...] + g_ref[0] * y


def _combine_call(ys, dest, x, wt, modr, row, g_chunk, seq, tm=256):
    t, d = x.shape
    nt = t // tm
    rpb = seq // tm
    dest3 = dest.reshape(nt, 1, EXPERT_TOPK * tm)
    return pl.pallas_call(
        _combine_kernel,
        grid=(nt,),
        in_specs=[
            pl.BlockSpec((1, 1, EXPERT_TOPK * tm), lambda i: (i, 0, 0), memory_space=pltpu.SMEM),
            pl.BlockSpec((1, 1, EXPERT_TOPK * tm), lambda i: (jnp.minimum(i + 1, nt - 1), 0, 0),
                         memory_space=pltpu.SMEM),
            pl.BlockSpec(memory_space=pl.ANY),
            pl.BlockSpec((tm, d), lambda i: (i, 0)),
            pl.BlockSpec((tm, LANES), lambda i: (i, 0)),
            _mod_spec(d, rpb, row, g_chunk),
        ],
        out_specs=pl.BlockSpec((tm, d), lambda i: (i, 0)),
        out_shape=jax.ShapeDtypeStruct((t, d), F32),
        scratch_shapes=[pltpu.VMEM((2, EXPERT_TOPK, tm, d), F32), pltpu.SemaphoreType.DMA((2,))],
        compiler_params=_params(("arbitrary",), 40),
        name="moe_combine",
    )(dest3, dest3, ys, x, wt, modr)


def _routing_tables(top_idx, n_experts, tm, n_blocks, n_ff_tiles):
    flat_e = top_idx.reshape(-1)
    onehot = (flat_e[:, None] == jnp.arange(n_experts, dtype=jnp.int32)[None, :]).astype(jnp.int32)
    csum = jnp.cumsum(onehot, axis=0)
    counts = csum[-1]
    pos = jnp.sum((csum - 1) * onehot, axis=1)
    nblk = (counts + tm - 1) // tm
    blk_end = jnp.cumsum(nblk)
    blk_start = blk_end - nblk
    dest = (blk_start[flat_e] * tm + pos).astype(jnp.int32)
    n_used = blk_end[-1]

    blocks = jnp.arange(n_blocks, dtype=jnp.int32)
    x_map = jnp.minimum(blocks, n_used - 1)
    blk_e = jnp.minimum(jnp.sum(blk_end[None, :] <= x_map[:, None], axis=1), n_experts - 1)
    from_end = blk_end[blk_e] - blocks
    is_tail = (from_end <= TAIL_BLOCKS) & (blocks < n_used)
    kind = jnp.where(blocks >= n_used, BLOCK_UNUSED,
                     jnp.where(is_tail, BLOCK_TAIL, BLOCK_COMPUTE))
    last_compute = lax.cummax(jnp.where(kind == BLOCK_COMPUTE, blocks, -1))
    w_expert = blk_e[jnp.maximum(last_compute, 0)]
    tail_map = lax.cummax(jnp.where(is_tail, TAIL_BLOCKS * (blk_e + 1) - from_end, 0))
    slot_from_end = TAIL_BLOCKS - jnp.arange(TAIL_BLOCKS, dtype=jnp.int32)
    tail_block = jnp.clip(blk_end[:, None] - slot_from_end[None, :], 0, n_blocks - 1)
    cand = jnp.concatenate([blk_end - 1, blocks])
    cand_ok = jnp.concatenate([nblk > 0, blocks >= n_used])
    zero_blocks = jnp.maximum(cand[jnp.argsort(~cand_ok, stable=True)], 0)
    items_per_block = jnp.where(kind == BLOCK_COMPUTE, n_ff_tiles, 1)
    item_end = jnp.cumsum(items_per_block)
    items = jnp.arange(n_blocks * n_ff_tiles, dtype=jnp.int32)
    item_block = jnp.minimum(jnp.sum(item_end[None, :] <= items[:, None], axis=1), n_blocks - 1)
    first_item = (item_end - items_per_block)[item_block]
    item_tile = jnp.where(kind[item_block] == BLOCK_COMPUTE,
                          jnp.clip(items - first_item, 0, n_ff_tiles - 1), n_ff_tiles - 1)
    i32 = lambda a: a.astype(jnp.int32)
    tables = dict(item_kind=i32(kind[item_block]), item_tile=i32(item_tile),
                  item_block=i32(item_block), item_x=i32(x_map[item_block]),
                  item_expert=i32(w_expert[item_block]), item_tail=i32(tail_map[item_block]),
                  n_items=i32(item_end[-1]),
                  tail_block=i32(tail_block.reshape(-1)),
                  n_tail=i32(jnp.minimum(nblk, TAIL_BLOCKS)),
                  zero_blocks=i32(zero_blocks), n_zero=i32(jnp.sum(cand_ok)).reshape(1))
    return dest, tables


def _rope_tables(seq, head_dim, scale=1.0):
    half = head_dim // 2
    inv_freq = jnp.exp(-math.log(ROPE_THETA) * jnp.arange(half, dtype=F32) / half)
    ang = jnp.arange(seq).astype(F32)[:, None] * inv_freq[None, :]
    cos, sin = jnp.cos(ang) * scale, jnp.sin(ang) * scale
    return (jnp.concatenate([cos, cos, cos, cos], axis=-1),
            jnp.concatenate([-sin, sin, -sin, sin], axis=-1))


def kernel(x, c, mod_w, mod_b, norm_mix, norm_ffn, conv_in, conv_w, conv_out, ffn_gate, ffn_up,
           ffn_down, qkv_w, q_norm, k_norm, attn_out, router_w, router_b, exp_gate, exp_up,
           exp_down):
    bsz, seq, d = x.shape
    depth = mod_w.shape[0]
    assert depth == 2, "layer 0 = short-conv + dense FFN, layer 1 = MoBA + MoE"
    t = bsz * seq
    head_dim = d // N_HEADS
    n_experts = router_w.shape[-1]

    mod = _mod_call(c, mod_w, mod_b)
    modr = mod.reshape(depth * bsz, 1, 6 * d)
    xt = x.reshape(t, d)

    row = bsz
    h = _normmod_call(xt, norm_mix[0], modr, 0, 1, 0, seq)
    bu = _convin_call(h, conv_in[0], conv_w[0], seq)
    xt, h = _resmm_call(bu, conv_out[0].astype(BF16), xt, modr, 0, 2, seq,
                        next_norm=(norm_ffn[0], 0, 4, 3))
    xt, h = _ffn_call(h, ffn_gate[0].astype(BF16), ffn_up[0].astype(BF16),
                      ffn_down[0].astype(BF16), xt, modr, 0, 5, seq,
                      next_norm=(norm_mix[1], row, 1, 0))

    w_qkv = qkv_w[0]
    cos_q, sin_q = _rope_tables(seq, head_dim, head_dim ** -0.5 * math.log2(math.e))
    cos_k, sin_k = _rope_tables(seq, head_dim)
    q = _qkv_call(h, w_qkv, 0, seq, head_dim, q_norm[0], cos_q, sin_q)
    k = _qkv_call(h, w_qkv, 1, seq, head_dim, k_norm[0], cos_k, sin_k)
    v = _qkv_call(h, w_qkv, 2, seq, head_dim)
    o = _attn_call(q, k, v, seq, head_dim)
    xt, hf, top_idx, top_w = _resmm_call(o, attn_out[0].astype(BF16), xt, modr, row, 2, seq,
                                         next_norm=(norm_ffn[1], row, 4, 3),
                                         router=(router_w[0], router_b[0]))
    n_blocks = (t * EXPERT_TOPK) // EXPERT_ROWS + n_experts
    dest, tables = _routing_tables(top_idx[:, :EXPERT_TOPK], n_experts, EXPERT_ROWS, n_blocks,
                                   exp_gate.shape[-1] // EXPERT_FF_TILE)
    xs = _dispatch_call(hf, dest, tables, n_blocks * EXPERT_ROWS, EXPERT_ROWS)
    y_tail, wg, wu, wd = _expert_tail_call(xs, tables, exp_gate[0], exp_up[0], exp_down[0],
                                           EXPERT_ROWS)
    ys = _expert_call(xs, tables, y_tail, wg, wu, wd, EXPERT_ROWS, EXPERT_FF_TILE)
    xt = _combine_call(ys, dest, xt, top_w, modr, row, 5, seq)
    return xt.reshape(bsz, seq, d)
```

```python
import functools
import math

import jax
import jax.numpy as jnp
from jax import lax
from jax.experimental import pallas as pl
from jax.experimental.pallas import tpu as pltpu

N_HEADS = 16
CONV_WIDTH = 3
MOBA_BLOCK = 256
MOBA_TOPK = 3
ROPE_THETA = 10000.0
EXPERT_TOPK = 2
EXPERT_ROWS = 512
ROW_DMA_UNROLL = 8
NORM_EPS = 1e-6
NEG_INF = -1e30

LANES = 128
SUBLANES = 8
BF16_SUBLANES = 16
MIB = 1024 * 1024

F32 = jnp.float32
BF16 = jnp.bfloat16

_NT_DIMS = (((1,), (1,)), ((), ()))


def _params(semantics, vmem_mib):
    return pltpu.CompilerParams(dimension_semantics=semantics,
                                vmem_limit_bytes=vmem_mib * MIB)


def _dot(a, b):
    return jnp.dot(a, b, preferred_element_type=F32)


def _silu(a):
    return a * (1.0 / (1.0 + jnp.exp(-a)))


def _cast_weight_once(w_ref, w_scr):
    @pl.when(pl.program_id(1) == 0)
    def _():
        w_scr[...] = w_ref[...].astype(BF16)


def _mod_kernel(c_ref, w_ref, b_ref, o_ref):
    ca = _silu(c_ref[...])
    o_ref[0] = _dot(ca.astype(BF16), w_ref[0].astype(BF16)) + b_ref[0]


def _mod_call(c, mod_w, mod_b, tn=1024):
    depth, d, n = mod_w.shape
    bsz = c.shape[0]
    return pl.pallas_call(
        _mod_kernel,
        grid=(depth, n // tn),
        in_specs=[
            pl.BlockSpec((bsz, d), lambda l, j: (0, 0)),
            pl.BlockSpec((1, d, tn), lambda l, j: (l, 0, j)),
            pl.BlockSpec((1, 1, tn), lambda l, j: (l, 0, j)),
        ],
        out_specs=pl.BlockSpec((1, bsz, tn), lambda l, j: (l, 0, j)),
        out_shape=jax.ShapeDtypeStruct((depth, bsz, n), F32),
        compiler_params=_params(("arbitrary", "arbitrary"), 40),
        name="adaln_mod",
    )(c, mod_w, mod_b.reshape(depth, 1, n))


def _normmod(x, gain, scale, shift):
    ms = jnp.mean(x * x, axis=-1, keepdims=True)
    y = x * lax.rsqrt(ms + NORM_EPS) * gain
    return y * (1.0 + scale) + shift


def _normmod_kernel(x_ref, gain_ref, sc_ref, sh_ref, o_ref):
    o_ref[...] = _normmod(x_ref[...], gain_ref[...], sc_ref[0], sh_ref[0]).astype(o_ref.dtype)


def _mod_spec(d, rows_per_batch, row, chunk):
    return pl.BlockSpec((1, 1, d), lambda i, *_: (row + i // rows_per_batch, 0, chunk))


def _normmod_call(x, gain, modr, row, sc_chunk, sh_chunk, seq, tm=512):
    t, d = x.shape
    rpb = seq // tm
    return pl.pallas_call(
        _normmod_kernel,
        grid=(t // tm,),
        in_specs=[
            pl.BlockSpec((tm, d), lambda i: (i, 0)),
            pl.BlockSpec((1, d), lambda i: (0, 0)),
            _mod_spec(d, rpb, row, sc_chunk),
            _mod_spec(d, rpb, row, sh_chunk),
        ],
        out_specs=pl.BlockSpec((tm, d), lambda i: (i, 0)),
        out_shape=jax.ShapeDtypeStruct((t, d), BF16),
        compiler_params=_params(("arbitrary",), 32),
        name="normmod",
    )(x, gain.reshape(1, d), modr, modr)


def _convin_kernel(h_ref, wb_ref, wc_ref, wv_ref, cw_ref, o_ref, u_scr, wb_scr, wc_scr, wv_scr):
    seq = h_ref.shape[0]
    for w_ref, w_scr in ((wb_ref, wb_scr), (wc_ref, wc_scr), (wv_ref, wv_scr)):
        _cast_weight_once(w_ref, w_scr)
    h = h_ref[...]
    u = _dot(h, wc_scr[...]) * _dot(h, wv_scr[...])
    u_scr[0:SUBLANES, :] = jnp.zeros((SUBLANES, u.shape[1]), F32)
    u_scr[SUBLANES:, :] = u
    cw = cw_ref[...]
    conv = cw[CONV_WIDTH - 1:CONV_WIDTH, :] * u
    for tap in range(1, CONV_WIDTH):
        conv = conv + (cw[CONV_WIDTH - 1 - tap:CONV_WIDTH - tap, :]
                       * u_scr[SUBLANES - tap:SUBLANES - tap + seq, :])
    o_ref[...] = (_dot(h, wb_scr[...]) * conv).astype(o_ref.dtype)


def _convin_call(h, w_in, conv_w, seq, tn=256):
    t, d = h.shape
    nj = d // tn
    return pl.pallas_call(
        _convin_kernel,
        grid=(nj, t // seq),
        in_specs=[
            pl.BlockSpec((seq, d), lambda j, b: (b, 0)),
            pl.BlockSpec((d, tn), lambda j, b: (0, j)),
            pl.BlockSpec((d, tn), lambda j, b: (0, j + nj)),
            pl.BlockSpec((d, tn), lambda j, b: (0, j + 2 * nj)),
            pl.BlockSpec((CONV_WIDTH, tn), lambda j, b: (0, j)),
        ],
        out_specs=pl.BlockSpec((seq, tn), lambda j, b: (b, j)),
        out_shape=jax.ShapeDtypeStruct((t, d), BF16),
        scratch_shapes=[pltpu.VMEM((seq + SUBLANES, tn), F32)] + [pltpu.VMEM((d, tn), BF16)] * 3,
        compiler_params=_params(("arbitrary", "arbitrary"), 54),
        name="conv_in",
    )(h, w_in, w_in, w_in, conv_w)


def _resmm_norm_kernel(a_ref, w_ref, x_ref, g_ref, gain_ref, sc_ref, sh_ref, o_ref, h_ref):
    xn = x_ref[...] + g_ref[0] * _dot(a_ref[...], w_ref[...])
    o_ref[...] = xn
    h_ref[...] = _normmod(xn, gain_ref[...], sc_ref[0], sh_ref[0]).astype(h_ref.dtype)


def _resmm_route_kernel(a_ref, w_ref, x_ref, g_ref, gain_ref, sc_ref, sh_ref, wr_ref, br_ref,
                        o_ref, h_ref, idx_ref, wt_ref, *, n_experts):
    xn = x_ref[...] + g_ref[0] * _dot(a_ref[...], w_ref[...])
    o_ref[...] = xn
    h = _normmod(xn, gain_ref[...], sc_ref[0], sh_ref[0])
    h_ref[...] = h
    idx_ref[...], wt_ref[...] = _route(h, wr_ref[...], br_ref[...], n_experts)


def _resmm_call(a, w, x, modr, row, g_chunk, seq, next_norm, router=None, tm=512):
    t, k = a.shape
    d = w.shape[1]
    rpb = seq // tm
    gain, nrow, sc_chunk, sh_chunk = next_norm
    in_specs = [
        pl.BlockSpec((tm, k), lambda i: (i, 0)),
        pl.BlockSpec((k, d), lambda i: (0, 0)),
        pl.BlockSpec((tm, d), lambda i: (i, 0)),
        _mod_spec(d, rpb, row, g_chunk),
        pl.BlockSpec((1, d), lambda i: (0, 0)),
        _mod_spec(d, rpb, nrow, sc_chunk),
        _mod_spec(d, rpb, nrow, sh_chunk),
    ]
    args = (a, w, x, modr, gain.reshape(1, d), modr, modr)
    row_spec = pl.BlockSpec((tm, d), lambda i: (i, 0))
    lane_spec = pl.BlockSpec((tm, LANES), lambda i: (i, 0))
    cp = _params(("arbitrary",), 56)
    if router is None:
        return pl.pallas_call(
            _resmm_norm_kernel, grid=(t // tm,), in_specs=in_specs,
            out_specs=[row_spec, row_spec],
            out_shape=[jax.ShapeDtypeStruct((t, d), F32), jax.ShapeDtypeStruct((t, d), BF16)],
            compiler_params=cp, name="res_matmul_norm")(*args)
    w_router, b_router = router
    n_experts = w_router.shape[1]
    wr = jnp.zeros((d, LANES), F32).at[:, :n_experts].set(w_router)
    br = jnp.zeros((1, LANES), F32).at[0, :n_experts].set(b_router)
    in_specs += [pl.BlockSpec((d, LANES), lambda i: (0, 0)),
                 pl.BlockSpec((1, LANES), lambda i: (0, 0))]
    return pl.pallas_call(
        functools.partial(_resmm_route_kernel, n_experts=n_experts), grid=(t // tm,),
        in_specs=in_specs, out_specs=[row_spec, row_spec, lane_spec, lane_spec],
        out_shape=[jax.ShapeDtypeStruct((t, d), F32), jax.ShapeDtypeStruct((t, d), F32),
                   jax.ShapeDtypeStruct((t, LANES), jnp.int32),
                   jax.ShapeDtypeStruct((t, LANES), F32)],
        compiler_params=cp, name="res_matmul_route")(*args, wr, br)


def _ffn_kernel(h_ref, wg_ref, wu_ref, wd_ref, x_ref, g_ref, gain_ref, sc_ref, sh_ref,
                o_ref, hn_ref):
    j = pl.program_id(1)
    h = h_ref[...]
    act = _silu(_dot(h, wg_ref[...])) * _dot(h, wu_ref[...])
    part = _dot(act.astype(BF16), wd_ref[...])

    @pl.when(j == 0)
    def _():
        o_ref[...] = part

    @pl.when(j > 0)
    def _():
        o_ref[...] += part

    @pl.when(j == pl.num_programs(1) - 1)
    def _():
        xn = x_ref[...] + g_ref[0] * o_ref[...]
        o_ref[...] = xn
        hn_ref[...] = _normmod(xn, gain_ref[...], sc_ref[0], sh_ref[0]).astype(hn_ref.dtype)


def _ffn_call(h, wg, wu, wd, x, modr, row, g_chunk, seq, next_norm, tm=512, tf=512):
    t, d = h.shape
    f = wg.shape[1]
    rpb = seq // tm
    gain, nrow, sc_chunk, sh_chunk = next_norm
    row_spec = pl.BlockSpec((tm, d), lambda i, j: (i, 0))
    return pl.pallas_call(
        _ffn_kernel,
        grid=(t // tm, f // tf),
        in_specs=[
            row_spec,
            pl.BlockSpec((d, tf), lambda i, j: (0, j)),
            pl.BlockSpec((d, tf), lambda i, j: (0, j)),
            pl.BlockSpec((tf, d), lambda i, j: (j, 0)),
            row_spec,
            _mod_spec(d, rpb, row, g_chunk),
            pl.BlockSpec((1, d), lambda i, j: (0, 0)),
            _mod_spec(d, rpb, nrow, sc_chunk),
            _mod_spec(d, rpb, nrow, sh_chunk),
        ],
        out_specs=[row_spec, row_spec],
        out_shape=[jax.ShapeDtypeStruct((t, d), F32), jax.ShapeDtypeStruct((t, d), BF16)],
        compiler_params=_params(("arbitrary", "arbitrary"), 52),
        name="dense_ffn",
    )(h, wg, wu, wd, x, modr, gain.reshape(1, d), modr, modr)


def _qk_kernel(h_ref, w_ref, gain_ref, cos_ref, sin_ref, ones_ref, perm_ref, o_ref, w_scr,
               *, head_dim):
    _cast_weight_once(w_ref, w_scr)
    acc = _dot(h_ref[...], w_scr[...])
    pair = ones_ref.shape[0]
    gain = gain_ref[...]
    cos = cos_ref[...]
    sin = sin_ref[...]
    ones = ones_ref[...]
    perm = perm_ref[...]
    for p0 in range(0, acc.shape[1], pair):
        xp = acc[:, p0:p0 + pair]
        ss = _dot((xp * xp).astype(BF16), ones)
        inv = lax.rsqrt(ss * (1.0 / head_dim) + NORM_EPS)
        z = xp * gain
        rot = _dot(z.astype(BF16), perm)
        out = (z * cos + rot * sin) * inv
        o_ref[:, p0:p0 + pair] = out.astype(o_ref.dtype)


def _v_kernel(h_ref, w_ref, o_ref, w_scr):
    _cast_weight_once(w_ref, w_scr)
    o_ref[...] = _dot(h_ref[...], w_scr[...]).astype(o_ref.dtype)


def _head_pair_matrices(head_dim):
    pair = 2 * head_dim
    src = jnp.arange(pair)[:, None]
    dst = jnp.arange(pair)[None, :]
    same_head = (src // head_dim) == (dst // head_dim)
    ones = same_head.astype(BF16)
    perm = (same_head & ((src % head_dim) == ((dst + head_dim // 2) % head_dim))).astype(BF16)
    return ones, perm


def _qkv_call(h, w_qkv, which, seq, head_dim, gain=None, cos=None, sin=None, tm=1024, tn=512):
    t, d = h.shape
    nj = d // tn
    off = which * nj
    rpb = seq // tm
    h_spec = pl.BlockSpec((tm, d), lambda j, i: (i, 0))
    w_spec = pl.BlockSpec((d, tn), lambda j, i: (0, j + off))
    out_spec = pl.BlockSpec((tm, tn), lambda j, i: (i, j))
    out_shape = jax.ShapeDtypeStruct((t, d), BF16)
    cp = _params(("arbitrary", "arbitrary"), 40)
    w_scratch = [pltpu.VMEM((d, tn), BF16)]
    if gain is None:
        return pl.pallas_call(_v_kernel, grid=(nj, t // tm), in_specs=[h_spec, w_spec],
                              out_specs=out_spec, out_shape=out_shape,
                              scratch_shapes=w_scratch, compiler_params=cp,
                              name="v_proj")(h, w_qkv)
    pair = 2 * head_dim
    ones, perm = _head_pair_matrices(head_dim)
    tab_spec = pl.BlockSpec((tm, pair), lambda j, i: (i % rpb, 0))
    const_spec = pl.BlockSpec((pair, pair), lambda j, i: (0, 0))
    return pl.pallas_call(
        functools.partial(_qk_kernel, head_dim=head_dim),
        grid=(nj, t // tm),
        in_specs=[h_spec, w_spec, pl.BlockSpec((1, pair), lambda j, i: (0, 0)),
                  tab_spec, tab_spec, const_spec, const_spec],
        out_specs=out_spec, out_shape=out_shape, scratch_shapes=w_scratch, compiler_params=cp,
        name="qk_proj",
    )(h, w_qkv, jnp.tile(gain.reshape(1, head_dim), (1, 2)), cos, sin, ones, perm)


def _attn_kernel(q_ref, k_ref, v_ref, o_ref, vt_scr, s_scr):
    seq, hd = q_ref.shape
    blk = MOBA_BLOCK
    nb = seq // blk

    rows = []
    for n in range(nb):
        vt_scr[n, 0:hd, :] = v_ref[n * blk:(n + 1) * blk, :].astype(F32).T.astype(BF16)
        vt_scr[n, hd:, :] = jnp.ones((vt_scr.shape[1] - hd, blk), BF16)
        kb = k_ref[n * blk:(n + 1) * blk, :].astype(F32)
        rows.append(jnp.sum(kb, axis=0, keepdims=True) * (1.0 / blk))
    km = jnp.concatenate(rows, axis=0)
    km_a = km.astype(BF16)
    rem = km - km_a.astype(F32)
    km_b = rem.astype(BF16)
    km_c = (rem - km_b.astype(F32)).astype(BF16)
    km_parts = jnp.concatenate([km_a, km_b, km_c, jnp.zeros_like(km_a)], axis=0)

    causal = (lax.broadcasted_iota(jnp.int32, (blk, blk), 0)
              <= lax.broadcasted_iota(jnp.int32, (blk, blk), 1))

    def masked_scores(qi):
        qb = q_ref[qi * blk:(qi + 1) * blk, :]
        sel = None
        if qi > MOBA_TOPK:
            g3 = lax.dot_general(km_parts, qb, _NT_DIMS, preferred_element_type=F32)
            gate = g3[0:nb] + g3[nb:2 * nb] + g3[2 * nb:3 * nb]
            g_row = [gate[n:n + 1, :] for n in range(qi)]
            rank = [jnp.zeros((1, blk), F32) for _ in range(qi)]
            for a in range(qi):
                for b in range(a + 1, qi):
                    a_first = (g_row[a] >= g_row[b]).astype(F32)
                    rank[b] = rank[b] + a_first
                    rank[a] = rank[a] + (1.0 - a_first)
            sel = [rank[n] < MOBA_TOPK for n in range(qi)]

        m_run = None
        for n in range(qi + 1):
            s = lax.dot_general(k_ref[n * blk:(n + 1) * blk, :], qb, _NT_DIMS,
                                preferred_element_type=F32)
            if n == qi:
                s = jnp.where(causal, s, NEG_INF)
            elif sel is not None:
                s = jnp.where(sel[n], s, NEG_INF)
            s_scr[qi % 2, n] = s
            cm = jnp.max(s, axis=0, keepdims=True)
            m_run = cm if m_run is None else jnp.maximum(m_run, cm)
        return m_run

    def weighted_values(qi, m_run):
        acc = None
        for n in range(qi + 1):
            p = jnp.exp2(s_scr[qi % 2, n] - m_run)
            pv = _dot(vt_scr[n], p.astype(BF16))
            acc = pv if acc is None else acc + pv
        out_t = acc[0:hd] * (1.0 / acc[hd:hd + 1])
        o_ref[qi * blk:(qi + 1) * blk, :] = out_t.T.astype(o_ref.dtype)

    m_next = masked_scores(0)
    for qi in range(nb):
        m_cur = m_next
        if qi + 1 < nb:
            m_next = masked_scores(qi + 1)
        weighted_values(qi, m_cur)


def _attn_call(q, k, v, seq, head_dim):
    t, d = q.shape
    nh = d // head_dim
    nb = seq // MOBA_BLOCK
    spec = pl.BlockSpec((seq, head_dim), lambda b, h: (b, h))
    return pl.pallas_call(
        _attn_kernel,
        grid=(t // seq, nh),
        in_specs=[spec, spec, spec],
        out_specs=spec,
        out_shape=jax.ShapeDtypeStruct((t, d), BF16),
        scratch_shapes=[pltpu.VMEM((nb, head_dim + BF16_SUBLANES, MOBA_BLOCK), BF16),
                        pltpu.VMEM((2, nb, MOBA_BLOCK, MOBA_BLOCK), F32)],
        compiler_params=_params(("arbitrary", "arbitrary"), 32),
        name="moba_attention",
    )(q, k, v)


def _route(h, w, bias, n_experts):
    h_a = h.astype(BF16)
    h_b = (h - h_a.astype(F32)).astype(BF16)
    w_a = w.astype(BF16)
    w_b = (w - w_a.astype(F32)).astype(BF16)
    logits = _dot(h_a, w_a) + _dot(h_b, w_a) + _dot(h_a, w_b) + bias
    lane = lax.broadcasted_iota(jnp.int32, logits.shape, 1)
    logits = jnp.where(lane < n_experts, logits, -jnp.inf)
    m1 = jnp.max(logits, axis=-1, keepdims=True)
    i1 = jnp.min(jnp.where(logits == m1, lane, LANES), axis=-1, keepdims=True)
    rest = jnp.where(lane == i1, -jnp.inf, logits)
    m2 = jnp.max(rest, axis=-1, keepdims=True)
    i2 = jnp.min(jnp.where(rest == m2, lane, LANES), axis=-1, keepdims=True)
    e2 = jnp.exp(m2 - m1)
    w1 = 1.0 / (1.0 + e2)
    w2 = e2 * w1
    return (jnp.where(lane == 0, i1, jnp.where(lane == 1, i2, 0)),
            jnp.where(lane == 0, w1, jnp.where(lane == 1, w2, 0.0)))


def _dispatch_kernel(zb_ref, nz_ref, dest_ref, h_ref, xs_ref, zbuf, sem, zsem):
    tm = h_ref.shape[0]
    zrows = zbuf.shape[0]

    @pl.when(pl.program_id(0) == 0)
    def _():
        zbuf[...] = jnp.zeros(zbuf.shape, zbuf.dtype)

        def zero_copy(k):
            row0 = pl.multiple_of(zb_ref[k] * zrows, zrows)
            return pltpu.make_async_copy(zbuf, xs_ref.at[pl.ds(row0, zrows), :], zsem)

        def start(k, c):
            zero_copy(k).start()
            return c

        def wait(k, c):
            zero_copy(k).wait()
            return c

        lax.fori_loop(0, nz_ref[0], start, 0)
        lax.fori_loop(0, nz_ref[0], wait, 0)

    def issue(r, c):
        for k in range(EXPERT_TOPK):
            d = dest_ref[0, 0, EXPERT_TOPK * r + k]
            pltpu.make_async_copy(h_ref.at[pl.ds(r, 1), :], xs_ref.at[pl.ds(d, 1), :],
                                  sem).start(priority=k % 2)
        return c

    lax.fori_loop(0, tm, issue, 0, unroll=ROW_DMA_UNROLL)
    for k in range(EXPERT_TOPK):
        pltpu.make_async_copy(h_ref, xs_ref.at[pl.ds(0, tm), :], sem).wait()


def _dispatch_call(h, dest, tables, n_rows, block_rows, tm=512):
    t, d = h.shape
    nt = t // tm
    grid_spec = pltpu.PrefetchScalarGridSpec(
        num_scalar_prefetch=2,
        grid=(nt,),
        in_specs=[
            pl.BlockSpec((1, 1, EXPERT_TOPK * tm), lambda i, zb, nz: (i, 0, 0),
                         memory_space=pltpu.SMEM),
            pl.BlockSpec((tm, d), lambda i, zb, nz: (i, 0)),
        ],
        out_specs=pl.BlockSpec(memory_space=pl.ANY),
        scratch_shapes=[pltpu.VMEM((block_rows, d), h.dtype), pltpu.SemaphoreType.DMA,
                        pltpu.SemaphoreType.DMA],
    )
    return pl.pallas_call(
        _dispatch_kernel,
        grid_spec=grid_spec,
        out_shape=jax.ShapeDtypeStruct((n_rows, d), h.dtype),
        compiler_params=_params(("arbitrary",), 32),
        name="moe_dispatch",
    )(tables["zero_blocks"], tables["n_zero"], dest.reshape(nt, 1, EXPERT_TOPK * tm), h)


BLOCK_COMPUTE, BLOCK_TAIL, BLOCK_UNUSED = 0, 1, 2


def _expert_kernel(kind_ref, xmap_ref, we_ref, tmap_ref, xs_ref, wg_ref, wu_ref, wd_ref,
                   ytail_ref, ys_ref, xb_scr):
    del xmap_ref, we_ref, tmap_ref
    i = pl.program_id(0)
    j = pl.program_id(1)
    kind = kind_ref[i]

    @pl.when((kind == BLOCK_UNUSED) & (j == 0))
    def _():
        ys_ref[...] = jnp.zeros(ys_ref.shape, F32)

    @pl.when((kind == BLOCK_TAIL) & (j == 0))
    def _():
        ys_ref[...] = ytail_ref[...]

    @pl.when(kind == BLOCK_COMPUTE)
    def _():
        @pl.when(j == 0)
        def _():
            xb_scr[...] = xs_ref[...].astype(BF16)
            ys_ref[...] = jnp.zeros(ys_ref.shape, F32)

        x = xb_scr[...]
        act = _silu(_dot(x, wg_ref[0])) * _dot(x, wu_ref[0])
        ys_ref[...] += _dot(act.astype(BF16), wd_ref[0])


def _expert_call(xs, tables, y_tail, wg, wu, wd, tm, tf=512):
    n_rows, d = xs.shape
    f = wg.shape[2]
    nf = f // tf

    def wj(i, j, kind):
        return jnp.where(kind[i] == BLOCK_COMPUTE, j, nf - 1)

    grid_spec = pltpu.PrefetchScalarGridSpec(
        num_scalar_prefetch=4,
        grid=(n_rows // tm, nf),
        in_specs=[
            pl.BlockSpec((tm, d), lambda i, j, kind, xm, we, fm: (xm[i], 0)),
            pl.BlockSpec((1, d, tf), lambda i, j, kind, xm, we, fm: (we[i], 0, wj(i, j, kind))),
            pl.BlockSpec((1, d, tf), lambda i, j, kind, xm, we, fm: (we[i], 0, wj(i, j, kind))),
            pl.BlockSpec((1, tf, d), lambda i, j, kind, xm, we, fm: (we[i], wj(i, j, kind), 0)),
            pl.BlockSpec((tm, d), lambda i, j, kind, xm, we, fm: (fm[i], 0)),
        ],
        out_specs=pl.BlockSpec((tm, d), lambda i, j, kind, xm, we, fm: (i, 0)),
        scratch_shapes=[pltpu.VMEM((tm, d), BF16)],
    )
    return pl.pallas_call(
        _expert_kernel,
        grid_spec=grid_spec,
        out_shape=jax.ShapeDtypeStruct((n_rows, d), F32),
        compiler_params=_params(("arbitrary", "arbitrary"), 52),
        name="moe_experts",
    )(tables["kind"], tables["x_map"], tables["w_expert"], tables["tail_map"],
      xs, wg, wu, wd, y_tail)


TAIL_BLOCKS = 2


def _expert_tail_kernel(tb_ref, nt_ref, *refs):
    del tb_ref
    xs_refs = refs[:TAIL_BLOCKS]
    wg_ref, wu_ref, wd_ref, ys_ref, wgb_ref, wub_ref, wdb_ref, xb_scr = refs[TAIL_BLOCKS:]
    tm = xs_refs[0].shape[0]
    e = pl.program_id(0)
    j = pl.program_id(1)

    @pl.when(j == 0)
    def _():
        for k, xs_ref in enumerate(xs_refs):
            xb_scr[k * tm:(k + 1) * tm, :] = xs_ref[...].astype(BF16)
        ys_ref[...] = jnp.zeros(ys_ref.shape, F32)

    x = xb_scr[...]
    wg = wg_ref[0].astype(BF16)
    wu = wu_ref[0].astype(BF16)
    act = _silu(_dot(x, wg)) * _dot(x, wu)
    wd = wd_ref[0].astype(BF16)
    part = _dot(act.astype(BF16), wd)
    for k in range(TAIL_BLOCKS):
        rows = slice(k * tm, (k + 1) * tm)
        ys_ref[rows, :] += jnp.where(nt_ref[e] >= TAIL_BLOCKS - k, part[rows, :], 0.0)
    wgb_ref[0] = wg
    wub_ref[0] = wu
    wdb_ref[0] = wd


def _expert_tail_call(xs, tables, wg, wu, wd, tm, tf=256):
    d = xs.shape[1]
    n_experts, _, f = wg.shape

    def xs_spec(k):
        return pl.BlockSpec((tm, d), lambda e, j, tb, nt: (tb[TAIL_BLOCKS * e + k], 0),
                            pipeline_mode=pl.Buffered(1))

    w_in = lambda e, j, tb, nt: (e, 0, j)
    w_out = lambda e, j, tb, nt: (e, j, 0)
    grid_spec = pltpu.PrefetchScalarGridSpec(
        num_scalar_prefetch=2,
        grid=(n_experts, f // tf),
        in_specs=[xs_spec(k) for k in range(TAIL_BLOCKS)] + [
            pl.BlockSpec((1, d, tf), w_in),
            pl.BlockSpec((1, d, tf), w_in),
            pl.BlockSpec((1, tf, d), w_out),
        ],
        out_specs=[
            pl.BlockSpec((TAIL_BLOCKS * tm, d), lambda e, j, tb, nt: (e, 0)),
            pl.BlockSpec((1, d, tf), w_in),
            pl.BlockSpec((1, d, tf), w_in),
            pl.BlockSpec((1, tf, d), w_out),
        ],
        scratch_shapes=[pltpu.VMEM((TAIL_BLOCKS * tm, d), BF16)],
    )
    return pl.pallas_call(
        _expert_tail_kernel,
        grid_spec=grid_spec,
        out_shape=[
            jax.ShapeDtypeStruct((n_experts * TAIL_BLOCKS * tm, d), F32),
            jax.ShapeDtypeStruct(wg.shape, BF16),
            jax.ShapeDtypeStruct(wu.shape, BF16),
            jax.ShapeDtypeStruct(wd.shape, BF16),
        ],
        compiler_params=_params(("arbitrary", "arbitrary"), 58),
        name="moe_experts_tail",
    )(tables["tail_block"], tables["n_tail"], *([xs] * TAIL_BLOCKS), wg, wu, wd)


def _combine_kernel(dest_ref, dest_next_ref, ys_ref, x_ref, wt_ref, g_ref, o_ref, ybuf, sems):
    tm = x_ref.shape[0]
    i = pl.program_id(0)
    slot = i % 2

    def start_gathers(d_ref, dst_slot):
        def issue(r, c):
            for k in range(EXPERT_TOPK):
                d = d_ref[0, 0, EXPERT_TOPK * r + k]
                pltpu.make_async_copy(ys_ref.at[pl.ds(d, 1), :],
                                      ybuf.at[dst_slot, k, pl.ds(r, 1), :],
                                      sems.at[dst_slot]).start(priority=k % 2)
            return c

        lax.fori_loop(0, tm, issue, 0, unroll=ROW_DMA_UNROLL)

    @pl.when(i == 0)
    def _():
        start_gathers(dest_ref, 0)

    @pl.when(i + 1 < pl.num_programs(0))
    def _():
        start_gathers(dest_next_ref, 1 - slot)

    for k in range(EXPERT_TOPK):
        pltpu.make_async_copy(ys_ref.at[pl.ds(0, tm), :], ybuf.at[slot, k], sems.at[slot]).wait()
    wt = wt_ref[...]
    y = wt[:, 0:1] * ybuf[slot, 0]
    for k in range(1, EXPERT_TOPK):
        y = y + wt[:, k:k + 1] * ybuf[slot, k]
    o_ref[...] = x_ref[...] + g_ref[0] * y


def _combine_call(ys, dest, x, wt, modr, row, g_chunk, seq, tm=512):
    t, d = x.shape
    nt = t // tm
    rpb = seq // tm
    dest3 = dest.reshape(nt, 1, EXPERT_TOPK * tm)
    return pl.pallas_call(
        _combine_kernel,
        grid=(nt,),
        in_specs=[
            pl.BlockSpec((1, 1, EXPERT_TOPK * tm), lambda i: (i, 0, 0), memory_space=pltpu.SMEM),
            pl.BlockSpec((1, 1, EXPERT_TOPK * tm), lambda i: (jnp.minimum(i + 1, nt - 1), 0, 0),
                         memory_space=pltpu.SMEM),
            pl.BlockSpec(memory_space=pl.ANY),
            pl.BlockSpec((tm, d), lambda i: (i, 0)),
            pl.BlockSpec((tm, LANES), lambda i: (i, 0)),
            _mod_spec(d, rpb, row, g_chunk),
        ],
        out_specs=pl.BlockSpec((tm, d), lambda i: (i, 0)),
        out_shape=jax.ShapeDtypeStruct((t, d), F32),
        scratch_shapes=[pltpu.VMEM((2, EXPERT_TOPK, tm, d), F32), pltpu.SemaphoreType.DMA((2,))],
        compiler_params=_params(("arbitrary",), 40),
        name="moe_combine",
    )(dest3, dest3, ys, x, wt, modr)


def _routing_tables(top_idx, n_experts, tm, n_blocks):
    flat_e = top_idx.reshape(-1)
    onehot = (flat_e[:, None] == jnp.arange(n_experts, dtype=jnp.int32)[None, :]).astype(jnp.int32)
    csum = jnp.cumsum(onehot, axis=0)
    counts = csum[-1]
    pos = jnp.sum((csum - 1) * onehot, axis=1)
    nblk = (counts + tm - 1) // tm
    blk_end = jnp.cumsum(nblk)
    blk_start = blk_end - nblk
    dest = (blk_start[flat_e] * tm + pos).astype(jnp.int32)
    n_used = blk_end[-1]

    blocks = jnp.arange(n_blocks, dtype=jnp.int32)
    x_map = jnp.minimum(blocks, n_used - 1)
    blk_e = jnp.minimum(jnp.sum(blk_end[None, :] <= x_map[:, None], axis=1), n_experts - 1)
    from_end = blk_end[blk_e] - blocks
    is_tail = (from_end <= TAIL_BLOCKS) & (blocks < n_used)
    kind = jnp.where(blocks >= n_used, BLOCK_UNUSED,
                     jnp.where(is_tail, BLOCK_TAIL, BLOCK_COMPUTE))
    last_compute = lax.cummax(jnp.where(kind == BLOCK_COMPUTE, blocks, -1))
    w_expert = blk_e[jnp.maximum(last_compute, 0)]
    tail_map = lax.cummax(jnp.where(is_tail, TAIL_BLOCKS * (blk_e + 1) - from_end, 0))
    slot_from_end = TAIL_BLOCKS - jnp.arange(TAIL_BLOCKS, dtype=jnp.int32)
    tail_block = jnp.clip(blk_end[:, None] - slot_from_end[None, :], 0, n_blocks - 1)
    cand = jnp.concatenate([blk_end - 1, blocks])
    cand_ok = jnp.concatenate([nblk > 0, blocks >= n_used])
    zero_blocks = jnp.maximum(cand[jnp.argsort(~cand_ok, stable=True)], 0)
    i32 = lambda a: a.astype(jnp.int32)
    tables = dict(kind=i32(kind), x_map=i32(x_map), w_expert=i32(w_expert),
                  tail_map=i32(tail_map), tail_block=i32(tail_block.reshape(-1)),
                  n_tail=i32(jnp.minimum(nblk, TAIL_BLOCKS)),
                  zero_blocks=i32(zero_blocks), n_zero=i32(jnp.sum(cand_ok)).reshape(1))
    return dest, tables


def _rope_tables(seq, head_dim, scale=1.0):
    half = head_dim // 2
    inv_freq = jnp.exp(-math.log(ROPE_THETA) * jnp.arange(half, dtype=F32) / half)
    ang = jnp.arange(seq).astype(F32)[:, None] * inv_freq[None, :]
    cos, sin = jnp.cos(ang) * scale, jnp.sin(ang) * scale
    return (jnp.concatenate([cos, cos, cos, cos], axis=-1),
            jnp.concatenate([-sin, sin, -sin, sin], axis=-1))


def kernel(x, c, mod_w, mod_b, norm_mix, norm_ffn, conv_in, conv_w, conv_out, ffn_gate, ffn_up,
           ffn_down, qkv_w, q_norm, k_norm, attn_out, router_w, router_b, exp_gate, exp_up,
           exp_down):
    bsz, seq, d = x.shape
    depth = mod_w.shape[0]
    assert depth == 2, "layer 0 = short-conv + dense FFN, layer 1 = MoBA + MoE"
    t = bsz * seq
    head_dim = d // N_HEADS
    n_experts = router_w.shape[-1]

    mod = _mod_call(c, mod_w, mod_b)
    modr = mod.reshape(depth * bsz, 1, 6 * d)
    xt = x.reshape(t, d)

    row = bsz
    h = _normmod_call(xt, norm_mix[0], modr, 0, 1, 0, seq)
    bu = _convin_call(h, conv_in[0], conv_w[0], seq)
    xt, h = _resmm_call(bu, conv_out[0].astype(BF16), xt, modr, 0, 2, seq,
                        next_norm=(norm_ffn[0], 0, 4, 3))
    xt, h = _ffn_call(h, ffn_gate[0].astype(BF16), ffn_up[0].astype(BF16),
                      ffn_down[0].astype(BF16), xt, modr, 0, 5, seq,
                      next_norm=(norm_mix[1], row, 1, 0))

    w_qkv = qkv_w[0]
    cos_q, sin_q = _rope_tables(seq, head_dim, head_dim ** -0.5 * math.log2(math.e))
    cos_k, sin_k = _rope_tables(seq, head_dim)
    q = _qkv_call(h, w_qkv, 0, seq, head_dim, q_norm[0], cos_q, sin_q)
    k = _qkv_call(h, w_qkv, 1, seq, head_dim, k_norm[0], cos_k, sin_k)
    v = _qkv_call(h, w_qkv, 2, seq, head_dim)
    o = _attn_call(q, k, v, seq, head_dim)
    xt, hf, top_idx, top_w = _resmm_call(o, attn_out[0].astype(BF16), xt, modr, row, 2, seq,
                                         next_norm=(norm_ffn[1], row, 4, 3),
                                         router=(router_w[0], router_b[0]))
    n_blocks = (t * EXPERT_TOPK) // EXPERT_ROWS + n_experts
    dest, tables = _routing_tables(top_idx[:, :EXPERT_TOPK], n_experts, EXPERT_ROWS, n_blocks)
    xs = _dispatch_call(hf, dest, tables, n_blocks * EXPERT_ROWS, EXPERT_ROWS)
    y_tail, wg, wu, wd = _expert_tail_call(xs, tables, exp_gate[0], exp_up[0], exp_down[0],
                                           EXPERT_ROWS)
    ys = _expert_call(xs, tables, y_tail, wg, wu, wd, EXPERT_ROWS)
    xt = _combine_call(ys, dest, xt, top_w, modr, row, 5, seq)
    return xt.reshape(bsz, seq, d)
```

```python
import functools
import math

import jax
import jax.numpy as jnp
from jax import lax
from jax.experimental import pallas as pl
from jax.experimental.pallas import tpu as pltpu

N_HEADS = 16
CONV_WIDTH = 3
MOBA_BLOCK = 256
MOBA_TOPK = 3
ROPE_THETA = 10000.0
EXPERT_TOPK = 2
EXPERT_ROWS = 512
ROW_DMA_UNROLL = 8
NORM_EPS = 1e-6
NEG_INF = -1e30

LANES = 128
SUBLANES = 8
BF16_SUBLANES = 16
MIB = 1024 * 1024

F32 = jnp.float32
BF16 = jnp.bfloat16

_NT_DIMS = (((1,), (1,)), ((), ()))


def _params(semantics, vmem_mib):
    return pltpu.CompilerParams(dimension_semantics=semantics,
                                vmem_limit_bytes=vmem_mib * MIB)


def _dot(a, b):
    return jnp.dot(a, b, preferred_element_type=F32)


def _silu(a):
    return a * (1.0 / (1.0 + jnp.exp(-a)))


def _cast_weight_once(w_ref, w_scr):
    @pl.when(pl.program_id(1) == 0)
    def _():
        w_scr[...] = w_ref[...].astype(BF16)


def _mod_kernel(c_ref, w_ref, b_ref, o_ref):
    ca = _silu(c_ref[...])
    o_ref[0] = _dot(ca.astype(BF16), w_ref[0].astype(BF16)) + b_ref[0]


def _mod_call(c, mod_w, mod_b, tn=1024):
    depth, d, n = mod_w.shape
    bsz = c.shape[0]
    return pl.pallas_call(
        _mod_kernel,
        grid=(depth, n // tn),
        in_specs=[
            pl.BlockSpec((bsz, d), lambda l, j: (0, 0)),
            pl.BlockSpec((1, d, tn), lambda l, j: (l, 0, j)),
            pl.BlockSpec((1, 1, tn), lambda l, j: (l, 0, j)),
        ],
        out_specs=pl.BlockSpec((1, bsz, tn), lambda l, j: (l, 0, j)),
        out_shape=jax.ShapeDtypeStruct((depth, bsz, n), F32),
        compiler_params=_params(("arbitrary", "arbitrary"), 40),
        name="adaln_mod",
    )(c, mod_w, mod_b.reshape(depth, 1, n))


def _normmod(x, gain, scale, shift):
    ms = jnp.mean(x * x, axis=-1, keepdims=True)
    y = x * lax.rsqrt(ms + NORM_EPS) * gain
    return y * (1.0 + scale) + shift


def _normmod_kernel(x_ref, gain_ref, sc_ref, sh_ref, o_ref):
    o_ref[...] = _normmod(x_ref[...], gain_ref[...], sc_ref[0], sh_ref[0]).astype(o_ref.dtype)


def _mod_spec(d, rows_per_batch, row, chunk):
    return pl.BlockSpec((1, 1, d), lambda i, *_: (row + i // rows_per_batch, 0, chunk))


def _normmod_call(x, gain, modr, row, sc_chunk, sh_chunk, seq, tm=512):
    t, d = x.shape
    rpb = seq // tm
    return pl.pallas_call(
        _normmod_kernel,
        grid=(t // tm,),
        in_specs=[
            pl.BlockSpec((tm, d), lambda i: (i, 0)),
            pl.BlockSpec((1, d), lambda i: (0, 0)),
            _mod_spec(d, rpb, row, sc_chunk),
            _mod_spec(d, rpb, row, sh_chunk),
        ],
        out_specs=pl.BlockSpec((tm, d), lambda i: (i, 0)),
        out_shape=jax.ShapeDtypeStruct((t, d), BF16),
        compiler_params=_params(("arbitrary",), 32),
        name="normmod",
    )(x, gain.reshape(1, d), modr, modr)


def _convin_kernel(h_ref, wb_ref, wc_ref, wv_ref, cw_ref, o_ref, u_scr, wb_scr, wc_scr, wv_scr):
    seq = h_ref.shape[0]
    for w_ref, w_scr in ((wb_ref, wb_scr), (wc_ref, wc_scr), (wv_ref, wv_scr)):
        _cast_weight_once(w_ref, w_scr)
    h = h_ref[...]
    u = _dot(h, wc_scr[...]) * _dot(h, wv_scr[...])
    u_scr[0:SUBLANES, :] = jnp.zeros((SUBLANES, u.shape[1]), F32)
    u_scr[SUBLANES:, :] = u
    cw = cw_ref[...]
    conv = cw[CONV_WIDTH - 1:CONV_WIDTH, :] * u
    for tap in range(1, CONV_WIDTH):
        conv = conv + (cw[CONV_WIDTH - 1 - tap:CONV_WIDTH - tap, :]
                       * u_scr[SUBLANES - tap:SUBLANES - tap + seq, :])
    o_ref[...] = (_dot(h, wb_scr[...]) * conv).astype(o_ref.dtype)


def _convin_call(h, w_in, conv_w, seq, tn=256):
    t, d = h.shape
    nj = d // tn
    return pl.pallas_call(
        _convin_kernel,
        grid=(nj, t // seq),
        in_specs=[
            pl.BlockSpec((seq, d), lambda j, b: (b, 0)),
            pl.BlockSpec((d, tn), lambda j, b: (0, j)),
            pl.BlockSpec((d, tn), lambda j, b: (0, j + nj)),
            pl.BlockSpec((d, tn), lambda j, b: (0, j + 2 * nj)),
            pl.BlockSpec((CONV_WIDTH, tn), lambda j, b: (0, j)),
        ],
        out_specs=pl.BlockSpec((seq, tn), lambda j, b: (b, j)),
        out_shape=jax.ShapeDtypeStruct((t, d), BF16),
        scratch_shapes=[pltpu.VMEM((seq + SUBLANES, tn), F32)] + [pltpu.VMEM((d, tn), BF16)] * 3,
        compiler_params=_params(("arbitrary", "arbitrary"), 54),
        name="conv_in",
    )(h, w_in, w_in, w_in, conv_w)


def _resmm_norm_kernel(a_ref, w_ref, x_ref, g_ref, gain_ref, sc_ref, sh_ref, o_ref, h_ref):
    xn = x_ref[...] + g_ref[0] * _dot(a_ref[...], w_ref[...])
    o_ref[...] = xn
    h_ref[...] = _normmod(xn, gain_ref[...], sc_ref[0], sh_ref[0]).astype(h_ref.dtype)


def _resmm_route_kernel(a_ref, w_ref, x_ref, g_ref, gain_ref, sc_ref, sh_ref, wr_ref, br_ref,
                        o_ref, h_ref, idx_ref, wt_ref, *, n_experts):
    xn = x_ref[...] + g_ref[0] * _dot(a_ref[...], w_ref[...])
    o_ref[...] = xn
    h = _normmod(xn, gain_ref[...], sc_ref[0], sh_ref[0])
    h_ref[...] = h
    idx_ref[...], wt_ref[...] = _route(h, wr_ref[...], br_ref[...], n_experts)


def _resmm_call(a, w, x, modr, row, g_chunk, seq, next_norm, router=None, tm=512):
    t, k = a.shape
    d = w.shape[1]
    rpb = seq // tm
    gain, nrow, sc_chunk, sh_chunk = next_norm
    in_specs = [
        pl.BlockSpec((tm, k), lambda i: (i, 0)),
        pl.BlockSpec((k, d), lambda i: (0, 0)),
        pl.BlockSpec((tm, d), lambda i: (i, 0)),
        _mod_spec(d, rpb, row, g_chunk),
        pl.BlockSpec((1, d), lambda i: (0, 0)),
        _mod_spec(d, rpb, nrow, sc_chunk),
        _mod_spec(d, rpb, nrow, sh_chunk),
    ]
    args = (a, w, x, modr, gain.reshape(1, d), modr, modr)
    row_spec = pl.BlockSpec((tm, d), lambda i: (i, 0))
    lane_spec = pl.BlockSpec((tm, LANES), lambda i: (i, 0))
    cp = _params(("arbitrary",), 56)
    if router is None:
        return pl.pallas_call(
            _resmm_norm_kernel, grid=(t // tm,), in_specs=in_specs,
            out_specs=[row_spec, row_spec],
            out_shape=[jax.ShapeDtypeStruct((t, d), F32), jax.ShapeDtypeStruct((t, d), BF16)],
            compiler_params=cp, name="res_matmul_norm")(*args)
    w_router, b_router = router
    n_experts = w_router.shape[1]
    wr = jnp.zeros((d, LANES), F32).at[:, :n_experts].set(w_router)
    br = jnp.zeros((1, LANES), F32).at[0, :n_experts].set(b_router)
    in_specs += [pl.BlockSpec((d, LANES), lambda i: (0, 0)),
                 pl.BlockSpec((1, LANES), lambda i: (0, 0))]
    return pl.pallas_call(
        functools.partial(_resmm_route_kernel, n_experts=n_experts), grid=(t // tm,),
        in_specs=in_specs, out_specs=[row_spec, row_spec, lane_spec, lane_spec],
        out_shape=[jax.ShapeDtypeStruct((t, d), F32), jax.ShapeDtypeStruct((t, d), F32),
                   jax.ShapeDtypeStruct((t, LANES), jnp.int32),
                   jax.ShapeDtypeStruct((t, LANES), F32)],
        compiler_params=cp, name="res_matmul_route")(*args, wr, br)


def _ffn_kernel(h_ref, wg_ref, wu_ref, wd_ref, x_ref, g_ref, gain_ref, sc_ref, sh_ref,
                o_ref, hn_ref):
    j = pl.program_id(1)

    @pl.when(j == 0)
    def _():
        o_ref[...] = jnp.zeros(o_ref.shape, F32)

    h = h_ref[...]
    act = _silu(_dot(h, wg_ref[...])) * _dot(h, wu_ref[...])
    o_ref[...] += _dot(act.astype(BF16), wd_ref[...])

    @pl.when(j == pl.num_programs(1) - 1)
    def _():
        xn = x_ref[...] + g_ref[0] * o_ref[...]
        o_ref[...] = xn
        hn_ref[...] = _normmod(xn, gain_ref[...], sc_ref[0], sh_ref[0]).astype(hn_ref.dtype)


def _ffn_call(h, wg, wu, wd, x, modr, row, g_chunk, seq, next_norm, tm=512, tf=512):
    t, d = h.shape
    f = wg.shape[1]
    rpb = seq // tm
    gain, nrow, sc_chunk, sh_chunk = next_norm
    row_spec = pl.BlockSpec((tm, d), lambda i, j: (i, 0))
    return pl.pallas_call(
        _ffn_kernel,
        grid=(t // tm, f // tf),
        in_specs=[
            row_spec,
            pl.BlockSpec((d, tf), lambda i, j: (0, j)),
            pl.BlockSpec((d, tf), lambda i, j: (0, j)),
            pl.BlockSpec((tf, d), lambda i, j: (j, 0)),
            row_spec,
            _mod_spec(d, rpb, row, g_chunk),
            pl.BlockSpec((1, d), lambda i, j: (0, 0)),
            _mod_spec(d, rpb, nrow, sc_chunk),
            _mod_spec(d, rpb, nrow, sh_chunk),
        ],
        out_specs=[row_spec, row_spec],
        out_shape=[jax.ShapeDtypeStruct((t, d), F32), jax.ShapeDtypeStruct((t, d), BF16)],
        compiler_params=_params(("arbitrary", "arbitrary"), 52),
        name="dense_ffn",
    )(h, wg, wu, wd, x, modr, gain.reshape(1, d), modr, modr)


def _qk_kernel(h_ref, w_ref, gain_ref, cos_ref, sin_ref, ones_ref, perm_ref, o_ref, w_scr,
               *, head_dim):
    _cast_weight_once(w_ref, w_scr)
    acc = _dot(h_ref[...], w_scr[...])
    pair = ones_ref.shape[0]
    gain = gain_ref[...]
    cos = cos_ref[...]
    sin = sin_ref[...]
    ones = ones_ref[...]
    perm = perm_ref[...]
    for p0 in range(0, acc.shape[1], pair):
        xp = acc[:, p0:p0 + pair]
        ss = _dot((xp * xp).astype(BF16), ones)
        inv = lax.rsqrt(ss * (1.0 / head_dim) + NORM_EPS)
        z = xp * gain
        rot = _dot(z.astype(BF16), perm)
        out = (z * cos + rot * sin) * inv
        o_ref[:, p0:p0 + pair] = out.astype(o_ref.dtype)


def _v_kernel(h_ref, w_ref, o_ref, w_scr):
    _cast_weight_once(w_ref, w_scr)
    o_ref[...] = _dot(h_ref[...], w_scr[...]).astype(o_ref.dtype)


def _head_pair_matrices(head_dim):
    pair = 2 * head_dim
    src = jnp.arange(pair)[:, None]
    dst = jnp.arange(pair)[None, :]
    same_head = (src // head_dim) == (dst // head_dim)
    ones = same_head.astype(BF16)
    perm = (same_head & ((src % head_dim) == ((dst + head_dim // 2) % head_dim))).astype(BF16)
    return ones, perm


def _qkv_call(h, w_qkv, which, seq, head_dim, gain=None, cos=None, sin=None, tm=1024, tn=512):
    t, d = h.shape
    nj = d // tn
    off = which * nj
    rpb = seq // tm
    h_spec = pl.BlockSpec((tm, d), lambda j, i: (i, 0))
    w_spec = pl.BlockSpec((d, tn), lambda j, i: (0, j + off))
    out_spec = pl.BlockSpec((tm, tn), lambda j, i: (i, j))
    out_shape = jax.ShapeDtypeStruct((t, d), BF16)
    cp = _params(("arbitrary", "arbitrary"), 40)
    w_scratch = [pltpu.VMEM((d, tn), BF16)]
    if gain is None:
        return pl.pallas_call(_v_kernel, grid=(nj, t // tm), in_specs=[h_spec, w_spec],
                              out_specs=out_spec, out_shape=out_shape,
                              scratch_shapes=w_scratch, compiler_params=cp,
                              name="v_proj")(h, w_qkv)
    pair = 2 * head_dim
    ones, perm = _head_pair_matrices(head_dim)
    tab_spec = pl.BlockSpec((tm, pair), lambda j, i: (i % rpb, 0))
    const_spec = pl.BlockSpec((pair, pair), lambda j, i: (0, 0))
    return pl.pallas_call(
        functools.partial(_qk_kernel, head_dim=head_dim),
        grid=(nj, t // tm),
        in_specs=[h_spec, w_spec, pl.BlockSpec((1, pair), lambda j, i: (0, 0)),
                  tab_spec, tab_spec, const_spec, const_spec],
        out_specs=out_spec, out_shape=out_shape, scratch_shapes=w_scratch, compiler_params=cp,
        name="qk_proj",
    )(h, w_qkv, jnp.tile(gain.reshape(1, head_dim), (1, 2)), cos, sin, ones, perm)


def _attn_kernel(q_ref, k_ref, v_ref, o_ref, vt_scr, s_scr):
    seq, hd = q_ref.shape
    blk = MOBA_BLOCK
    nb = seq // blk

    rows = []
    for n in range(nb):
        vt_scr[n, 0:hd, :] = v_ref[n * blk:(n + 1) * blk, :].astype(F32).T.astype(BF16)
        vt_scr[n, hd:, :] = jnp.ones((vt_scr.shape[1] - hd, blk), BF16)
        kb = k_ref[n * blk:(n + 1) * blk, :].astype(F32)
        rows.append(jnp.sum(kb, axis=0, keepdims=True) * (1.0 / blk))
    km = jnp.concatenate(rows, axis=0)
    km_a = km.astype(BF16)
    rem = km - km_a.astype(F32)
    km_b = rem.astype(BF16)
    km_c = (rem - km_b.astype(F32)).astype(BF16)
    km_parts = jnp.concatenate([km_a, km_b, km_c, jnp.zeros_like(km_a)], axis=0)

    causal = (lax.broadcasted_iota(jnp.int32, (blk, blk), 0)
              <= lax.broadcasted_iota(jnp.int32, (blk, blk), 1))

    def masked_scores(qi):
        qb = q_ref[qi * blk:(qi + 1) * blk, :]
        sel = None
        if qi > MOBA_TOPK:
            g3 = lax.dot_general(km_parts, qb, _NT_DIMS, preferred_element_type=F32)
            gate = g3[0:nb] + g3[nb:2 * nb] + g3[2 * nb:3 * nb]
            g_row = [gate[n:n + 1, :] for n in range(qi)]
            rank = [jnp.zeros((1, blk), F32) for _ in range(qi)]
            for a in range(qi):
                for b in range(a + 1, qi):
                    a_first = (g_row[a] >= g_row[b]).astype(F32)
                    rank[b] = rank[b] + a_first
                    rank[a] = rank[a] + (1.0 - a_first)
            sel = [rank[n] < MOBA_TOPK for n in range(qi)]

        m_run = None
        for n in range(qi + 1):
            s = lax.dot_general(k_ref[n * blk:(n + 1) * blk, :], qb, _NT_DIMS,
                                preferred_element_type=F32)
            if n == qi:
                s = jnp.where(causal, s, NEG_INF)
            elif sel is not None:
                s = jnp.where(sel[n], s, NEG_INF)
            s_scr[qi % 2, n] = s
            cm = jnp.max(s, axis=0, keepdims=True)
            m_run = cm if m_run is None else jnp.maximum(m_run, cm)
        return m_run

    def weighted_values(qi, m_run):
        acc = None
        for n in range(qi + 1):
            p = jnp.exp2(s_scr[qi % 2, n] - m_run)
            pv = _dot(vt_scr[n], p.astype(BF16))
            acc = pv if acc is None else acc + pv
        out_t = acc[0:hd] * (1.0 / acc[hd:hd + 1])
        o_ref[qi * blk:(qi + 1) * blk, :] = out_t.T.astype(o_ref.dtype)

    m_next = masked_scores(0)
    for qi in range(nb):
        m_cur = m_next
        if qi + 1 < nb:
            m_next = masked_scores(qi + 1)
        weighted_values(qi, m_cur)


def _attn_call(q, k, v, seq, head_dim):
    t, d = q.shape
    nh = d // head_dim
    nb = seq // MOBA_BLOCK
    spec = pl.BlockSpec((seq, head_dim), lambda b, h: (b, h))
    return pl.pallas_call(
        _attn_kernel,
        grid=(t // seq, nh),
        in_specs=[spec, spec, spec],
        out_specs=spec,
        out_shape=jax.ShapeDtypeStruct((t, d), BF16),
        scratch_shapes=[pltpu.VMEM((nb, head_dim + BF16_SUBLANES, MOBA_BLOCK), BF16),
                        pltpu.VMEM((2, nb, MOBA_BLOCK, MOBA_BLOCK), F32)],
        compiler_params=_params(("arbitrary", "arbitrary"), 32),
        name="moba_attention",
    )(q, k, v)


def _route(h, w, bias, n_experts):
    h_a = h.astype(BF16)
    h_b = (h - h_a.astype(F32)).astype(BF16)
    w_a = w.astype(BF16)
    w_b = (w - w_a.astype(F32)).astype(BF16)
    logits = _dot(h_a, w_a) + _dot(h_b, w_a) + _dot(h_a, w_b) + bias
    lane = lax.broadcasted_iota(jnp.int32, logits.shape, 1)
    logits = jnp.where(lane < n_experts, logits, -jnp.inf)
    m1 = jnp.max(logits, axis=-1, keepdims=True)
    i1 = jnp.min(jnp.where(logits == m1, lane, LANES), axis=-1, keepdims=True)
    rest = jnp.where(lane == i1, -jnp.inf, logits)
    m2 = jnp.max(rest, axis=-1, keepdims=True)
    i2 = jnp.min(jnp.where(rest == m2, lane, LANES), axis=-1, keepdims=True)
    e2 = jnp.exp(m2 - m1)
    w1 = 1.0 / (1.0 + e2)
    w2 = e2 * w1
    return (jnp.where(lane == 0, i1, jnp.where(lane == 1, i2, 0)),
            jnp.where(lane == 0, w1, jnp.where(lane == 1, w2, 0.0)))


def _dispatch_kernel(zb_ref, nz_ref, dest_ref, h_ref, xs_ref, zbuf, sem, zsem):
    tm = h_ref.shape[0]
    zrows = zbuf.shape[0]

    @pl.when(pl.program_id(0) == 0)
    def _():
        zbuf[...] = jnp.zeros(zbuf.shape, zbuf.dtype)

        def zero_copy(k):
            row0 = pl.multiple_of(zb_ref[k] * zrows, zrows)
            return pltpu.make_async_copy(zbuf, xs_ref.at[pl.ds(row0, zrows), :], zsem)

        def start(k, c):
            zero_copy(k).start()
            return c

        def wait(k, c):
            zero_copy(k).wait()
            return c

        lax.fori_loop(0, nz_ref[0], start, 0)
        lax.fori_loop(0, nz_ref[0], wait, 0)

    def issue(r, c):
        for k in range(EXPERT_TOPK):
            d = dest_ref[0, 0, EXPERT_TOPK * r + k]
            pltpu.make_async_copy(h_ref.at[pl.ds(r, 1), :], xs_ref.at[pl.ds(d, 1), :],
                                  sem).start(priority=k % 2)
        return c

    lax.fori_loop(0, tm, issue, 0, unroll=ROW_DMA_UNROLL)
    for k in range(EXPERT_TOPK):
        pltpu.make_async_copy(h_ref, xs_ref.at[pl.ds(0, tm), :], sem).wait()


def _dispatch_call(h, dest, tables, n_rows, block_rows, tm=512):
    t, d = h.shape
    nt = t // tm
    grid_spec = pltpu.PrefetchScalarGridSpec(
        num_scalar_prefetch=2,
        grid=(nt,),
        in_specs=[
            pl.BlockSpec((1, 1, EXPERT_TOPK * tm), lambda i, zb, nz: (i, 0, 0),
                         memory_space=pltpu.SMEM),
            pl.BlockSpec((tm, d), lambda i, zb, nz: (i, 0)),
        ],
        out_specs=pl.BlockSpec(memory_space=pl.ANY),
        scratch_shapes=[pltpu.VMEM((block_rows, d), h.dtype), pltpu.SemaphoreType.DMA,
                        pltpu.SemaphoreType.DMA],
    )
    return pl.pallas_call(
        _dispatch_kernel,
        grid_spec=grid_spec,
        out_shape=jax.ShapeDtypeStruct((n_rows, d), h.dtype),
        compiler_params=_params(("arbitrary",), 32),
        name="moe_dispatch",
    )(tables["zero_blocks"], tables["n_zero"], dest.reshape(nt, 1, EXPERT_TOPK * tm), h)


BLOCK_COMPUTE, BLOCK_TAIL, BLOCK_UNUSED = 0, 1, 2


def _expert_kernel(kind_ref, xmap_ref, we_ref, tmap_ref, xs_ref, wg_ref, wu_ref, wd_ref,
                   ytail_ref, ys_ref, xb_scr):
    del xmap_ref, we_ref, tmap_ref
    i = pl.program_id(0)
    j = pl.program_id(1)
    kind = kind_ref[i]

    @pl.when((kind == BLOCK_UNUSED) & (j == 0))
    def _():
        ys_ref[...] = jnp.zeros(ys_ref.shape, F32)

    @pl.when((kind == BLOCK_TAIL) & (j == 0))
    def _():
        ys_ref[...] = ytail_ref[...]

    @pl.when(kind == BLOCK_COMPUTE)
    def _():
        @pl.when(j == 0)
        def _():
            xb_scr[...] = xs_ref[...].astype(BF16)
            ys_ref[...] = jnp.zeros(ys_ref.shape, F32)

        x = xb_scr[...]
        act = _silu(_dot(x, wg_ref[0])) * _dot(x, wu_ref[0])
        ys_ref[...] += _dot(act.astype(BF16), wd_ref[0])


def _expert_call(xs, tables, y_tail, wg, wu, wd, tm, tf=512):
    n_rows, d = xs.shape
    f = wg.shape[2]
    nf = f // tf

    def wj(i, j, kind):
        return jnp.where(kind[i] == BLOCK_COMPUTE, j, nf - 1)

    grid_spec = pltpu.PrefetchScalarGridSpec(
        num_scalar_prefetch=4,
        grid=(n_rows // tm, nf),
        in_specs=[
            pl.BlockSpec((tm, d), lambda i, j, kind, xm, we, fm: (xm[i], 0)),
            pl.BlockSpec((1, d, tf), lambda i, j, kind, xm, we, fm: (we[i], 0, wj(i, j, kind))),
            pl.BlockSpec((1, d, tf), lambda i, j, kind, xm, we, fm: (we[i], 0, wj(i, j, kind))),
            pl.BlockSpec((1, tf, d), lambda i, j, kind, xm, we, fm: (we[i], wj(i, j, kind), 0)),
            pl.BlockSpec((tm, d), lambda i, j, kind, xm, we, fm: (fm[i], 0)),
        ],
        out_specs=pl.BlockSpec((tm, d), lambda i, j, kind, xm, we, fm: (i, 0)),
        scratch_shapes=[pltpu.VMEM((tm, d), BF16)],
    )
    return pl.pallas_call(
        _expert_kernel,
        grid_spec=grid_spec,
        out_shape=jax.ShapeDtypeStruct((n_rows, d), F32),
        compiler_params=_params(("arbitrary", "arbitrary"), 52),
        name="moe_experts",
    )(tables["kind"], tables["x_map"], tables["w_expert"], tables["tail_map"],
      xs, wg, wu, wd, y_tail)


TAIL_BLOCKS = 2


def _expert_tail_kernel(tb_ref, nt_ref, *refs):
    del tb_ref
    xs_refs = refs[:TAIL_BLOCKS]
    wg_ref, wu_ref, wd_ref, ys_ref, wgb_ref, wub_ref, wdb_ref, xb_scr = refs[TAIL_BLOCKS:]
    tm = xs_refs[0].shape[0]
    e = pl.program_id(0)
    j = pl.program_id(1)

    @pl.when(j == 0)
    def _():
        for k, xs_ref in enumerate(xs_refs):
            xb_scr[k * tm:(k + 1) * tm, :] = xs_ref[...].astype(BF16)
        ys_ref[...] = jnp.zeros(ys_ref.shape, F32)

    x = xb_scr[...]
    wg = wg_ref[0].astype(BF16)
    wu = wu_ref[0].astype(BF16)
    act = _silu(_dot(x, wg)) * _dot(x, wu)
    wd = wd_ref[0].astype(BF16)
    part = _dot(act.astype(BF16), wd)
    for k in range(TAIL_BLOCKS):
        rows = slice(k * tm, (k + 1) * tm)
        ys_ref[rows, :] += jnp.where(nt_ref[e] >= TAIL_BLOCKS - k, part[rows, :], 0.0)
    wgb_ref[0] = wg
    wub_ref[0] = wu
    wdb_ref[0] = wd


def _expert_tail_call(xs, tables, wg, wu, wd, tm, tf=256):
    d = xs.shape[1]
    n_experts, _, f = wg.shape

    def xs_spec(k):
        return pl.BlockSpec((tm, d), lambda e, j, tb, nt: (tb[TAIL_BLOCKS * e + k], 0),
                            pipeline_mode=pl.Buffered(1))

    w_in = lambda e, j, tb, nt: (e, 0, j)
    w_out = lambda e, j, tb, nt: (e, j, 0)
    grid_spec = pltpu.PrefetchScalarGridSpec(
        num_scalar_prefetch=2,
        grid=(n_experts, f // tf),
        in_specs=[xs_spec(k) for k in range(TAIL_BLOCKS)] + [
            pl.BlockSpec((1, d, tf), w_in),
            pl.BlockSpec((1, d, tf), w_in),
            pl.BlockSpec((1, tf, d), w_out),
        ],
        out_specs=[
            pl.BlockSpec((TAIL_BLOCKS * tm, d), lambda e, j, tb, nt: (e, 0)),
            pl.BlockSpec((1, d, tf), w_in),
            pl.BlockSpec((1, d, tf), w_in),
            pl.BlockSpec((1, tf, d), w_out),
        ],
        scratch_shapes=[pltpu.VMEM((TAIL_BLOCKS * tm, d), BF16)],
    )
    return pl.pallas_call(
        _expert_tail_kernel,
        grid_spec=grid_spec,
        out_shape=[
            jax.ShapeDtypeStruct((n_experts * TAIL_BLOCKS * tm, d), F32),
            jax.ShapeDtypeStruct(wg.shape, BF16),
            jax.ShapeDtypeStruct(wu.shape, BF16),
            jax.ShapeDtypeStruct(wd.shape, BF16),
        ],
        compiler_params=_params(("arbitrary", "arbitrary"), 58),
        name="moe_experts_tail",
    )(tables["tail_block"], tables["n_tail"], *([xs] * TAIL_BLOCKS), wg, wu, wd)


def _combine_kernel(dest_ref, dest_next_ref, ys_ref, x_ref, wt_ref, g_ref, o_ref, ybuf, sems):
    tm = x_ref.shape[0]
    i = pl.program_id(0)
    slot = i % 2

    def start_gathers(d_ref, dst_slot):
        def issue(r, c):
            for k in range(EXPERT_TOPK):
                d = d_ref[0, 0, EXPERT_TOPK * r + k]
                pltpu.make_async_copy(ys_ref.at[pl.ds(d, 1), :],
                                      ybuf.at[dst_slot, k, pl.ds(r, 1), :],
                                      sems.at[dst_slot]).start(priority=k % 2)
            return c

        lax.fori_loop(0, tm, issue, 0, unroll=ROW_DMA_UNROLL)

    @pl.when(i == 0)
    def _():
        start_gathers(dest_ref, 0)

    @pl.when(i + 1 < pl.num_programs(0))
    def _():
        start_gathers(dest_next_ref, 1 - slot)

    for k in range(EXPERT_TOPK):
        pltpu.make_async_copy(ys_ref.at[pl.ds(0, tm), :], ybuf.at[slot, k], sems.at[slot]).wait()
    wt = wt_ref[...]
    y = wt[:, 0:1] * ybuf[slot, 0]
    for k in range(1, EXPERT_TOPK):
        y = y + wt[:, k:k + 1] * ybuf[slot, k]
    o_ref[...] = x_ref[...] + g_ref[0] * y


def _combine_call(ys, dest, x, wt, modr, row, g_chunk, seq, tm=512):
    t, d = x.shape
    nt = t // tm
    rpb = seq // tm
    dest3 = dest.reshape(nt, 1, EXPERT_TOPK * tm)
    return pl.pallas_call(
        _combine_kernel,
        grid=(nt,),
        in_specs=[
            pl.BlockSpec((1, 1, EXPERT_TOPK * tm), lambda i: (i, 0, 0), memory_space=pltpu.SMEM),
            pl.BlockSpec((1, 1, EXPERT_TOPK * tm), lambda i: (jnp.minimum(i + 1, nt - 1), 0, 0),
                         memory_space=pltpu.SMEM),
            pl.BlockSpec(memory_space=pl.ANY),
            pl.BlockSpec((tm, d), lambda i: (i, 0)),
            pl.BlockSpec((tm, LANES), lambda i: (i, 0)),
            _mod_spec(d, rpb, row, g_chunk),
        ],
        out_specs=pl.BlockSpec((tm, d), lambda i: (i, 0)),
        out_shape=jax.ShapeDtypeStruct((t, d), F32),
        scratch_shapes=[pltpu.VMEM((2, EXPERT_TOPK, tm, d), F32), pltpu.SemaphoreType.DMA((2,))],
        compiler_params=_params(("arbitrary",), 40),
        name="moe_combine",
    )(dest3, dest3, ys, x, wt, modr)


def _routing_tables(top_idx, n_experts, tm, n_blocks):
    flat_e = top_idx.reshape(-1)
    onehot = (flat_e[:, None] == jnp.arange(n_experts, dtype=jnp.int32)[None, :]).astype(jnp.int32)
    csum = jnp.cumsum(onehot, axis=0)
    counts = csum[-1]
    pos = jnp.sum((csum - 1) * onehot, axis=1)
    nblk = (counts + tm - 1) // tm
    blk_end = jnp.cumsum(nblk)
    blk_start = blk_end - nblk
    dest = (blk_start[flat_e] * tm + pos).astype(jnp.int32)
    n_used = blk_end[-1]

    blocks = jnp.arange(n_blocks, dtype=jnp.int32)
    x_map = jnp.minimum(blocks, n_used - 1)
    blk_e = jnp.minimum(jnp.sum(blk_end[None, :] <= x_map[:, None], axis=1), n_experts - 1)
    from_end = blk_end[blk_e] - blocks
    is_tail = (from_end <= TAIL_BLOCKS) & (blocks < n_used)
    kind = jnp.where(blocks >= n_used, BLOCK_UNUSED,
                     jnp.where(is_tail, BLOCK_TAIL, BLOCK_COMPUTE))
    last_compute = lax.cummax(jnp.where(kind == BLOCK_COMPUTE, blocks, -1))
    w_expert = blk_e[jnp.maximum(last_compute, 0)]
    tail_map = lax.cummax(jnp.where(is_tail, TAIL_BLOCKS * (blk_e + 1) - from_end, 0))
    slot_from_end = TAIL_BLOCKS - jnp.arange(TAIL_BLOCKS, dtype=jnp.int32)
    tail_block = jnp.clip(blk_end[:, None] - slot_from_end[None, :], 0, n_blocks - 1)
    cand = jnp.concatenate([blk_end - 1, blocks])
    cand_ok = jnp.concatenate([nblk > 0, blocks >= n_used])
    zero_blocks = jnp.maximum(cand[jnp.argsort(~cand_ok, stable=True)], 0)
    i32 = lambda a: a.astype(jnp.int32)
    tables = dict(kind=i32(kind), x_map=i32(x_map), w_expert=i32(w_expert),
                  tail_map=i32(tail_map), tail_block=i32(tail_block.reshape(-1)),
                  n_tail=i32(jnp.minimum(nblk, TAIL_BLOCKS)),
                  zero_blocks=i32(zero_blocks), n_zero=i32(jnp.sum(cand_ok)).reshape(1))
    return dest, tables


def _rope_tables(seq, head_dim, scale=1.0):
    half = head_dim // 2
    inv_freq = jnp.exp(-math.log(ROPE_THETA) * jnp.arange(half, dtype=F32) / half)
    ang = jnp.arange(seq).astype(F32)[:, None] * inv_freq[None, :]
    cos, sin = jnp.cos(ang) * scale, jnp.sin(ang) * scale
    return (jnp.concatenate([cos, cos, cos, cos], axis=-1),
            jnp.concatenate([-sin, sin, -sin, sin], axis=-1))


def kernel(x, c, mod_w, mod_b, norm_mix, norm_ffn, conv_in, conv_w, conv_out, ffn_gate, ffn_up,
           ffn_down, qkv_w, q_norm, k_norm, attn_out, router_w, router_b, exp_gate, exp_up,
           exp_down):
    bsz, seq, d = x.shape
    depth = mod_w.shape[0]
    assert depth == 2, "layer 0 = short-conv + dense FFN, layer 1 = MoBA + MoE"
    t = bsz * seq
    head_dim = d // N_HEADS
    n_experts = router_w.shape[-1]

    mod = _mod_call(c, mod_w, mod_b)
    modr = mod.reshape(depth * bsz, 1, 6 * d)
    xt = x.reshape(t, d)

    row = bsz
    h = _normmod_call(xt, norm_mix[0], modr, 0, 1, 0, seq)
    bu = _convin_call(h, conv_in[0], conv_w[0], seq)
    xt, h = _resmm_call(bu, conv_out[0].astype(BF16), xt, modr, 0, 2, seq,
                        next_norm=(norm_ffn[0], 0, 4, 3))
    xt, h = _ffn_call(h, ffn_gate[0].astype(BF16), ffn_up[0].astype(BF16),
                      ffn_down[0].astype(BF16), xt, modr, 0, 5, seq,
                      next_norm=(norm_mix[1], row, 1, 0))

    w_qkv = qkv_w[0]
    cos_q, sin_q = _rope_tables(seq, head_dim, head_dim ** -0.5 * math.log2(math.e))
    cos_k, sin_k = _rope_tables(seq, head_dim)
    q = _qkv_call(h, w_qkv, 0, seq, head_dim, q_norm[0], cos_q, sin_q)
    k = _qkv_call(h, w_qkv, 1, seq, head_dim, k_norm[0], cos_k, sin_k)
    v = _qkv_call(h, w_qkv, 2, seq, head_dim)
    o = _attn_call(q, k, v, seq, head_dim)
    xt, hf, top_idx, top_w = _resmm_call(o, attn_out[0].astype(BF16), xt, modr, row, 2, seq,
                                         next_norm=(norm_ffn[1], row, 4, 3),
                                         router=(router_w[0], router_b[0]))
    n_blocks = (t * EXPERT_TOPK) // EXPERT_ROWS + n_experts
    dest, tables = _routing_tables(top_idx[:, :EXPERT_TOPK], n_experts, EXPERT_ROWS, n_blocks)
    xs = _dispatch_call(hf, dest, tables, n_blocks * EXPERT_ROWS, EXPERT_ROWS)
    y_tail, wg, wu, wd = _expert_tail_call(xs, tables, exp_gate[0], exp_up[0], exp_down[0],
                                           EXPERT_ROWS)
    ys = _expert_call(xs, tables, y_tail, wg, wu, wd, EXPERT_ROWS)
    xt = _combine_call(ys, dest, xt, top_w, modr, row, 5, seq)
    return xt.reshape(bsz, seq, d)
```

```python
import functools
import math

import jax
import jax.numpy as jnp
from jax import lax
from jax.experimental import pallas as pl
from jax.experimental.pallas import tpu as pltpu

N_HEADS = 16
CONV_WIDTH = 3
MOBA_BLOCK = 256
MOBA_TOPK = 3
ROPE_THETA = 10000.0
EXPERT_TOPK = 2
EXPERT_ROWS = 512
ROW_DMA_UNROLL = 8
NORM_EPS = 1e-6
NEG_INF = -1e30

LANES = 128
SUBLANES = 8
BF16_SUBLANES = 16
MIB = 1024 * 1024

F32 = jnp.float32
BF16 = jnp.bfloat16

_NT_DIMS = (((1,), (1,)), ((), ()))


def _params(semantics, vmem_mib):
    return pltpu.CompilerParams(dimension_semantics=semantics,
                                vmem_limit_bytes=vmem_mib * MIB)


def _dot(a, b):
    return jnp.dot(a, b, preferred_element_type=F32)


def _silu(a):
    return a * (1.0 / (1.0 + jnp.exp(-a)))


def _cast_weight_once(w_ref, w_scr):
    @pl.when(pl.program_id(1) == 0)
    def _():
        w_scr[...] = w_ref[...].astype(BF16)


def _mod_kernel(c_ref, w_ref, b_ref, o_ref):
    ca = _silu(c_ref[...])
    o_ref[0] = _dot(ca.astype(BF16), w_ref[0].astype(BF16)) + b_ref[0]


def _mod_call(c, mod_w, mod_b, tn=1024):
    depth, d, n = mod_w.shape
    bsz = c.shape[0]
    return pl.pallas_call(
        _mod_kernel,
        grid=(depth, n // tn),
        in_specs=[
            pl.BlockSpec((bsz, d), lambda l, j: (0, 0)),
            pl.BlockSpec((1, d, tn), lambda l, j: (l, 0, j)),
            pl.BlockSpec((1, 1, tn), lambda l, j: (l, 0, j)),
        ],
        out_specs=pl.BlockSpec((1, bsz, tn), lambda l, j: (l, 0, j)),
        out_shape=jax.ShapeDtypeStruct((depth, bsz, n), F32),
        compiler_params=_params(("arbitrary", "arbitrary"), 40),
        name="adaln_mod",
    )(c, mod_w, mod_b.reshape(depth, 1, n))


def _normmod(x, gain, scale, shift):
    ms = jnp.mean(x * x, axis=-1, keepdims=True)
    y = x * lax.rsqrt(ms + NORM_EPS) * gain
    return y * (1.0 + scale) + shift


def _normmod_kernel(x_ref, gain_ref, sc_ref, sh_ref, o_ref):
    o_ref[...] = _normmod(x_ref[...], gain_ref[...], sc_ref[0], sh_ref[0]).astype(o_ref.dtype)


def _mod_spec(d, rows_per_batch, row, chunk):
    return pl.BlockSpec((1, 1, d), lambda i, *_: (row + i // rows_per_batch, 0, chunk))


def _normmod_call(x, gain, modr, row, sc_chunk, sh_chunk, seq, tm=512):
    t, d = x.shape
    rpb = seq // tm
    return pl.pallas_call(
        _normmod_kernel,
        grid=(t // tm,),
        in_specs=[
            pl.BlockSpec((tm, d), lambda i: (i, 0)),
            pl.BlockSpec((1, d), lambda i: (0, 0)),
            _mod_spec(d, rpb, row, sc_chunk),
            _mod_spec(d, rpb, row, sh_chunk),
        ],
        out_specs=pl.BlockSpec((tm, d), lambda i: (i, 0)),
        out_shape=jax.ShapeDtypeStruct((t, d), BF16),
        compiler_params=_params(("arbitrary",), 32),
        name="normmod",
    )(x, gain.reshape(1, d), modr, modr)


def _convin_kernel(h_ref, wb_ref, wc_ref, wv_ref, cw_ref, o_ref, u_scr, wb_scr, wc_scr, wv_scr):
    seq = h_ref.shape[0]
    for w_ref, w_scr in ((wb_ref, wb_scr), (wc_ref, wc_scr), (wv_ref, wv_scr)):
        _cast_weight_once(w_ref, w_scr)
    h = h_ref[...]
    u = _dot(h, wc_scr[...]) * _dot(h, wv_scr[...])
    u_scr[0:SUBLANES, :] = jnp.zeros((SUBLANES, u.shape[1]), F32)
    u_scr[SUBLANES:, :] = u
    cw = cw_ref[...]
    conv = cw[CONV_WIDTH - 1:CONV_WIDTH, :] * u
    for tap in range(1, CONV_WIDTH):
        conv = conv + (cw[CONV_WIDTH - 1 - tap:CONV_WIDTH - tap, :]
                       * u_scr[SUBLANES - tap:SUBLANES - tap + seq, :])
    o_ref[...] = (_dot(h, wb_scr[...]) * conv).astype(o_ref.dtype)


def _convin_call(h, w_in, conv_w, seq, tn=256):
    t, d = h.shape
    nj = d // tn
    return pl.pallas_call(
        _convin_kernel,
        grid=(nj, t // seq),
        in_specs=[
            pl.BlockSpec((seq, d), lambda j, b: (b, 0)),
            pl.BlockSpec((d, tn), lambda j, b: (0, j)),
            pl.BlockSpec((d, tn), lambda j, b: (0, j + nj)),
            pl.BlockSpec((d, tn), lambda j, b: (0, j + 2 * nj)),
            pl.BlockSpec((CONV_WIDTH, tn), lambda j, b: (0, j)),
        ],
        out_specs=pl.BlockSpec((seq, tn), lambda j, b: (b, j)),
        out_shape=jax.ShapeDtypeStruct((t, d), BF16),
        scratch_shapes=[pltpu.VMEM((seq + SUBLANES, tn), F32)] + [pltpu.VMEM((d, tn), BF16)] * 3,
        compiler_params=_params(("arbitrary", "arbitrary"), 54),
        name="conv_in",
    )(h, w_in, w_in, w_in, conv_w)


def _resmm_norm_kernel(a_ref, w_ref, x_ref, g_ref, gain_ref, sc_ref, sh_ref, o_ref, h_ref):
    xn = x_ref[...] + g_ref[0] * _dot(a_ref[...], w_ref[...])
    o_ref[...] = xn
    h_ref[...] = _normmod(xn, gain_ref[...], sc_ref[0], sh_ref[0]).astype(h_ref.dtype)


def _resmm_route_kernel(a_ref, w_ref, x_ref, g_ref, gain_ref, sc_ref, sh_ref, wr_ref, br_ref,
                        o_ref, h_ref, idx_ref, wt_ref, *, n_experts):
    xn = x_ref[...] + g_ref[0] * _dot(a_ref[...], w_ref[...])
    o_ref[...] = xn
    h = _normmod(xn, gain_ref[...], sc_ref[0], sh_ref[0])
    h_ref[...] = h
    idx_ref[...], wt_ref[...] = _route(h, wr_ref[...], br_ref[...], n_experts)


def _resmm_call(a, w, x, modr, row, g_chunk, seq, next_norm, router=None, tm=512):
    t, k = a.shape
    d = w.shape[1]
    rpb = seq // tm
    gain, nrow, sc_chunk, sh_chunk = next_norm
    in_specs = [
        pl.BlockSpec((tm, k), lambda i: (i, 0)),
        pl.BlockSpec((k, d), lambda i: (0, 0)),
        pl.BlockSpec((tm, d), lambda i: (i, 0)),
        _mod_spec(d, rpb, row, g_chunk),
        pl.BlockSpec((1, d), lambda i: (0, 0)),
        _mod_spec(d, rpb, nrow, sc_chunk),
        _mod_spec(d, rpb, nrow, sh_chunk),
    ]
    args = (a, w, x, modr, gain.reshape(1, d), modr, modr)
    row_spec = pl.BlockSpec((tm, d), lambda i: (i, 0))
    lane_spec = pl.BlockSpec((tm, LANES), lambda i: (i, 0))
    cp = _params(("arbitrary",), 56)
    if router is None:
        return pl.pallas_call(
            _resmm_norm_kernel, grid=(t // tm,), in_specs=in_specs,
            out_specs=[row_spec, row_spec],
            out_shape=[jax.ShapeDtypeStruct((t, d), F32), jax.ShapeDtypeStruct((t, d), BF16)],
            compiler_params=cp, name="res_matmul_norm")(*args)
    w_router, b_router = router
    n_experts = w_router.shape[1]
    wr = jnp.zeros((d, LANES), F32).at[:, :n_experts].set(w_router)
    br = jnp.zeros((1, LANES), F32).at[0, :n_experts].set(b_router)
    in_specs += [pl.BlockSpec((d, LANES), lambda i: (0, 0)),
                 pl.BlockSpec((1, LANES), lambda i: (0, 0))]
    return pl.pallas_call(
        functools.partial(_resmm_route_kernel, n_experts=n_experts), grid=(t // tm,),
        in_specs=in_specs, out_specs=[row_spec, row_spec, lane_spec, lane_spec],
        out_shape=[jax.ShapeDtypeStruct((t, d), F32), jax.ShapeDtypeStruct((t, d), F32),
                   jax.ShapeDtypeStruct((t, LANES), jnp.int32),
                   jax.ShapeDtypeStruct((t, LANES), F32)],
        compiler_params=cp, name="res_matmul_route")(*args, wr, br)


def _ffn_kernel(h_ref, wg_ref, wu_ref, wd_ref, x_ref, g_ref, gain_ref, sc_ref, sh_ref,
                o_ref, hn_ref):
    j = pl.program_id(1)

    @pl.when(j == 0)
    def _():
        o_ref[...] = jnp.zeros(o_ref.shape, F32)

    h = h_ref[...]
    act = _silu(_dot(h, wg_ref[...])) * _dot(h, wu_ref[...])
    o_ref[...] += _dot(act.astype(BF16), wd_ref[...])

    @pl.when(j == pl.num_programs(1) - 1)
    def _():
        xn = x_ref[...] + g_ref[0] * o_ref[...]
        o_ref[...] = xn
        hn_ref[...] = _normmod(xn, gain_ref[...], sc_ref[0], sh_ref[0]).astype(hn_ref.dtype)


def _ffn_call(h, wg, wu, wd, x, modr, row, g_chunk, seq, next_norm, tm=512, tf=512):
    t, d = h.shape
    f = wg.shape[1]
    rpb = seq // tm
    gain, nrow, sc_chunk, sh_chunk = next_norm
    row_spec = pl.BlockSpec((tm, d), lambda i, j: (i, 0))
    return pl.pallas_call(
        _ffn_kernel,
        grid=(t // tm, f // tf),
        in_specs=[
            row_spec,
            pl.BlockSpec((d, tf), lambda i, j: (0, j)),
            pl.BlockSpec((d, tf), lambda i, j: (0, j)),
            pl.BlockSpec((tf, d), lambda i, j: (j, 0)),
            row_spec,
            _mod_spec(d, rpb, row, g_chunk),
            pl.BlockSpec((1, d), lambda i, j: (0, 0)),
            _mod_spec(d, rpb, nrow, sc_chunk),
            _mod_spec(d, rpb, nrow, sh_chunk),
        ],
        out_specs=[row_spec, row_spec],
        out_shape=[jax.ShapeDtypeStruct((t, d), F32), jax.ShapeDtypeStruct((t, d), BF16)],
        compiler_params=_params(("arbitrary", "arbitrary"), 52),
        name="dense_ffn",
    )(h, wg, wu, wd, x, modr, gain.reshape(1, d), modr, modr)


def _qk_kernel(h_ref, w_ref, gain_ref, cos_ref, sin_ref, ones_ref, perm_ref, o_ref, w_scr,
               *, head_dim):
    _cast_weight_once(w_ref, w_scr)
    acc = _dot(h_ref[...], w_scr[...])
    pair = ones_ref.shape[0]
    gain = gain_ref[...]
    cos = cos_ref[...]
    sin = sin_ref[...]
    ones = ones_ref[...]
    perm = perm_ref[...]
    for p0 in range(0, acc.shape[1], pair):
        xp = acc[:, p0:p0 + pair]
        ss = _dot((xp * xp).astype(BF16), ones)
        inv = lax.rsqrt(ss * (1.0 / head_dim) + NORM_EPS)
        z = xp * gain
        rot = _dot(z.astype(BF16), perm)
        out = (z * cos + rot * sin) * inv
        o_ref[:, p0:p0 + pair] = out.astype(o_ref.dtype)


def _v_kernel(h_ref, w_ref, o_ref, w_scr):
    _cast_weight_once(w_ref, w_scr)
    o_ref[...] = _dot(h_ref[...], w_scr[...]).astype(o_ref.dtype)


def _head_pair_matrices(head_dim):
    pair = 2 * head_dim
    src = jnp.arange(pair)[:, None]
    dst = jnp.arange(pair)[None, :]
    same_head = (src // head_dim) == (dst // head_dim)
    ones = same_head.astype(BF16)
    perm = (same_head & ((src % head_dim) == ((dst + head_dim // 2) % head_dim))).astype(BF16)
    return ones, perm


def _qkv_call(h, w_qkv, which, seq, head_dim, gain=None, cos=None, sin=None, tm=1024, tn=512):
    t, d = h.shape
    nj = d // tn
    off = which * nj
    rpb = seq // tm
    h_spec = pl.BlockSpec((tm, d), lambda j, i: (i, 0))
    w_spec = pl.BlockSpec((d, tn), lambda j, i: (0, j + off))
    out_spec = pl.BlockSpec((tm, tn), lambda j, i: (i, j))
    out_shape = jax.ShapeDtypeStruct((t, d), BF16)
    cp = _params(("arbitrary", "arbitrary"), 40)
    w_scratch = [pltpu.VMEM((d, tn), BF16)]
    if gain is None:
        return pl.pallas_call(_v_kernel, grid=(nj, t // tm), in_specs=[h_spec, w_spec],
                              out_specs=out_spec, out_shape=out_shape,
                              scratch_shapes=w_scratch, compiler_params=cp,
                              name="v_proj")(h, w_qkv)
    pair = 2 * head_dim
    ones, perm = _head_pair_matrices(head_dim)
    tab_spec = pl.BlockSpec((tm, pair), lambda j, i: (i % rpb, 0))
    const_spec = pl.BlockSpec((pair, pair), lambda j, i: (0, 0))
    return pl.pallas_call(
        functools.partial(_qk_kernel, head_dim=head_dim),
        grid=(nj, t // tm),
        in_specs=[h_spec, w_spec, pl.BlockSpec((1, pair), lambda j, i: (0, 0)),
                  tab_spec, tab_spec, const_spec, const_spec],
        out_specs=out_spec, out_shape=out_shape, scratch_shapes=w_scratch, compiler_params=cp,
        name="qk_proj",
    )(h, w_qkv, jnp.tile(gain.reshape(1, head_dim), (1, 2)), cos, sin, ones, perm)


def _attn_kernel(q_ref, k_ref, v_ref, o_ref, vt_scr, s_scr):
    seq, hd = q_ref.shape
    blk = MOBA_BLOCK
    nb = seq // blk

    rows = []
    for n in range(nb):
        vt_scr[n, 0:hd, :] = v_ref[n * blk:(n + 1) * blk, :].astype(F32).T.astype(BF16)
        vt_scr[n, hd:, :] = jnp.ones((vt_scr.shape[1] - hd, blk), BF16)
        kb = k_ref[n * blk:(n + 1) * blk, :].astype(F32)
        rows.append(jnp.sum(kb, axis=0, keepdims=True) * (1.0 / blk))
    km = jnp.concatenate(rows, axis=0)
    km_a = km.astype(BF16)
    rem = km - km_a.astype(F32)
    km_b = rem.astype(BF16)
    km_c = (rem - km_b.astype(F32)).astype(BF16)
    km_parts = jnp.concatenate([km_a, km_b, km_c, jnp.zeros_like(km_a)], axis=0)

    causal = (lax.broadcasted_iota(jnp.int32, (blk, blk), 0)
              <= lax.broadcasted_iota(jnp.int32, (blk, blk), 1))

    def masked_scores(qi):
        qb = q_ref[qi * blk:(qi + 1) * blk, :]
        sel = None
        if qi > MOBA_TOPK:
            g3 = lax.dot_general(km_parts, qb, _NT_DIMS, preferred_element_type=F32)
            gate = g3[0:nb] + g3[nb:2 * nb] + g3[2 * nb:3 * nb]
            g_row = [gate[n:n + 1, :] for n in range(qi)]
            rank = [jnp.zeros((1, blk), F32) for _ in range(qi)]
            for a in range(qi):
                for b in range(a + 1, qi):
                    a_first = (g_row[a] >= g_row[b]).astype(F32)
                    rank[b] = rank[b] + a_first
                    rank[a] = rank[a] + (1.0 - a_first)
            sel = [rank[n] < MOBA_TOPK for n in range(qi)]

        m_run = None
        for n in range(qi + 1):
            s = lax.dot_general(k_ref[n * blk:(n + 1) * blk, :], qb, _NT_DIMS,
                                preferred_element_type=F32)
            if n == qi:
                s = jnp.where(causal, s, NEG_INF)
            elif sel is not None:
                s = jnp.where(sel[n], s, NEG_INF)
            s_scr[qi % 2, n] = s
            cm = jnp.max(s, axis=0, keepdims=True)
            m_run = cm if m_run is None else jnp.maximum(m_run, cm)
        return m_run

    def weighted_values(qi, m_run):
        acc = None
        for n in range(qi + 1):
            p = jnp.exp2(s_scr[qi % 2, n] - m_run)
            pv = _dot(vt_scr[n], p.astype(BF16))
            acc = pv if acc is None else acc + pv
        out_t = acc[0:hd] * (1.0 / acc[hd:hd + 1])
        o_ref[qi * blk:(qi + 1) * blk, :] = out_t.T.astype(o_ref.dtype)

    m_next = masked_scores(0)
    for qi in range(nb):
        m_cur = m_next
        if qi + 1 < nb:
            m_next = masked_scores(qi + 1)
        weighted_values(qi, m_cur)


def _attn_call(q, k, v, seq, head_dim):
    t, d = q.shape
    nh = d // head_dim
    nb = seq // MOBA_BLOCK
    spec = pl.BlockSpec((seq, head_dim), lambda b, h: (b, h))
    return pl.pallas_call(
        _attn_kernel,
        grid=(t // seq, nh),
        in_specs=[spec, spec, spec],
        out_specs=spec,
        out_shape=jax.ShapeDtypeStruct((t, d), BF16),
        scratch_shapes=[pltpu.VMEM((nb, head_dim + BF16_SUBLANES, MOBA_BLOCK), BF16),
                        pltpu.VMEM((2, nb, MOBA_BLOCK, MOBA_BLOCK), F32)],
        compiler_params=_params(("arbitrary", "arbitrary"), 32),
        name="moba_attention",
    )(q, k, v)


def _route(h, w, bias, n_experts):
    h_a = h.astype(BF16)
    h_b = (h - h_a.astype(F32)).astype(BF16)
    w_a = w.astype(BF16)
    w_b = (w - w_a.astype(F32)).astype(BF16)
    logits = _dot(h_a, w_a) + _dot(h_b, w_a) + _dot(h_a, w_b) + bias
    lane = lax.broadcasted_iota(jnp.int32, logits.shape, 1)
    logits = jnp.where(lane < n_experts, logits, -jnp.inf)
    m1 = jnp.max(logits, axis=-1, keepdims=True)
    i1 = jnp.min(jnp.where(logits == m1, lane, LANES), axis=-1, keepdims=True)
    rest = jnp.where(lane == i1, -jnp.inf, logits)
    m2 = jnp.max(rest, axis=-1, keepdims=True)
    i2 = jnp.min(jnp.where(rest == m2, lane, LANES), axis=-1, keepdims=True)
    e2 = jnp.exp(m2 - m1)
    w1 = 1.0 / (1.0 + e2)
    w2 = e2 * w1
    return (jnp.where(lane == 0, i1, jnp.where(lane == 1, i2, 0)),
            jnp.where(lane == 0, w1, jnp.where(lane == 1, w2, 0.0)))


def _dispatch_kernel(zb_ref, nz_ref, dest_ref, h_ref, xs_ref, zbuf, sem, zsem):
    tm = h_ref.shape[0]
    zrows = zbuf.shape[0]

    @pl.when(pl.program_id(0) == 0)
    def _():
        zbuf[...] = jnp.zeros(zbuf.shape, zbuf.dtype)

        def zero_copy(k):
            row0 = pl.multiple_of(zb_ref[k] * zrows, zrows)
            return pltpu.make_async_copy(zbuf, xs_ref.at[pl.ds(row0, zrows), :], zsem)

        def start(k, c):
            zero_copy(k).start()
            return c

        def wait(k, c):
            zero_copy(k).wait()
            return c

        lax.fori_loop(0, nz_ref[0], start, 0)
        lax.fori_loop(0, nz_ref[0], wait, 0)

    def issue(r, c):
        for k in range(EXPERT_TOPK):
            d = dest_ref[0, 0, EXPERT_TOPK * r + k]
            pltpu.make_async_copy(h_ref.at[pl.ds(r, 1), :], xs_ref.at[pl.ds(d, 1), :],
                                  sem).start(priority=k % 2)
        return c

    lax.fori_loop(0, tm, issue, 0, unroll=ROW_DMA_UNROLL)
    for k in range(EXPERT_TOPK):
        pltpu.make_async_copy(h_ref, xs_ref.at[pl.ds(0, tm), :], sem).wait()


def _dispatch_call(h, dest, tables, n_rows, block_rows, tm=512):
    t, d = h.shape
    nt = t // tm
    grid_spec = pltpu.PrefetchScalarGridSpec(
        num_scalar_prefetch=2,
        grid=(nt,),
        in_specs=[
            pl.BlockSpec((1, 1, EXPERT_TOPK * tm), lambda i, zb, nz: (i, 0, 0),
                         memory_space=pltpu.SMEM),
            pl.BlockSpec((tm, d), lambda i, zb, nz: (i, 0)),
        ],
        out_specs=pl.BlockSpec(memory_space=pl.ANY),
        scratch_shapes=[pltpu.VMEM((block_rows, d), h.dtype), pltpu.SemaphoreType.DMA,
                        pltpu.SemaphoreType.DMA],
    )
    return pl.pallas_call(
        _dispatch_kernel,
        grid_spec=grid_spec,
        out_shape=jax.ShapeDtypeStruct((n_rows, d), h.dtype),
        compiler_params=_params(("arbitrary",), 32),
        name="moe_dispatch",
    )(tables["zero_blocks"], tables["n_zero"], dest.reshape(nt, 1, EXPERT_TOPK * tm), h)


BLOCK_COMPUTE, BLOCK_TAIL, BLOCK_UNUSED = 0, 1, 2


def _expert_kernel(kind_ref, xmap_ref, we_ref, tmap_ref, xs_ref, wg_ref, wu_ref, wd_ref,
                   ytail_ref, ys_ref, xb_scr):
    del xmap_ref, we_ref, tmap_ref
    i = pl.program_id(0)
    j = pl.program_id(1)
    kind = kind_ref[i]

    @pl.when((kind == BLOCK_UNUSED) & (j == 0))
    def _():
        ys_ref[...] = jnp.zeros(ys_ref.shape, F32)

    @pl.when((kind == BLOCK_TAIL) & (j == 0))
    def _():
        ys_ref[...] = ytail_ref[...]

    @pl.when(kind == BLOCK_COMPUTE)
    def _():
        @pl.when(j == 0)
        def _():
            xb_scr[...] = xs_ref[...].astype(BF16)
            ys_ref[...] = jnp.zeros(ys_ref.shape, F32)

        x = xb_scr[...]
        act = _silu(_dot(x, wg_ref[0])) * _dot(x, wu_ref[0])
        ys_ref[...] += _dot(act.astype(BF16), wd_ref[0])


def _expert_call(xs, tables, y_tail, wg, wu, wd, tm, tf=1024):
    n_rows, d = xs.shape
    f = wg.shape[2]
    nf = f // tf

    def wj(i, j, kind):
        return jnp.where(kind[i] == BLOCK_COMPUTE, j, nf - 1)

    grid_spec = pltpu.PrefetchScalarGridSpec(
        num_scalar_prefetch=4,
        grid=(n_rows // tm, nf),
        in_specs=[
            pl.BlockSpec((tm, d), lambda i, j, kind, xm, we, fm: (xm[i], 0)),
            pl.BlockSpec((1, d, tf), lambda i, j, kind, xm, we, fm: (we[i], 0, wj(i, j, kind))),
            pl.BlockSpec((1, d, tf), lambda i, j, kind, xm, we, fm: (we[i], 0, wj(i, j, kind))),
            pl.BlockSpec((1, tf, d), lambda i, j, kind, xm, we, fm: (we[i], wj(i, j, kind), 0)),
            pl.BlockSpec((tm, d), lambda i, j, kind, xm, we, fm: (fm[i], 0)),
        ],
        out_specs=pl.BlockSpec((tm, d), lambda i, j, kind, xm, we, fm: (i, 0)),
        scratch_shapes=[pltpu.VMEM((tm, d), BF16)],
    )
    return pl.pallas_call(
        _expert_kernel,
        grid_spec=grid_spec,
        out_shape=jax.ShapeDtypeStruct((n_rows, d), F32),
        compiler_params=_params(("arbitrary", "arbitrary"), 60),
        name="moe_experts",
    )(tables["kind"], tables["x_map"], tables["w_expert"], tables["tail_map"],
      xs, wg, wu, wd, y_tail)


TAIL_BLOCKS = 2


def _expert_tail_kernel(tb_ref, nt_ref, *refs):
    del tb_ref
    xs_refs = refs[:TAIL_BLOCKS]
    wg_ref, wu_ref, wd_ref, ys_ref, wgb_ref, wub_ref, wdb_ref, xb_scr = refs[TAIL_BLOCKS:]
    tm = xs_refs[0].shape[0]
    e = pl.program_id(0)
    j = pl.program_id(1)

    @pl.when(j == 0)
    def _():
        for k, xs_ref in enumerate(xs_refs):
            xb_scr[k * tm:(k + 1) * tm, :] = xs_ref[...].astype(BF16)
        ys_ref[...] = jnp.zeros(ys_ref.shape, F32)

    x = xb_scr[...]
    wg = wg_ref[0].astype(BF16)
    wu = wu_ref[0].astype(BF16)
    act = _silu(_dot(x, wg)) * _dot(x, wu)
    wd = wd_ref[0].astype(BF16)
    part = _dot(act.astype(BF16), wd)
    for k in range(TAIL_BLOCKS):
        rows = slice(k * tm, (k + 1) * tm)
        ys_ref[rows, :] += jnp.where(nt_ref[e] >= TAIL_BLOCKS - k, part[rows, :], 0.0)
    wgb_ref[0] = wg
    wub_ref[0] = wu
    wdb_ref[0] = wd


def _expert_tail_call(xs, tables, wg, wu, wd, tm, tf=256):
    d = xs.shape[1]
    n_experts, _, f = wg.shape

    def xs_spec(k):
        return pl.BlockSpec((tm, d), lambda e, j, tb, nt: (tb[TAIL_BLOCKS * e + k], 0))

    w_in = lambda e, j, tb, nt: (e, 0, j)
    w_out = lambda e, j, tb, nt: (e, j, 0)
    grid_spec = pltpu.PrefetchScalarGridSpec(
        num_scalar_prefetch=2,
        grid=(n_experts, f // tf),
        in_specs=[xs_spec(k) for k in range(TAIL_BLOCKS)] + [
            pl.BlockSpec((1, d, tf), w_in),
            pl.BlockSpec((1, d, tf), w_in),
            pl.BlockSpec((1, tf, d), w_out),
        ],
        out_specs=[
            pl.BlockSpec((TAIL_BLOCKS * tm, d), lambda e, j, tb, nt: (e, 0)),
            pl.BlockSpec((1, d, tf), w_in),
            pl.BlockSpec((1, d, tf), w_in),
            pl.BlockSpec((1, tf, d), w_out),
        ],
        scratch_shapes=[pltpu.VMEM((TAIL_BLOCKS * tm, d), BF16)],
    )
    return pl.pallas_call(
        _expert_tail_kernel,
        grid_spec=grid_spec,
        out_shape=[
            jax.ShapeDtypeStruct((n_experts * TAIL_BLOCKS * tm, d), F32),
            jax.ShapeDtypeStruct(wg.shape, BF16),
            jax.ShapeDtypeStruct(wu.shape, BF16),
            jax.ShapeDtypeStruct(wd.shape, BF16),
        ],
        compiler_params=_params(("arbitrary", "arbitrary"), 58),
        name="moe_experts_tail",
    )(tables["tail_block"], tables["n_tail"], *([xs] * TAIL_BLOCKS), wg, wu, wd)


def _combine_kernel(dest_ref, dest_next_ref, ys_ref, x_ref, wt_ref, g_ref, o_ref, ybuf, sems):
    tm = x_ref.shape[0]
    i = pl.program_id(0)
    slot = i % 2

    def start_gathers(d_ref, dst_slot):
        def issue(r, c):
            for k in range(EXPERT_TOPK):
                d = d_ref[0, 0, EXPERT_TOPK * r + k]
                pltpu.make_async_copy(ys_ref.at[pl.ds(d, 1), :],
                                      ybuf.at[dst_slot, k, pl.ds(r, 1), :],
                                      sems.at[dst_slot]).start(priority=k % 2)
            return c

        lax.fori_loop(0, tm, issue, 0, unroll=ROW_DMA_UNROLL)

    @pl.when(i == 0)
    def _():
        start_gathers(dest_ref, 0)

    @pl.when(i + 1 < pl.num_programs(0))
    def _():
        start_gathers(dest_next_ref, 1 - slot)

    for k in range(EXPERT_TOPK):
        pltpu.make_async_copy(ys_ref.at[pl.ds(0, tm), :], ybuf.at[slot, k], sems.at[slot]).wait()
    wt = wt_ref[...]
    y = wt[:, 0:1] * ybuf[slot, 0]
    for k in range(1, EXPERT_TOPK):
        y = y + wt[:, k:k + 1] * ybuf[slot, k]
    o_ref[...] = x_ref[...] + g_ref[0] * y


def _combine_call(ys, dest, x, wt, modr, row, g_chunk, seq, tm=512):
    t, d = x.shape
    nt = t // tm
    rpb = seq // tm
    dest3 = dest.reshape(nt, 1, EXPERT_TOPK * tm)
    return pl.pallas_call(
        _combine_kernel,
        grid=(nt,),
        in_specs=[
            pl.BlockSpec((1, 1, EXPERT_TOPK * tm), lambda i: (i, 0, 0), memory_space=pltpu.SMEM),
            pl.BlockSpec((1, 1, EXPERT_TOPK * tm), lambda i: (jnp.minimum(i + 1, nt - 1), 0, 0),
                         memory_space=pltpu.SMEM),
            pl.BlockSpec(memory_space=pl.ANY),
            pl.BlockSpec((tm, d), lambda i: (i, 0)),
            pl.BlockSpec((tm, LANES), lambda i: (i, 0)),
            _mod_spec(d, rpb, row, g_chunk),
        ],
        out_specs=pl.BlockSpec((tm, d), lambda i: (i, 0)),
        out_shape=jax.ShapeDtypeStruct((t, d), F32),
        scratch_shapes=[pltpu.VMEM((2, EXPERT_TOPK, tm, d), F32), pltpu.SemaphoreType.DMA((2,))],
        compiler_params=_params(("arbitrary",), 40),
        name="moe_combine",
    )(dest3, dest3, ys, x, wt, modr)


def _routing_tables(top_idx, n_experts, tm, n_blocks):
    flat_e = top_idx.reshape(-1)
    onehot = (flat_e[:, None] == jnp.arange(n_experts, dtype=jnp.int32)[None, :]).astype(jnp.int32)
    csum = jnp.cumsum(onehot, axis=0)
    counts = csum[-1]
    pos = jnp.sum((csum - 1) * onehot, axis=1)
    nblk = (counts + tm - 1) // tm
    blk_end = jnp.cumsum(nblk)
    blk_start = blk_end - nblk
    dest = (blk_start[flat_e] * tm + pos).astype(jnp.int32)
    n_used = blk_end[-1]

    blocks = jnp.arange(n_blocks, dtype=jnp.int32)
    x_map = jnp.minimum(blocks, n_used - 1)
    blk_e = jnp.minimum(jnp.sum(blk_end[None, :] <= x_map[:, None], axis=1), n_experts - 1)
    from_end = blk_end[blk_e] - blocks
    is_tail = (from_end <= TAIL_BLOCKS) & (blocks < n_used)
    kind = jnp.where(blocks >= n_used, BLOCK_UNUSED,
                     jnp.where(is_tail, BLOCK_TAIL, BLOCK_COMPUTE))
    last_compute = lax.cummax(jnp.where(kind == BLOCK_COMPUTE, blocks, -1))
    w_expert = blk_e[jnp.maximum(last_compute, 0)]
    tail_map = lax.cummax(jnp.where(is_tail, TAIL_BLOCKS * (blk_e + 1) - from_end, 0))
    slot_from_end = TAIL_BLOCKS - jnp.arange(TAIL_BLOCKS, dtype=jnp.int32)
    tail_block = jnp.clip(blk_end[:, None] - slot_from_end[None, :], 0, n_blocks - 1)
    cand = jnp.concatenate([blk_end - 1, blocks])
    cand_ok = jnp.concatenate([nblk > 0, blocks >= n_used])
    zero_blocks = jnp.maximum(cand[jnp.argsort(~cand_ok, stable=True)], 0)
    i32 = lambda a: a.astype(jnp.int32)
    tables = dict(kind=i32(kind), x_map=i32(x_map), w_expert=i32(w_expert),
                  tail_map=i32(tail_map), tail_block=i32(tail_block.reshape(-1)),
                  n_tail=i32(jnp.minimum(nblk, TAIL_BLOCKS)),
                  zero_blocks=i32(zero_blocks), n_zero=i32(jnp.sum(cand_ok)).reshape(1))
    return dest, tables


def _rope_tables(seq, head_dim, scale=1.0):
    half = head_dim // 2
    inv_freq = jnp.exp(-math.log(ROPE_THETA) * jnp.arange(half, dtype=F32) / half)
    ang = jnp.arange(seq).astype(F32)[:, None] * inv_freq[None, :]
    cos, sin = jnp.cos(ang) * scale, jnp.sin(ang) * scale
    return (jnp.concatenate([cos, cos, cos, cos], axis=-1),
            jnp.concatenate([-sin, sin, -sin, sin], axis=-1))


def kernel(x, c, mod_w, mod_b, norm_mix, norm_ffn, conv_in, conv_w, conv_out, ffn_gate, ffn_up,
           ffn_down, qkv_w, q_norm, k_norm, attn_out, router_w, router_b, exp_gate, exp_up,
           exp_down):
    bsz, seq, d = x.shape
    depth = mod_w.shape[0]
    assert depth == 2, "layer 0 = short-conv + dense FFN, layer 1 = MoBA + MoE"
    t = bsz * seq
    head_dim = d // N_HEADS
    n_experts = router_w.shape[-1]

    mod = _mod_call(c, mod_w, mod_b)
    modr = mod.reshape(depth * bsz, 1, 6 * d)
    xt = x.reshape(t, d)

    row = bsz
    h = _normmod_call(xt, norm_mix[0], modr, 0, 1, 0, seq)
    bu = _convin_call(h, conv_in[0], conv_w[0], seq)
    xt, h = _resmm_call(bu, conv_out[0].astype(BF16), xt, modr, 0, 2, seq,
                        next_norm=(norm_ffn[0], 0, 4, 3))
    xt, h = _ffn_call(h, ffn_gate[0].astype(BF16), ffn_up[0].astype(BF16),
                      ffn_down[0].astype(BF16), xt, modr, 0, 5, seq,
                      next_norm=(norm_mix[1], row, 1, 0))

    w_qkv = qkv_w[0]
    cos_q, sin_q = _rope_tables(seq, head_dim, head_dim ** -0.5 * math.log2(math.e))
    cos_k, sin_k = _rope_tables(seq, head_dim)
    q = _qkv_call(h, w_qkv, 0, seq, head_dim, q_norm[0], cos_q, sin_q)
    k = _qkv_call(h, w_qkv, 1, seq, head_dim, k_norm[0], cos_k, sin_k)
    v = _qkv_call(h, w_qkv, 2, seq, head_dim)
    o = _attn_call(q, k, v, seq, head_dim)
    xt, hf, top_idx, top_w = _resmm_call(o, attn_out[0].astype(BF16), xt, modr, row, 2, seq,
                                         next_norm=(norm_ffn[1], row, 4, 3),
                                         router=(router_w[0], router_b[0]))
    n_blocks = (t * EXPERT_TOPK) // EXPERT_ROWS + n_experts
    dest, tables = _routing_tables(top_idx[:, :EXPERT_TOPK], n_experts, EXPERT_ROWS, n_blocks)
    xs = _dispatch_call(hf, dest, tables, n_blocks * EXPERT_ROWS, EXPERT_ROWS)
    y_tail, wg, wu, wd = _expert_tail_call(xs, tables, exp_gate[0], exp_up[0], exp_down[0],
                                           EXPERT_ROWS)
    ys = _expert_call(xs, tables, y_tail, wg, wu, wd, EXPERT_ROWS)
    xt = _combine_call(ys, dest, xt, top_w, modr, row, 5, seq)
    return xt.reshape(bsz, seq, d)
```

```python
import functools
import math

import jax
import jax.numpy as jnp
from jax import lax
from jax.experimental import pallas as pl
from jax.experimental.pallas import tpu as pltpu

N_HEADS = 16
CONV_WIDTH = 3
MOBA_BLOCK = 256
MOBA_TOPK = 3
ATTN_HEADS_PER_STEP = 2
ROPE_THETA = 10000.0
EXPERT_TOPK = 2
EXPERT_ROWS = 512
ROW_DMA_UNROLL = 8
NORM_EPS = 1e-6
NEG_INF = -1e30

LANES = 128
SUBLANES = 8
BF16_SUBLANES = 16
MIB = 1024 * 1024

F32 = jnp.float32
BF16 = jnp.bfloat16

_NT_DIMS = (((1,), (1,)), ((), ()))


def _params(semantics, vmem_mib):
    return pltpu.CompilerParams(dimension_semantics=semantics,
                                vmem_limit_bytes=vmem_mib * MIB)


def _dot(a, b):
    return jnp.dot(a, b, preferred_element_type=F32)


def _silu(a):
    return a * (1.0 / (1.0 + jnp.exp(-a)))


def _cast_weight_once(w_ref, w_scr):
    @pl.when(pl.program_id(1) == 0)
    def _():
        w_scr[...] = w_ref[...].astype(BF16)


def _mod_kernel(c_ref, w_ref, b_ref, o_ref):
    ca = _silu(c_ref[...])
    o_ref[0] = _dot(ca.astype(BF16), w_ref[0].astype(BF16)) + b_ref[0]


def _mod_call(c, mod_w, mod_b, tn=1024):
    depth, d, n = mod_w.shape
    bsz = c.shape[0]
    return pl.pallas_call(
        _mod_kernel,
        grid=(depth, n // tn),
        in_specs=[
            pl.BlockSpec((bsz, d), lambda l, j: (0, 0)),
            pl.BlockSpec((1, d, tn), lambda l, j: (l, 0, j)),
            pl.BlockSpec((1, 1, tn), lambda l, j: (l, 0, j)),
        ],
        out_specs=pl.BlockSpec((1, bsz, tn), lambda l, j: (l, 0, j)),
        out_shape=jax.ShapeDtypeStruct((depth, bsz, n), F32),
        compiler_params=_params(("arbitrary", "arbitrary"), 40),
        name="adaln_mod",
    )(c, mod_w, mod_b.reshape(depth, 1, n))


def _normmod(x, gain, scale, shift):
    ms = jnp.mean(x * x, axis=-1, keepdims=True)
    y = x * lax.rsqrt(ms + NORM_EPS) * gain
    return y * (1.0 + scale) + shift


def _normmod_kernel(x_ref, gain_ref, sc_ref, sh_ref, o_ref):
    o_ref[...] = _normmod(x_ref[...], gain_ref[...], sc_ref[0], sh_ref[0]).astype(o_ref.dtype)


def _mod_spec(d, rows_per_batch, row, chunk):
    return pl.BlockSpec((1, 1, d), lambda i, *_: (row + i // rows_per_batch, 0, chunk))


def _normmod_call(x, gain, modr, row, sc_chunk, sh_chunk, seq, tm=512):
    t, d = x.shape
    rpb = seq // tm
    return pl.pallas_call(
        _normmod_kernel,
        grid=(t // tm,),
        in_specs=[
            pl.BlockSpec((tm, d), lambda i: (i, 0)),
            pl.BlockSpec((1, d), lambda i: (0, 0)),
            _mod_spec(d, rpb, row, sc_chunk),
            _mod_spec(d, rpb, row, sh_chunk),
        ],
        out_specs=pl.BlockSpec((tm, d), lambda i: (i, 0)),
        out_shape=jax.ShapeDtypeStruct((t, d), BF16),
        compiler_params=_params(("arbitrary",), 32),
        name="normmod",
    )(x, gain.reshape(1, d), modr, modr)


def _convin_kernel(h_ref, wb_ref, wc_ref, wv_ref, cw_ref, o_ref, u_scr, wb_scr, wc_scr, wv_scr):
    seq = h_ref.shape[0]
    for w_ref, w_scr in ((wb_ref, wb_scr), (wc_ref, wc_scr), (wv_ref, wv_scr)):
        _cast_weight_once(w_ref, w_scr)
    h = h_ref[...]
    u = _dot(h, wc_scr[...]) * _dot(h, wv_scr[...])
    u_scr[0:SUBLANES, :] = jnp.zeros((SUBLANES, u.shape[1]), F32)
    u_scr[SUBLANES:, :] = u
    cw = cw_ref[...]
    conv = cw[CONV_WIDTH - 1:CONV_WIDTH, :] * u
    for tap in range(1, CONV_WIDTH):
        conv = conv + (cw[CONV_WIDTH - 1 - tap:CONV_WIDTH - tap, :]
                       * u_scr[SUBLANES - tap:SUBLANES - tap + seq, :])
    o_ref[...] = (_dot(h, wb_scr[...]) * conv).astype(o_ref.dtype)


def _convin_call(h, w_in, conv_w, seq, tn=256):
    t, d = h.shape
    nj = d // tn
    return pl.pallas_call(
        _convin_kernel,
        grid=(nj, t // seq),
        in_specs=[
            pl.BlockSpec((seq, d), lambda j, b: (b, 0)),
            pl.BlockSpec((d, tn), lambda j, b: (0, j)),
            pl.BlockSpec((d, tn), lambda j, b: (0, j + nj)),
            pl.BlockSpec((d, tn), lambda j, b: (0, j + 2 * nj)),
            pl.BlockSpec((CONV_WIDTH, tn), lambda j, b: (0, j)),
        ],
        out_specs=pl.BlockSpec((seq, tn), lambda j, b: (b, j)),
        out_shape=jax.ShapeDtypeStruct((t, d), BF16),
        scratch_shapes=[pltpu.VMEM((seq + SUBLANES, tn), F32)] + [pltpu.VMEM((d, tn), BF16)] * 3,
        compiler_params=_params(("arbitrary", "arbitrary"), 54),
        name="conv_in",
    )(h, w_in, w_in, w_in, conv_w)


def _resmm_norm_kernel(a_ref, w_ref, x_ref, g_ref, gain_ref, sc_ref, sh_ref, o_ref, h_ref):
    xn = x_ref[...] + g_ref[0] * _dot(a_ref[...], w_ref[...])
    o_ref[...] = xn
    h_ref[...] = _normmod(xn, gain_ref[...], sc_ref[0], sh_ref[0]).astype(h_ref.dtype)


def _resmm_route_kernel(a_ref, w_ref, x_ref, g_ref, gain_ref, sc_ref, sh_ref, wr_ref, br_ref,
                        o_ref, h_ref, idx_ref, wt_ref, *, n_experts):
    xn = x_ref[...] + g_ref[0] * _dot(a_ref[...], w_ref[...])
    o_ref[...] = xn
    h = _normmod(xn, gain_ref[...], sc_ref[0], sh_ref[0])
    h_ref[...] = h
    idx_ref[...], wt_ref[...] = _route(h, wr_ref[...], br_ref[...], n_experts)


def _resmm_call(a, w, x, modr, row, g_chunk, seq, next_norm, router=None, tm=512):
    t, k = a.shape
    d = w.shape[1]
    rpb = seq // tm
    gain, nrow, sc_chunk, sh_chunk = next_norm
    in_specs = [
        pl.BlockSpec((tm, k), lambda i: (i, 0)),
        pl.BlockSpec((k, d), lambda i: (0, 0)),
        pl.BlockSpec((tm, d), lambda i: (i, 0)),
        _mod_spec(d, rpb, row, g_chunk),
        pl.BlockSpec((1, d), lambda i: (0, 0)),
        _mod_spec(d, rpb, nrow, sc_chunk),
        _mod_spec(d, rpb, nrow, sh_chunk),
    ]
    args = (a, w, x, modr, gain.reshape(1, d), modr, modr)
    row_spec = pl.BlockSpec((tm, d), lambda i: (i, 0))
    lane_spec = pl.BlockSpec((tm, LANES), lambda i: (i, 0))
    cp = _params(("arbitrary",), 56)
    if router is None:
        return pl.pallas_call(
            _resmm_norm_kernel, grid=(t // tm,), in_specs=in_specs,
            out_specs=[row_spec, row_spec],
            out_shape=[jax.ShapeDtypeStruct((t, d), F32), jax.ShapeDtypeStruct((t, d), BF16)],
            compiler_params=cp, name="res_matmul_norm")(*args)
    w_router, b_router = router
    n_experts = w_router.shape[1]
    wr = jnp.zeros((d, LANES), F32).at[:, :n_experts].set(w_router)
    br = jnp.zeros((1, LANES), F32).at[0, :n_experts].set(b_router)
    in_specs += [pl.BlockSpec((d, LANES), lambda i: (0, 0)),
                 pl.BlockSpec((1, LANES), lambda i: (0, 0))]
    return pl.pallas_call(
        functools.partial(_resmm_route_kernel, n_experts=n_experts), grid=(t // tm,),
        in_specs=in_specs, out_specs=[row_spec, row_spec, lane_spec, lane_spec],
        out_shape=[jax.ShapeDtypeStruct((t, d), F32), jax.ShapeDtypeStruct((t, d), F32),
                   jax.ShapeDtypeStruct((t, LANES), jnp.int32),
                   jax.ShapeDtypeStruct((t, LANES), F32)],
        compiler_params=cp, name="res_matmul_route")(*args, wr, br)


def _ffn_kernel(h_ref, wg_ref, wu_ref, wd_ref, x_ref, g_ref, gain_ref, sc_ref, sh_ref,
                o_ref, hn_ref):
    j = pl.program_id(1)

    @pl.when(j == 0)
    def _():
        o_ref[...] = jnp.zeros(o_ref.shape, F32)

    h = h_ref[...]
    act = _silu(_dot(h, wg_ref[...])) * _dot(h, wu_ref[...])
    o_ref[...] += _dot(act.astype(BF16), wd_ref[...])

    @pl.when(j == pl.num_programs(1) - 1)
    def _():
        xn = x_ref[...] + g_ref[0] * o_ref[...]
        o_ref[...] = xn
        hn_ref[...] = _normmod(xn, gain_ref[...], sc_ref[0], sh_ref[0]).astype(hn_ref.dtype)


def _ffn_call(h, wg, wu, wd, x, modr, row, g_chunk, seq, next_norm, tm=512, tf=512):
    t, d = h.shape
    f = wg.shape[1]
    rpb = seq // tm
    gain, nrow, sc_chunk, sh_chunk = next_norm
    row_spec = pl.BlockSpec((tm, d), lambda i, j: (i, 0))
    return pl.pallas_call(
        _ffn_kernel,
        grid=(t // tm, f // tf),
        in_specs=[
            row_spec,
            pl.BlockSpec((d, tf), lambda i, j: (0, j)),
            pl.BlockSpec((d, tf), lambda i, j: (0, j)),
            pl.BlockSpec((tf, d), lambda i, j: (j, 0)),
            row_spec,
            _mod_spec(d, rpb, row, g_chunk),
            pl.BlockSpec((1, d), lambda i, j: (0, 0)),
            _mod_spec(d, rpb, nrow, sc_chunk),
            _mod_spec(d, rpb, nrow, sh_chunk),
        ],
        out_specs=[row_spec, row_spec],
        out_shape=[jax.ShapeDtypeStruct((t, d), F32), jax.ShapeDtypeStruct((t, d), BF16)],
        compiler_params=_params(("arbitrary", "arbitrary"), 52),
        name="dense_ffn",
    )(h, wg, wu, wd, x, modr, gain.reshape(1, d), modr, modr)


def _qk_kernel(h_ref, w_ref, gain_ref, cos_ref, sin_ref, ones_ref, perm_ref, o_ref, w_scr,
               *, head_dim):
    _cast_weight_once(w_ref, w_scr)
    acc = _dot(h_ref[...], w_scr[...])
    pair = ones_ref.shape[0]
    gain = gain_ref[...]
    cos = cos_ref[...]
    sin = sin_ref[...]
    ones = ones_ref[...]
    perm = perm_ref[...]
    for p0 in range(0, acc.shape[1], pair):
        xp = acc[:, p0:p0 + pair]
        ss = _dot((xp * xp).astype(BF16), ones)
        inv = lax.rsqrt(ss * (1.0 / head_dim) + NORM_EPS)
        z = xp * gain
        rot = _dot(z.astype(BF16), perm)
        out = (z * cos + rot * sin) * inv
        o_ref[:, p0:p0 + pair] = out.astype(o_ref.dtype)


def _v_kernel(h_ref, w_ref, o_ref, w_scr):
    _cast_weight_once(w_ref, w_scr)
    o_ref[...] = _dot(h_ref[...], w_scr[...]).astype(o_ref.dtype)


def _head_pair_matrices(head_dim):
    pair = 2 * head_dim
    src = jnp.arange(pair)[:, None]
    dst = jnp.arange(pair)[None, :]
    same_head = (src // head_dim) == (dst // head_dim)
    ones = same_head.astype(BF16)
    perm = (same_head & ((src % head_dim) == ((dst + head_dim // 2) % head_dim))).astype(BF16)
    return ones, perm


def _qkv_call(h, w_qkv, which, seq, head_dim, gain=None, cos=None, sin=None, tm=2048, tn=512):
    t, d = h.shape
    nj = d // tn
    off = which * nj
    rpb = seq // tm
    h_spec = pl.BlockSpec((tm, d), lambda j, i: (i, 0))
    w_spec = pl.BlockSpec((d, tn), lambda j, i: (0, j + off))
    out_spec = pl.BlockSpec((tm, tn), lambda j, i: (i, j))
    out_shape = jax.ShapeDtypeStruct((t, d), BF16)
    cp = _params(("arbitrary", "arbitrary"), 40)
    w_scratch = [pltpu.VMEM((d, tn), BF16)]
    if gain is None:
        return pl.pallas_call(_v_kernel, grid=(nj, t // tm), in_specs=[h_spec, w_spec],
                              out_specs=out_spec, out_shape=out_shape,
                              scratch_shapes=w_scratch, compiler_params=cp,
                              name="v_proj")(h, w_qkv)
    pair = 2 * head_dim
    ones, perm = _head_pair_matrices(head_dim)
    tab_spec = pl.BlockSpec((tm, pair), lambda j, i: (i % rpb, 0))
    const_spec = pl.BlockSpec((pair, pair), lambda j, i: (0, 0))
    return pl.pallas_call(
        functools.partial(_qk_kernel, head_dim=head_dim),
        grid=(nj, t // tm),
        in_specs=[h_spec, w_spec, pl.BlockSpec((1, pair), lambda j, i: (0, 0)),
                  tab_spec, tab_spec, const_spec, const_spec],
        out_specs=out_spec, out_shape=out_shape, scratch_shapes=w_scratch, compiler_params=cp,
        name="qk_proj",
    )(h, w_qkv, jnp.tile(gain.reshape(1, head_dim), (1, 2)), cos, sin, ones, perm)


def _attn_kernel(q_ref, k_ref, v_ref, o_ref, vt_scr, s_scr):
    seq = q_ref.shape[0]
    hd = q_ref.shape[1] // ATTN_HEADS_PER_STEP
    blk = MOBA_BLOCK
    nb = seq // blk
    heads = range(ATTN_HEADS_PER_STEP)
    cols = [slice(hh * hd, (hh + 1) * hd) for hh in heads]

    km_parts = []
    for hh in heads:
        rows = []
        for n in range(nb):
            vt_scr[hh, n, 0:hd, :] = (v_ref[n * blk:(n + 1) * blk, cols[hh]]
                                      .astype(F32).T.astype(BF16))
            vt_scr[hh, n, hd:, :] = jnp.ones((vt_scr.shape[2] - hd, blk), BF16)
            kb = k_ref[n * blk:(n + 1) * blk, cols[hh]].astype(F32)
            rows.append(jnp.sum(kb, axis=0, keepdims=True) * (1.0 / blk))
        km = jnp.concatenate(rows, axis=0)
        km_a = km.astype(BF16)
        rem = km - km_a.astype(F32)
        km_b = rem.astype(BF16)
        km_c = (rem - km_b.astype(F32)).astype(BF16)
        km_parts.append(jnp.concatenate([km_a, km_b, km_c, jnp.zeros_like(km_a)], axis=0))

    causal = (lax.broadcasted_iota(jnp.int32, (blk, blk), 0)
              <= lax.broadcasted_iota(jnp.int32, (blk, blk), 1))

    def masked_scores(hh, qi):
        qb = q_ref[qi * blk:(qi + 1) * blk, cols[hh]]
        sel = None
        if qi > MOBA_TOPK:
            g3 = lax.dot_general(km_parts[hh], qb, _NT_DIMS, preferred_element_type=F32)
            gate = g3[0:nb] + g3[nb:2 * nb] + g3[2 * nb:3 * nb]
            g_row = [gate[n:n + 1, :] for n in range(qi)]
            rank = [jnp.zeros((1, blk), F32) for _ in range(qi)]
            for a in range(qi):
                for b in range(a + 1, qi):
                    a_first = (g_row[a] >= g_row[b]).astype(F32)
                    rank[b] = rank[b] + a_first
                    rank[a] = rank[a] + (1.0 - a_first)
            sel = [rank[n] < MOBA_TOPK for n in range(qi)]

        m_run = None
        for n in range(qi + 1):
            s = lax.dot_general(k_ref[n * blk:(n + 1) * blk, cols[hh]], qb, _NT_DIMS,
                                preferred_element_type=F32)
            if n == qi:
                s = jnp.where(causal, s, NEG_INF)
            elif sel is not None:
                s = jnp.where(sel[n], s, NEG_INF)
            s_scr[hh, qi % 2, n] = s
            cm = jnp.max(s, axis=0, keepdims=True)
            m_run = cm if m_run is None else jnp.maximum(m_run, cm)
        return m_run

    def weighted_values(hh, qi, m_run):
        acc = None
        for n in range(qi + 1):
            p = jnp.exp2(s_scr[hh, qi % 2, n] - m_run)
            pv = _dot(vt_scr[hh, n], p.astype(BF16))
            acc = pv if acc is None else acc + pv
        out_t = acc[0:hd] * (1.0 / acc[hd:hd + 1])
        o_ref[qi * blk:(qi + 1) * blk, cols[hh]] = out_t.T.astype(o_ref.dtype)

    m_next = [masked_scores(hh, 0) for hh in heads]
    for qi in range(nb):
        m_cur = m_next
        if qi + 1 < nb:
            m_next = [masked_scores(hh, qi + 1) for hh in heads]
        for hh in heads:
            weighted_values(hh, qi, m_cur[hh])


def _attn_call(q, k, v, seq, head_dim):
    t, d = q.shape
    width = ATTN_HEADS_PER_STEP * head_dim
    nb = seq // MOBA_BLOCK
    spec = pl.BlockSpec((seq, width), lambda b, h: (b, h))
    return pl.pallas_call(
        _attn_kernel,
        grid=(t // seq, d // width),
        in_specs=[spec, spec, spec],
        out_specs=spec,
        out_shape=jax.ShapeDtypeStruct((t, d), BF16),
        scratch_shapes=[
            pltpu.VMEM((ATTN_HEADS_PER_STEP, nb, head_dim + BF16_SUBLANES, MOBA_BLOCK), BF16),
            pltpu.VMEM((ATTN_HEADS_PER_STEP, 2, nb, MOBA_BLOCK, MOBA_BLOCK), F32)],
        compiler_params=_params(("arbitrary", "arbitrary"), 40),
        name="moba_attention",
    )(q, k, v)


def _route(h, w, bias, n_experts):
    h_a = h.astype(BF16)
    h_b = (h - h_a.astype(F32)).astype(BF16)
    w_a = w.astype(BF16)
    w_b = (w - w_a.astype(F32)).astype(BF16)
    logits = _dot(h_a, w_a) + _dot(h_b, w_a) + _dot(h_a, w_b) + bias
    lane = lax.broadcasted_iota(jnp.int32, logits.shape, 1)
    logits = jnp.where(lane < n_experts, logits, -jnp.inf)
    m1 = jnp.max(logits, axis=-1, keepdims=True)
    i1 = jnp.min(jnp.where(logits == m1, lane, LANES), axis=-1, keepdims=True)
    rest = jnp.where(lane == i1, -jnp.inf, logits)
    m2 = jnp.max(rest, axis=-1, keepdims=True)
    i2 = jnp.min(jnp.where(rest == m2, lane, LANES), axis=-1, keepdims=True)
    e2 = jnp.exp(m2 - m1)
    w1 = 1.0 / (1.0 + e2)
    w2 = e2 * w1
    return (jnp.where(lane == 0, i1, jnp.where(lane == 1, i2, 0)),
            jnp.where(lane == 0, w1, jnp.where(lane == 1, w2, 0.0)))


def _dispatch_kernel(zb_ref, nz_ref, dest_ref, h_ref, xs_ref, zbuf, sem, zsem):
    tm = h_ref.shape[0]
    zrows = zbuf.shape[0]

    @pl.when(pl.program_id(0) == 0)
    def _():
        zbuf[...] = jnp.zeros(zbuf.shape, zbuf.dtype)

        def zero_copy(k):
            row0 = pl.multiple_of(zb_ref[k] * zrows, zrows)
            return pltpu.make_async_copy(zbuf, xs_ref.at[pl.ds(row0, zrows), :], zsem)

        def start(k, c):
            zero_copy(k).start()
            return c

        def wait(k, c):
            zero_copy(k).wait()
            return c

        lax.fori_loop(0, nz_ref[0], start, 0)
        lax.fori_loop(0, nz_ref[0], wait, 0)

    def issue(r, c):
        for k in range(EXPERT_TOPK):
            d = dest_ref[0, 0, EXPERT_TOPK * r + k]
            pltpu.make_async_copy(h_ref.at[pl.ds(r, 1), :], xs_ref.at[pl.ds(d, 1), :],
                                  sem).start(priority=k % 2)
        return c

    lax.fori_loop(0, tm, issue, 0, unroll=ROW_DMA_UNROLL)
    for k in range(EXPERT_TOPK):
        pltpu.make_async_copy(h_ref, xs_ref.at[pl.ds(0, tm), :], sem).wait()


def _dispatch_call(h, dest, tables, n_rows, block_rows, tm=512):
    t, d = h.shape
    nt = t // tm
    grid_spec = pltpu.PrefetchScalarGridSpec(
        num_scalar_prefetch=2,
        grid=(nt,),
        in_specs=[
            pl.BlockSpec((1, 1, EXPERT_TOPK * tm), lambda i, zb, nz: (i, 0, 0),
                         memory_space=pltpu.SMEM),
            pl.BlockSpec((tm, d), lambda i, zb, nz: (i, 0)),
        ],
        out_specs=pl.BlockSpec(memory_space=pl.ANY),
        scratch_shapes=[pltpu.VMEM((block_rows, d), h.dtype), pltpu.SemaphoreType.DMA,
                        pltpu.SemaphoreType.DMA],
    )
    return pl.pallas_call(
        _dispatch_kernel,
        grid_spec=grid_spec,
        out_shape=jax.ShapeDtypeStruct((n_rows, d), h.dtype),
        compiler_params=_params(("arbitrary",), 32),
        name="moe_dispatch",
    )(tables["zero_blocks"], tables["n_zero"], dest.reshape(nt, 1, EXPERT_TOPK * tm), h)


BLOCK_COMPUTE, BLOCK_TAIL, BLOCK_UNUSED = 0, 1, 2


def _expert_kernel(kind_ref, xmap_ref, we_ref, tmap_ref, xs_ref, wg_ref, wu_ref, wd_ref,
                   ytail_ref, ys_ref, xb_scr):
    del xmap_ref, we_ref, tmap_ref
    i = pl.program_id(0)
    j = pl.program_id(1)
    kind = kind_ref[i]

    @pl.when((kind == BLOCK_UNUSED) & (j == 0))
    def _():
        ys_ref[...] = jnp.zeros(ys_ref.shape, F32)

    @pl.when((kind == BLOCK_TAIL) & (j == 0))
    def _():
        ys_ref[...] = ytail_ref[...]

    @pl.when(kind == BLOCK_COMPUTE)
    def _():
        @pl.when(j == 0)
        def _():
            xb_scr[...] = xs_ref[...].astype(BF16)
            ys_ref[...] = jnp.zeros(ys_ref.shape, F32)

        x = xb_scr[...]
        act = _silu(_dot(x, wg_ref[0])) * _dot(x, wu_ref[0])
        ys_ref[...] += _dot(act.astype(BF16), wd_ref[0])


def _expert_call(xs, tables, y_tail, wg, wu, wd, tm, tf=1024):
    n_rows, d = xs.shape
    f = wg.shape[2]
    nf = f // tf

    def wj(i, j, kind):
        return jnp.where(kind[i] == BLOCK_COMPUTE, j, nf - 1)

    grid_spec = pltpu.PrefetchScalarGridSpec(
        num_scalar_prefetch=4,
        grid=(n_rows // tm, nf),
        in_specs=[
            pl.BlockSpec((tm, d), lambda i, j, kind, xm, we, fm: (xm[i], 0)),
            pl.BlockSpec((1, d, tf), lambda i, j, kind, xm, we, fm: (we[i], 0, wj(i, j, kind))),
            pl.BlockSpec((1, d, tf), lambda i, j, kind, xm, we, fm: (we[i], 0, wj(i, j, kind))),
            pl.BlockSpec((1, tf, d), lambda i, j, kind, xm, we, fm: (we[i], wj(i, j, kind), 0)),
            pl.BlockSpec((tm, d), lambda i, j, kind, xm, we, fm: (fm[i], 0)),
        ],
        out_specs=pl.BlockSpec((tm, d), lambda i, j, kind, xm, we, fm: (i, 0)),
        scratch_shapes=[pltpu.VMEM((tm, d), BF16)],
    )
    return pl.pallas_call(
        _expert_kernel,
        grid_spec=grid_spec,
        out_shape=jax.ShapeDtypeStruct((n_rows, d), F32),
        compiler_params=_params(("arbitrary", "arbitrary"), 60),
        name="moe_experts",
    )(tables["kind"], tables["x_map"], tables["w_expert"], tables["tail_map"],
      xs, wg, wu, wd, y_tail)


TAIL_BLOCKS = 2


def _expert_tail_kernel(tb_ref, nt_ref, *refs):
    del tb_ref
    xs_refs = refs[:TAIL_BLOCKS]
    wg_ref, wu_ref, wd_ref, ys_ref, wgb_ref, wub_ref, wdb_ref, xb_scr = refs[TAIL_BLOCKS:]
    tm = xs_refs[0].shape[0]
    e = pl.program_id(0)
    j = pl.program_id(1)

    @pl.when(j == 0)
    def _():
        for k, xs_ref in enumerate(xs_refs):
            xb_scr[k * tm:(k + 1) * tm, :] = xs_ref[...].astype(BF16)
        ys_ref[...] = jnp.zeros(ys_ref.shape, F32)

    x = xb_scr[...]
    wg = wg_ref[0].astype(BF16)
    wu = wu_ref[0].astype(BF16)
    act = _silu(_dot(x, wg)) * _dot(x, wu)
    wd = wd_ref[0].astype(BF16)
    part = _dot(act.astype(BF16), wd)
    for k in range(TAIL_BLOCKS):
        rows = slice(k * tm, (k + 1) * tm)
        ys_ref[rows, :] += jnp.where(nt_ref[e] >= TAIL_BLOCKS - k, part[rows, :], 0.0)
    wgb_ref[0] = wg
    wub_ref[0] = wu
    wdb_ref[0] = wd


def _expert_tail_call(xs, tables, wg, wu, wd, tm, tf=256):
    d = xs.shape[1]
    n_experts, _, f = wg.shape

    def xs_spec(k):
        return pl.BlockSpec((tm, d), lambda e, j, tb, nt: (tb[TAIL_BLOCKS * e + k], 0))

    w_in = lambda e, j, tb, nt: (e, 0, j)
    w_out = lambda e, j, tb, nt: (e, j, 0)
    grid_spec = pltpu.PrefetchScalarGridSpec(
        num_scalar_prefetch=2,
        grid=(n_experts, f // tf),
        in_specs=[xs_spec(k) for k in range(TAIL_BLOCKS)] + [
            pl.BlockSpec((1, d, tf), w_in),
            pl.BlockSpec((1, d, tf), w_in),
            pl.BlockSpec((1, tf, d), w_out),
        ],
        out_specs=[
            pl.BlockSpec((TAIL_BLOCKS * tm, d), lambda e, j, tb, nt: (e, 0)),
            pl.BlockSpec((1, d, tf), w_in),
            pl.BlockSpec((1, d, tf), w_in),
            pl.BlockSpec((1, tf, d), w_out),
        ],
        scratch_shapes=[pltpu.VMEM((TAIL_BLOCKS * tm, d), BF16)],
    )
    return pl.pallas_call(
        _expert_tail_kernel,
        grid_spec=grid_spec,
        out_shape=[
            jax.ShapeDtypeStruct((n_experts * TAIL_BLOCKS * tm, d), F32),
            jax.ShapeDtypeStruct(wg.shape, BF16),
            jax.ShapeDtypeStruct(wu.shape, BF16),
            jax.ShapeDtypeStruct(wd.shape, BF16),
        ],
        compiler_params=_params(("arbitrary", "arbitrary"), 58),
        name="moe_experts_tail",
    )(tables["tail_block"], tables["n_tail"], *([xs] * TAIL_BLOCKS), wg, wu, wd)


def _combine_kernel(dest_ref, dest_next_ref, ys_ref, x_ref, wt_ref, g_ref, o_ref, ybuf, sems):
    tm = x_ref.shape[0]
    i = pl.program_id(0)
    slot = i % 2

    def start_gathers(d_ref, dst_slot):
        def issue(r, c):
            for k in range(EXPERT_TOPK):
                d = d_ref[0, 0, EXPERT_TOPK * r + k]
                pltpu.make_async_copy(ys_ref.at[pl.ds(d, 1), :],
                                      ybuf.at[dst_slot, k, pl.ds(r, 1), :],
                                      sems.at[dst_slot]).start(priority=k % 2)
            return c

        lax.fori_loop(0, tm, issue, 0, unroll=ROW_DMA_UNROLL)

    @pl.when(i == 0)
    def _():
        start_gathers(dest_ref, 0)

    @pl.when(i + 1 < pl.num_programs(0))
    def _():
        start_gathers(dest_next_ref, 1 - slot)

    for k in range(EXPERT_TOPK):
        pltpu.make_async_copy(ys_ref.at[pl.ds(0, tm), :], ybuf.at[slot, k], sems.at[slot]).wait()
    wt = wt_ref[...]
    y = wt[:, 0:1] * ybuf[slot, 0]
    for k in range(1, EXPERT_TOPK):
        y = y + wt[:, k:k + 1] * ybuf[slot, k]
    o_ref[...] = x_ref[...] + g_ref[0] * y


def _combine_call(ys, dest, x, wt, modr, row, g_chunk, seq, tm=512):
    t, d = x.shape
    nt = t // tm
    rpb = seq // tm
    dest3 = dest.reshape(nt, 1, EXPERT_TOPK * tm)
    return pl.pallas_call(
        _combine_kernel,
        grid=(nt,),
        in_specs=[
            pl.BlockSpec((1, 1, EXPERT_TOPK * tm), lambda i: (i, 0, 0), memory_space=pltpu.SMEM),
            pl.BlockSpec((1, 1, EXPERT_TOPK * tm), lambda i: (jnp.minimum(i + 1, nt - 1), 0, 0),
                         memory_space=pltpu.SMEM),
            pl.BlockSpec(memory_space=pl.ANY),
            pl.BlockSpec((tm, d), lambda i: (i, 0)),
            pl.BlockSpec((tm, LANES), lambda i: (i, 0)),
            _mod_spec(d, rpb, row, g_chunk),
        ],
        out_specs=pl.BlockSpec((tm, d), lambda i: (i, 0)),
        out_shape=jax.ShapeDtypeStruct((t, d), F32),
        scratch_shapes=[pltpu.VMEM((2, EXPERT_TOPK, tm, d), F32), pltpu.SemaphoreType.DMA((2,))],
        compiler_params=_params(("arbitrary",), 40),
        name="moe_combine",
    )(dest3, dest3, ys, x, wt, modr)


def _routing_tables(top_idx, n_experts, tm, n_blocks):
    flat_e = top_idx.reshape(-1)
    onehot = (flat_e[:, None] == jnp.arange(n_experts, dtype=jnp.int32)[None, :]).astype(jnp.int32)
    csum = jnp.cumsum(onehot, axis=0)
    counts = csum[-1]
    pos = jnp.sum((csum - 1) * onehot, axis=1)
    nblk = (counts + tm - 1) // tm
    blk_end = jnp.cumsum(nblk)
    blk_start = blk_end - nblk
    dest = (blk_start[flat_e] * tm + pos).astype(jnp.int32)
    n_used = blk_end[-1]

    blocks = jnp.arange(n_blocks, dtype=jnp.int32)
    x_map = jnp.minimum(blocks, n_used - 1)
    blk_e = jnp.minimum(jnp.sum(blk_end[None, :] <= x_map[:, None], axis=1), n_experts - 1)
    from_end = blk_end[blk_e] - blocks
    is_tail = (from_end <= TAIL_BLOCKS) & (blocks < n_used)
    kind = jnp.where(blocks >= n_used, BLOCK_UNUSED,
                     jnp.where(is_tail, BLOCK_TAIL, BLOCK_COMPUTE))
    last_compute = lax.cummax(jnp.where(kind == BLOCK_COMPUTE, blocks, -1))
    w_expert = blk_e[jnp.maximum(last_compute, 0)]
    tail_map = lax.cummax(jnp.where(is_tail, TAIL_BLOCKS * (blk_e + 1) - from_end, 0))
    slot_from_end = TAIL_BLOCKS - jnp.arange(TAIL_BLOCKS, dtype=jnp.int32)
    tail_block = jnp.clip(blk_end[:, None] - slot_from_end[None, :], 0, n_blocks - 1)
    cand = jnp.concatenate([blk_end - 1, blocks])
    cand_ok = jnp.concatenate([nblk > 0, blocks >= n_used])
    zero_blocks = jnp.maximum(cand[jnp.argsort(~cand_ok, stable=True)], 0)
    i32 = lambda a: a.astype(jnp.int32)
    tables = dict(kind=i32(kind), x_map=i32(x_map), w_expert=i32(w_expert),
                  tail_map=i32(tail_map), tail_block=i32(tail_block.reshape(-1)),
                  n_tail=i32(jnp.minimum(nblk, TAIL_BLOCKS)),
                  zero_blocks=i32(zero_blocks), n_zero=i32(jnp.sum(cand_ok)).reshape(1))
    return dest, tables


def _rope_tables(seq, head_dim, scale=1.0):
    half = head_dim // 2
    inv_freq = jnp.exp(-math.log(ROPE_THETA) * jnp.arange(half, dtype=F32) / half)
    ang = jnp.arange(seq).astype(F32)[:, None] * inv_freq[None, :]
    cos, sin = jnp.cos(ang) * scale, jnp.sin(ang) * scale
    return (jnp.concatenate([cos, cos, cos, cos], axis=-1),
            jnp.concatenate([-sin, sin, -sin, sin], axis=-1))


def kernel(x, c, mod_w, mod_b, norm_mix, norm_ffn, conv_in, conv_w, conv_out, ffn_gate, ffn_up,
           ffn_down, qkv_w, q_norm, k_norm, attn_out, router_w, router_b, exp_gate, exp_up,
           exp_down):
    bsz, seq, d = x.shape
    depth = mod_w.shape[0]
    assert depth == 2, "layer 0 = short-conv + dense FFN, layer 1 = MoBA + MoE"
    t = bsz * seq
    head_dim = d // N_HEADS
    n_experts = router_w.shape[-1]

    mod = _mod_call(c, mod_w, mod_b)
    modr = mod.reshape(depth * bsz, 1, 6 * d)
    xt = x.reshape(t, d)

    row = bsz
    h = _normmod_call(xt, norm_mix[0], modr, 0, 1, 0, seq)
    bu = _convin_call(h, conv_in[0], conv_w[0], seq)
    xt, h = _resmm_call(bu, conv_out[0].astype(BF16), xt, modr, 0, 2, seq,
                        next_norm=(norm_ffn[0], 0, 4, 3))
    xt, h = _ffn_call(h, ffn_gate[0].astype(BF16), ffn_up[0].astype(BF16),
                      ffn_down[0].astype(BF16), xt, modr, 0, 5, seq,
                      next_norm=(norm_mix[1], row, 1, 0))

    w_qkv = qkv_w[0]
    cos_q, sin_q = _rope_tables(seq, head_dim, head_dim ** -0.5 * math.log2(math.e))
    cos_k, sin_k = _rope_tables(seq, head_dim)
    q = _qkv_call(h, w_qkv, 0, seq, head_dim, q_norm[0], cos_q, sin_q)
    k = _qkv_call(h, w_qkv, 1, seq, head_dim, k_norm[0], cos_k, sin_k)
    v = _qkv_call(h, w_qkv, 2, seq, head_dim)
    o = _attn_call(q, k, v, seq, head_dim)
    xt, hf, top_idx, top_w = _resmm_call(o, attn_out[0].astype(BF16), xt, modr, row, 2, seq,
                                         next_norm=(norm_ffn[1], row, 4, 3),
                                         router=(router_w[0], router_b[0]))
    n_blocks = (t * EXPERT_TOPK) // EXPERT_ROWS + n_experts
    dest, tables = _routing_tables(top_idx[:, :EXPERT_TOPK], n_experts, EXPERT_ROWS, n_blocks)
    xs = _dispatch_call(hf, dest, tables, n_blocks * EXPERT_ROWS, EXPERT_ROWS)
    y_tail, wg, wu, wd = _expert_tail_call(xs, tables, exp_gate[0], exp_up[0], exp_down[0],
                                           EXPERT_ROWS)
    ys = _expert_call(xs, tables, y_tail, wg, wu, wd, EXPERT_ROWS)
    xt = _combine_call(ys, dest, xt, top_w, modr, row, 5, seq)
    return xt.reshape(bsz, seq, d)
```

```python
import functools
import math

import jax
import jax.numpy as jnp
from jax import lax
from jax.experimental import pallas as pl
from jax.experimental.pallas import tpu as pltpu

N_HEADS = 16
CONV_WIDTH = 3
MOBA_BLOCK = 256
MOBA_TOPK = 3
ATTN_HEADS_PER_STEP = 2
ROPE_THETA = 10000.0
EXPERT_TOPK = 2
EXPERT_ROWS = 512
ROW_DMA_UNROLL = 8
NORM_EPS = 1e-6
NEG_INF = -1e30

LANES = 128
SUBLANES = 8
BF16_SUBLANES = 16
MIB = 1024 * 1024

F32 = jnp.float32
BF16 = jnp.bfloat16

_NT_DIMS = (((1,), (1,)), ((), ()))


def _params(semantics, vmem_mib):
    return pltpu.CompilerParams(dimension_semantics=semantics,
                                vmem_limit_bytes=vmem_mib * MIB)


def _dot(a, b):
    return jnp.dot(a, b, preferred_element_type=F32)


def _silu(a):
    return a * (1.0 / (1.0 + jnp.exp(-a)))


def _cast_weight_once(w_ref, w_scr):
    @pl.when(pl.program_id(1) == 0)
    def _():
        w_scr[...] = w_ref[...].astype(BF16)


def _mod_kernel(c_ref, w_ref, b_ref, o_ref):
    ca = _silu(c_ref[...])
    o_ref[0] = _dot(ca.astype(BF16), w_ref[0].astype(BF16)) + b_ref[0]


def _mod_call(c, mod_w, mod_b, tn=1024):
    depth, d, n = mod_w.shape
    bsz = c.shape[0]
    return pl.pallas_call(
        _mod_kernel,
        grid=(depth, n // tn),
        in_specs=[
            pl.BlockSpec((bsz, d), lambda l, j: (0, 0)),
            pl.BlockSpec((1, d, tn), lambda l, j: (l, 0, j)),
            pl.BlockSpec((1, 1, tn), lambda l, j: (l, 0, j)),
        ],
        out_specs=pl.BlockSpec((1, bsz, tn), lambda l, j: (l, 0, j)),
        out_shape=jax.ShapeDtypeStruct((depth, bsz, n), F32),
        compiler_params=_params(("arbitrary", "arbitrary"), 40),
        name="adaln_mod",
    )(c, mod_w, mod_b.reshape(depth, 1, n))


def _normmod(x, gain, scale, shift):
    ms = jnp.mean(x * x, axis=-1, keepdims=True)
    y = x * lax.rsqrt(ms + NORM_EPS) * gain
    return y * (1.0 + scale) + shift


def _normmod_kernel(x_ref, gain_ref, sc_ref, sh_ref, o_ref):
    o_ref[...] = _normmod(x_ref[...], gain_ref[...], sc_ref[0], sh_ref[0]).astype(o_ref.dtype)


def _mod_spec(d, rows_per_batch, row, chunk):
    return pl.BlockSpec((1, 1, d), lambda i, *_: (row + i // rows_per_batch, 0, chunk))


def _normmod_call(x, gain, modr, row, sc_chunk, sh_chunk, seq, tm=512):
    t, d = x.shape
    rpb = seq // tm
    return pl.pallas_call(
        _normmod_kernel,
        grid=(t // tm,),
        in_specs=[
            pl.BlockSpec((tm, d), lambda i: (i, 0)),
            pl.BlockSpec((1, d), lambda i: (0, 0)),
            _mod_spec(d, rpb, row, sc_chunk),
            _mod_spec(d, rpb, row, sh_chunk),
        ],
        out_specs=pl.BlockSpec((tm, d), lambda i: (i, 0)),
        out_shape=jax.ShapeDtypeStruct((t, d), BF16),
        compiler_params=_params(("arbitrary",), 32),
        name="normmod",
    )(x, gain.reshape(1, d), modr, modr)


def _convin_kernel(h_ref, wb_ref, wc_ref, wv_ref, cw_ref, o_ref, u_scr, wb_scr, wc_scr, wv_scr):
    seq = h_ref.shape[0]
    for w_ref, w_scr in ((wb_ref, wb_scr), (wc_ref, wc_scr), (wv_ref, wv_scr)):
        _cast_weight_once(w_ref, w_scr)
    h = h_ref[...]
    u = _dot(h, wc_scr[...]) * _dot(h, wv_scr[...])
    u_scr[0:SUBLANES, :] = jnp.zeros((SUBLANES, u.shape[1]), F32)
    u_scr[SUBLANES:, :] = u
    cw = cw_ref[...]
    conv = cw[CONV_WIDTH - 1:CONV_WIDTH, :] * u
    for tap in range(1, CONV_WIDTH):
        conv = conv + (cw[CONV_WIDTH - 1 - tap:CONV_WIDTH - tap, :]
                       * u_scr[SUBLANES - tap:SUBLANES - tap + seq, :])
    o_ref[...] = (_dot(h, wb_scr[...]) * conv).astype(o_ref.dtype)


def _convin_call(h, w_in, conv_w, seq, tn=256):
    t, d = h.shape
    nj = d // tn
    return pl.pallas_call(
        _convin_kernel,
        grid=(nj, t // seq),
        in_specs=[
            pl.BlockSpec((seq, d), lambda j, b: (b, 0)),
            pl.BlockSpec((d, tn), lambda j, b: (0, j)),
            pl.BlockSpec((d, tn), lambda j, b: (0, j + nj)),
            pl.BlockSpec((d, tn), lambda j, b: (0, j + 2 * nj)),
            pl.BlockSpec((CONV_WIDTH, tn), lambda j, b: (0, j)),
        ],
        out_specs=pl.BlockSpec((seq, tn), lambda j, b: (b, j)),
        out_shape=jax.ShapeDtypeStruct((t, d), BF16),
        scratch_shapes=[pltpu.VMEM((seq + SUBLANES, tn), F32)] + [pltpu.VMEM((d, tn), BF16)] * 3,
        compiler_params=_params(("arbitrary", "arbitrary"), 54),
        name="conv_in",
    )(h, w_in, w_in, w_in, conv_w)


def _resmm_norm_kernel(a_ref, w_ref, x_ref, g_ref, gain_ref, sc_ref, sh_ref, o_ref, h_ref):
    xn = x_ref[...] + g_ref[0] * _dot(a_ref[...], w_ref[...])
    o_ref[...] = xn
    h_ref[...] = _normmod(xn, gain_ref[...], sc_ref[0], sh_ref[0]).astype(h_ref.dtype)


def _resmm_route_kernel(a_ref, w_ref, x_ref, g_ref, gain_ref, sc_ref, sh_ref, wr_ref, br_ref,
                        o_ref, h_ref, idx_ref, wt_ref, *, n_experts):
    xn = x_ref[...] + g_ref[0] * _dot(a_ref[...], w_ref[...])
    o_ref[...] = xn
    h = _normmod(xn, gain_ref[...], sc_ref[0], sh_ref[0])
    h_ref[...] = h
    idx_ref[...], wt_ref[...] = _route(h, wr_ref[...], br_ref[...], n_experts)


def _resmm_call(a, w, x, modr, row, g_chunk, seq, next_norm, router=None, tm=512):
    t, k = a.shape
    d = w.shape[1]
    rpb = seq // tm
    gain, nrow, sc_chunk, sh_chunk = next_norm
    in_specs = [
        pl.BlockSpec((tm, k), lambda i: (i, 0)),
        pl.BlockSpec((k, d), lambda i: (0, 0)),
        pl.BlockSpec((tm, d), lambda i: (i, 0)),
        _mod_spec(d, rpb, row, g_chunk),
        pl.BlockSpec((1, d), lambda i: (0, 0)),
        _mod_spec(d, rpb, nrow, sc_chunk),
        _mod_spec(d, rpb, nrow, sh_chunk),
    ]
    args = (a, w, x, modr, gain.reshape(1, d), modr, modr)
    row_spec = pl.BlockSpec((tm, d), lambda i: (i, 0))
    lane_spec = pl.BlockSpec((tm, LANES), lambda i: (i, 0))
    cp = _params(("arbitrary",), 56)
    if router is None:
        return pl.pallas_call(
            _resmm_norm_kernel, grid=(t // tm,), in_specs=in_specs,
            out_specs=[row_spec, row_spec],
            out_shape=[jax.ShapeDtypeStruct((t, d), F32), jax.ShapeDtypeStruct((t, d), BF16)],
            compiler_params=cp, name="res_matmul_norm")(*args)
    w_router, b_router = router
    n_experts = w_router.shape[1]
    wr = jnp.zeros((d, LANES), F32).at[:, :n_experts].set(w_router)
    br = jnp.zeros((1, LANES), F32).at[0, :n_experts].set(b_router)
    in_specs += [pl.BlockSpec((d, LANES), lambda i: (0, 0)),
                 pl.BlockSpec((1, LANES), lambda i: (0, 0))]
    return pl.pallas_call(
        functools.partial(_resmm_route_kernel, n_experts=n_experts), grid=(t // tm,),
        in_specs=in_specs, out_specs=[row_spec, row_spec, lane_spec, lane_spec],
        out_shape=[jax.ShapeDtypeStruct((t, d), F32), jax.ShapeDtypeStruct((t, d), F32),
                   jax.ShapeDtypeStruct((t, LANES), jnp.int32),
                   jax.ShapeDtypeStruct((t, LANES), F32)],
        compiler_params=cp, name="res_matmul_route")(*args, wr, br)


def _ffn_kernel(h_ref, wg_ref, wu_ref, wd_ref, x_ref, g_ref, gain_ref, sc_ref, sh_ref,
                o_ref, hn_ref):
    j = pl.program_id(1)

    h = h_ref[...]
    act = _silu(_dot(h, wg_ref[...])) * _dot(h, wu_ref[...])
    prev = jnp.where(j == 0, 0.0, o_ref[...])
    o_ref[...] = prev + _dot(act.astype(BF16), wd_ref[...])

    @pl.when(j == pl.num_programs(1) - 1)
    def _():
        xn = x_ref[...] + g_ref[0] * o_ref[...]
        o_ref[...] = xn
        hn_ref[...] = _normmod(xn, gain_ref[...], sc_ref[0], sh_ref[0]).astype(hn_ref.dtype)


def _ffn_call(h, wg, wu, wd, x, modr, row, g_chunk, seq, next_norm, tm=512, tf=512):
    t, d = h.shape
    f = wg.shape[1]
    rpb = seq // tm
    gain, nrow, sc_chunk, sh_chunk = next_norm
    row_spec = pl.BlockSpec((tm, d), lambda i, j: (i, 0))
    return pl.pallas_call(
        _ffn_kernel,
        grid=(t // tm, f // tf),
        in_specs=[
            row_spec,
            pl.BlockSpec((d, tf), lambda i, j: (0, j)),
            pl.BlockSpec((d, tf), lambda i, j: (0, j)),
            pl.BlockSpec((tf, d), lambda i, j: (j, 0)),
            row_spec,
            _mod_spec(d, rpb, row, g_chunk),
            pl.BlockSpec((1, d), lambda i, j: (0, 0)),
            _mod_spec(d, rpb, nrow, sc_chunk),
            _mod_spec(d, rpb, nrow, sh_chunk),
        ],
        out_specs=[row_spec, row_spec],
        out_shape=[jax.ShapeDtypeStruct((t, d), F32), jax.ShapeDtypeStruct((t, d), BF16)],
        compiler_params=_params(("arbitrary", "arbitrary"), 52),
        name="dense_ffn",
    )(h, wg, wu, wd, x, modr, gain.reshape(1, d), modr, modr)


def _qk_kernel(h_ref, w_ref, gain_ref, cos_ref, sin_ref, ones_ref, perm_ref, o_ref, w_scr,
               *, head_dim):
    _cast_weight_once(w_ref, w_scr)
    acc = _dot(h_ref[...], w_scr[...])
    pair = ones_ref.shape[0]
    gain = gain_ref[...]
    cos = cos_ref[...]
    sin = sin_ref[...]
    ones = ones_ref[...]
    perm = perm_ref[...]
    for p0 in range(0, acc.shape[1], pair):
        xp = acc[:, p0:p0 + pair]
        ss = _dot((xp * xp).astype(BF16), ones)
        inv = lax.rsqrt(ss * (1.0 / head_dim) + NORM_EPS)
        z = xp * gain
        rot = _dot(z.astype(BF16), perm)
        out = (z * cos + rot * sin) * inv
        o_ref[:, p0:p0 + pair] = out.astype(o_ref.dtype)


def _v_kernel(h_ref, w_ref, o_ref, w_scr):
    _cast_weight_once(w_ref, w_scr)
    o_ref[...] = _dot(h_ref[...], w_scr[...]).astype(o_ref.dtype)


def _head_pair_matrices(head_dim):
    pair = 2 * head_dim
    src = jnp.arange(pair)[:, None]
    dst = jnp.arange(pair)[None, :]
    same_head = (src // head_dim) == (dst // head_dim)
    ones = same_head.astype(BF16)
    perm = (same_head & ((src % head_dim) == ((dst + head_dim // 2) % head_dim))).astype(BF16)
    return ones, perm


def _qkv_call(h, w_qkv, which, seq, head_dim, gain=None, cos=None, sin=None, tm=2048, tn=512):
    t, d = h.shape
    nj = d // tn
    off = which * nj
    rpb = seq // tm
    h_spec = pl.BlockSpec((tm, d), lambda j, i: (i, 0))
    w_spec = pl.BlockSpec((d, tn), lambda j, i: (0, j + off))
    out_spec = pl.BlockSpec((tm, tn), lambda j, i: (i, j))
    out_shape = jax.ShapeDtypeStruct((t, d), BF16)
    cp = _params(("arbitrary", "arbitrary"), 40)
    w_scratch = [pltpu.VMEM((d, tn), BF16)]
    if gain is None:
        return pl.pallas_call(_v_kernel, grid=(nj, t // tm), in_specs=[h_spec, w_spec],
                              out_specs=out_spec, out_shape=out_shape,
                              scratch_shapes=w_scratch, compiler_params=cp,
                              name="v_proj")(h, w_qkv)
    pair = 2 * head_dim
    ones, perm = _head_pair_matrices(head_dim)
    tab_spec = pl.BlockSpec((tm, pair), lambda j, i: (i % rpb, 0))
    const_spec = pl.BlockSpec((pair, pair), lambda j, i: (0, 0))
    return pl.pallas_call(
        functools.partial(_qk_kernel, head_dim=head_dim),
        grid=(nj, t // tm),
        in_specs=[h_spec, w_spec, pl.BlockSpec((1, pair), lambda j, i: (0, 0)),
                  tab_spec, tab_spec, const_spec, const_spec],
        out_specs=out_spec, out_shape=out_shape, scratch_shapes=w_scratch, compiler_params=cp,
        name="qk_proj",
    )(h, w_qkv, jnp.tile(gain.reshape(1, head_dim), (1, 2)), cos, sin, ones, perm)


def _attn_kernel(q_ref, k_ref, v_ref, o_ref, vt_scr, s_scr):
    seq = q_ref.shape[0]
    hd = q_ref.shape[1] // ATTN_HEADS_PER_STEP
    blk = MOBA_BLOCK
    nb = seq // blk
    heads = range(ATTN_HEADS_PER_STEP)
    cols = [slice(hh * hd, (hh + 1) * hd) for hh in heads]

    km_parts = []
    for hh in heads:
        rows = []
        for n in range(nb):
            vt_scr[hh, n, 0:hd, :] = (v_ref[n * blk:(n + 1) * blk, cols[hh]]
                                      .astype(F32).T.astype(BF16))
            vt_scr[hh, n, hd:, :] = jnp.ones((vt_scr.shape[2] - hd, blk), BF16)
            kb = k_ref[n * blk:(n + 1) * blk, cols[hh]].astype(F32)
            rows.append(jnp.sum(kb, axis=0, keepdims=True) * (1.0 / blk))
        km = jnp.concatenate(rows, axis=0)
        km_a = km.astype(BF16)
        rem = km - km_a.astype(F32)
        km_b = rem.astype(BF16)
        km_c = (rem - km_b.astype(F32)).astype(BF16)
        km_parts.append(jnp.concatenate([km_a, km_b, km_c, jnp.zeros_like(km_a)], axis=0))

    causal = (lax.broadcasted_iota(jnp.int32, (blk, blk), 0)
              <= lax.broadcasted_iota(jnp.int32, (blk, blk), 1))

    def masked_scores(hh, qi):
        qb = q_ref[qi * blk:(qi + 1) * blk, cols[hh]]
        sel = None
        if qi > MOBA_TOPK:
            g3 = lax.dot_general(km_parts[hh], qb, _NT_DIMS, preferred_element_type=F32)
            gate = g3[0:nb] + g3[nb:2 * nb] + g3[2 * nb:3 * nb]
            g_row = [gate[n:n + 1, :] for n in range(qi)]
            rank = [jnp.zeros((1, blk), F32) for _ in range(qi)]
            for a in range(qi):
                for b in range(a + 1, qi):
                    a_first = (g_row[a] >= g_row[b]).astype(F32)
                    rank[b] = rank[b] + a_first
                    rank[a] = rank[a] + (1.0 - a_first)
            sel = [rank[n] < MOBA_TOPK for n in range(qi)]

        m_run = None
        for n in range(qi + 1):
            s = lax.dot_general(k_ref[n * blk:(n + 1) * blk, cols[hh]], qb, _NT_DIMS,
                                preferred_element_type=F32)
            if n == qi:
                s = jnp.where(causal, s, NEG_INF)
            elif sel is not None:
                s = jnp.where(sel[n], s, NEG_INF)
            s_scr[hh, qi % 2, n] = s
            cm = jnp.max(s, axis=0, keepdims=True)
            m_run = cm if m_run is None else jnp.maximum(m_run, cm)
        return m_run

    def weighted_values(hh, qi, m_run):
        acc = None
        for n in range(qi + 1):
            p = jnp.exp2(s_scr[hh, qi % 2, n] - m_run)
            pv = _dot(vt_scr[hh, n], p.astype(BF16))
            acc = pv if acc is None else acc + pv
        out_t = acc[0:hd] * (1.0 / acc[hd:hd + 1])
        o_ref[qi * blk:(qi + 1) * blk, cols[hh]] = out_t.T.astype(o_ref.dtype)

    m_next = [masked_scores(hh, 0) for hh in heads]
    for qi in range(nb):
        m_cur = m_next
        if qi + 1 < nb:
            m_next = [masked_scores(hh, qi + 1) for hh in heads]
        for hh in heads:
            weighted_values(hh, qi, m_cur[hh])


def _attn_call(q, k, v, seq, head_dim):
    t, d = q.shape
    width = ATTN_HEADS_PER_STEP * head_dim
    nb = seq // MOBA_BLOCK
    spec = pl.BlockSpec((seq, width), lambda b, h: (b, h))
    return pl.pallas_call(
        _attn_kernel,
        grid=(t // seq, d // width),
        in_specs=[spec, spec, spec],
        out_specs=spec,
        out_shape=jax.ShapeDtypeStruct((t, d), BF16),
        scratch_shapes=[
            pltpu.VMEM((ATTN_HEADS_PER_STEP, nb, head_dim + BF16_SUBLANES, MOBA_BLOCK), BF16),
            pltpu.VMEM((ATTN_HEADS_PER_STEP, 2, nb, MOBA_BLOCK, MOBA_BLOCK), F32)],
        compiler_params=_params(("arbitrary", "arbitrary"), 40),
        name="moba_attention",
    )(q, k, v)


def _route(h, w, bias, n_experts):
    h_a = h.astype(BF16)
    h_b = (h - h_a.astype(F32)).astype(BF16)
    w_a = w.astype(BF16)
    w_b = (w - w_a.astype(F32)).astype(BF16)
    logits = _dot(h_a, w_a) + _dot(h_b, w_a) + _dot(h_a, w_b) + bias
    lane = lax.broadcasted_iota(jnp.int32, logits.shape, 1)
    logits = jnp.where(lane < n_experts, logits, -jnp.inf)
    m1 = jnp.max(logits, axis=-1, keepdims=True)
    i1 = jnp.min(jnp.where(logits == m1, lane, LANES), axis=-1, keepdims=True)
    rest = jnp.where(lane == i1, -jnp.inf, logits)
    m2 = jnp.max(rest, axis=-1, keepdims=True)
    i2 = jnp.min(jnp.where(rest == m2, lane, LANES), axis=-1, keepdims=True)
    e2 = jnp.exp(m2 - m1)
    w1 = 1.0 / (1.0 + e2)
    w2 = e2 * w1
    return (jnp.where(lane == 0, i1, jnp.where(lane == 1, i2, 0)),
            jnp.where(lane == 0, w1, jnp.where(lane == 1, w2, 0.0)))


def _dispatch_kernel(zb_ref, nz_ref, dest_ref, h_ref, xs_ref, zbuf, sem, zsem):
    tm = h_ref.shape[0]
    zrows = zbuf.shape[0]

    @pl.when(pl.program_id(0) == 0)
    def _():
        zbuf[...] = jnp.zeros(zbuf.shape, zbuf.dtype)

        def zero_copy(k):
            row0 = pl.multiple_of(zb_ref[k] * zrows, zrows)
            return pltpu.make_async_copy(zbuf, xs_ref.at[pl.ds(row0, zrows), :], zsem)

        def start(k, c):
            zero_copy(k).start()
            return c

        def wait(k, c):
            zero_copy(k).wait()
            return c

        lax.fori_loop(0, nz_ref[0], start, 0)
        lax.fori_loop(0, nz_ref[0], wait, 0)

    def issue(r, c):
        for k in range(EXPERT_TOPK):
            d = dest_ref[0, 0, EXPERT_TOPK * r + k]
            pltpu.make_async_copy(h_ref.at[pl.ds(r, 1), :], xs_ref.at[pl.ds(d, 1), :],
                                  sem).start(priority=k % 2)
        return c

    lax.fori_loop(0, tm, issue, 0, unroll=ROW_DMA_UNROLL)
    for k in range(EXPERT_TOPK):
        pltpu.make_async_copy(h_ref, xs_ref.at[pl.ds(0, tm), :], sem).wait()


def _dispatch_call(h, dest, tables, n_rows, block_rows, tm=512):
    t, d = h.shape
    nt = t // tm
    grid_spec = pltpu.PrefetchScalarGridSpec(
        num_scalar_prefetch=2,
        grid=(nt,),
        in_specs=[
            pl.BlockSpec((1, 1, EXPERT_TOPK * tm), lambda i, zb, nz: (i, 0, 0),
                         memory_space=pltpu.SMEM),
            pl.BlockSpec((tm, d), lambda i, zb, nz: (i, 0)),
        ],
        out_specs=pl.BlockSpec(memory_space=pl.ANY),
        scratch_shapes=[pltpu.VMEM((block_rows, d), h.dtype), pltpu.SemaphoreType.DMA,
                        pltpu.SemaphoreType.DMA],
    )
    return pl.pallas_call(
        _dispatch_kernel,
        grid_spec=grid_spec,
        out_shape=jax.ShapeDtypeStruct((n_rows, d), h.dtype),
        compiler_params=_params(("arbitrary",), 32),
        name="moe_dispatch",
    )(tables["zero_blocks"], tables["n_zero"], dest.reshape(nt, 1, EXPERT_TOPK * tm), h)


BLOCK_COMPUTE, BLOCK_TAIL, BLOCK_UNUSED = 0, 1, 2


def _expert_kernel(kind_ref, xmap_ref, we_ref, tmap_ref, xs_ref, wg_ref, wu_ref, wd_ref,
                   ytail_ref, ys_ref):
    del xmap_ref, we_ref, tmap_ref
    i = pl.program_id(0)
    j = pl.program_id(1)
    kind = kind_ref[i]

    @pl.when((kind == BLOCK_UNUSED) & (j == 0))
    def _():
        ys_ref[...] = jnp.zeros(ys_ref.shape, F32)

    @pl.when((kind == BLOCK_TAIL) & (j == 0))
    def _():
        ys_ref[...] = ytail_ref[...]

    @pl.when(kind == BLOCK_COMPUTE)
    def _():
        x = xs_ref[...].astype(BF16)
        act = _silu(_dot(x, wg_ref[0])) * _dot(x, wu_ref[0])
        prev = jnp.where(j == 0, 0.0, ys_ref[...])
        ys_ref[...] = prev + _dot(act.astype(BF16), wd_ref[0])


def _expert_call(xs, tables, y_tail, wg, wu, wd, tm, tf=1024):
    n_rows, d = xs.shape
    f = wg.shape[2]
    nf = f // tf

    def wj(i, j, kind):
        return jnp.where(kind[i] == BLOCK_COMPUTE, j, nf - 1)

    grid_spec = pltpu.PrefetchScalarGridSpec(
        num_scalar_prefetch=4,
        grid=(n_rows // tm, nf),
        in_specs=[
            pl.BlockSpec((tm, d), lambda i, j, kind, xm, we, fm: (xm[i], 0)),
            pl.BlockSpec((1, d, tf), lambda i, j, kind, xm, we, fm: (we[i], 0, wj(i, j, kind))),
            pl.BlockSpec((1, d, tf), lambda i, j, kind, xm, we, fm: (we[i], 0, wj(i, j, kind))),
            pl.BlockSpec((1, tf, d), lambda i, j, kind, xm, we, fm: (we[i], wj(i, j, kind), 0)),
            pl.BlockSpec((tm, d), lambda i, j, kind, xm, we, fm: (fm[i], 0)),
        ],
        out_specs=pl.BlockSpec((tm, d), lambda i, j, kind, xm, we, fm: (i, 0)),
    )
    return pl.pallas_call(
        _expert_kernel,
        grid_spec=grid_spec,
        out_shape=jax.ShapeDtypeStruct((n_rows, d), F32),
        compiler_params=_params(("arbitrary", "arbitrary"), 60),
        name="moe_experts",
    )(tables["kind"], tables["x_map"], tables["w_expert"], tables["tail_map"],
      xs, wg, wu, wd, y_tail)


TAIL_BLOCKS = 2


def _expert_tail_kernel(tb_ref, nt_ref, *refs):
    del tb_ref
    xs_refs = refs[:TAIL_BLOCKS]
    wg_ref, wu_ref, wd_ref, ys_ref, wgb_ref, wub_ref, wdb_ref, xb_scr = refs[TAIL_BLOCKS:]
    tm = xs_refs[0].shape[0]
    e = pl.program_id(0)
    j = pl.program_id(1)

    @pl.when(j == 0)
    def _():
        for k, xs_ref in enumerate(xs_refs):
            xb_scr[k * tm:(k + 1) * tm, :] = xs_ref[...].astype(BF16)
        ys_ref[...] = jnp.zeros(ys_ref.shape, F32)

    x = xb_scr[...]
    wg = wg_ref[0].astype(BF16)
    wu = wu_ref[0].astype(BF16)
    act = _silu(_dot(x, wg)) * _dot(x, wu)
    wd = wd_ref[0].astype(BF16)
    part = _dot(act.astype(BF16), wd)
    for k in range(TAIL_BLOCKS):
        rows = slice(k * tm, (k + 1) * tm)
        ys_ref[rows, :] += jnp.where(nt_ref[e] >= TAIL_BLOCKS - k, part[rows, :], 0.0)
    wgb_ref[0] = wg
    wub_ref[0] = wu
    wdb_ref[0] = wd


def _expert_tail_call(xs, tables, wg, wu, wd, tm, tf=256):
    d = xs.shape[1]
    n_experts, _, f = wg.shape

    def xs_spec(k):
        return pl.BlockSpec((tm, d), lambda e, j, tb, nt: (tb[TAIL_BLOCKS * e + k], 0))

    w_in = lambda e, j, tb, nt: (e, 0, j)
    w_out = lambda e, j, tb, nt: (e, j, 0)
    grid_spec = pltpu.PrefetchScalarGridSpec(
        num_scalar_prefetch=2,
        grid=(n_experts, f // tf),
        in_specs=[xs_spec(k) for k in range(TAIL_BLOCKS)] + [
            pl.BlockSpec((1, d, tf), w_in),
            pl.BlockSpec((1, d, tf), w_in),
            pl.BlockSpec((1, tf, d), w_out),
        ],
        out_specs=[
            pl.BlockSpec((TAIL_BLOCKS * tm, d), lambda e, j, tb, nt: (e, 0)),
            pl.BlockSpec((1, d, tf), w_in),
            pl.BlockSpec((1, d, tf), w_in),
            pl.BlockSpec((1, tf, d), w_out),
        ],
        scratch_shapes=[pltpu.VMEM((TAIL_BLOCKS * tm, d), BF16)],
    )
    return pl.pallas_call(
        _expert_tail_kernel,
        grid_spec=grid_spec,
        out_shape=[
            jax.ShapeDtypeStruct((n_experts * TAIL_BLOCKS * tm, d), F32),
            jax.ShapeDtypeStruct(wg.shape, BF16),
            jax.ShapeDtypeStruct(wu.shape, BF16),
            jax.ShapeDtypeStruct(wd.shape, BF16),
        ],
        compiler_params=_params(("arbitrary", "arbitrary"), 58),
        name="moe_experts_tail",
    )(tables["tail_block"], tables["n_tail"], *([xs] * TAIL_BLOCKS), wg, wu, wd)


def _combine_kernel(dest_ref, dest_next_ref, ys_ref, x_ref, wt_ref, g_ref, o_ref, ybuf, sems):
    tm = x_ref.shape[0]
    i = pl.program_id(0)
    slot = i % 2

    def start_gathers(d_ref, dst_slot):
        def issue(r, c):
            for k in range(EXPERT_TOPK):
                d = d_ref[0, 0, EXPERT_TOPK * r + k]
                pltpu.make_async_copy(ys_ref.at[pl.ds(d, 1), :],
                                      ybuf.at[dst_slot, k, pl.ds(r, 1), :],
                                      sems.at[dst_slot]).start(priority=k % 2)
            return c

        lax.fori_loop(0, tm, issue, 0, unroll=ROW_DMA_UNROLL)

    @pl.when(i == 0)
    def _():
        start_gathers(dest_ref, 0)

    @pl.when(i + 1 < pl.num_programs(0))
    def _():
        start_gathers(dest_next_ref, 1 - slot)

    for k in range(EXPERT_TOPK):
        pltpu.make_async_copy(ys_ref.at[pl.ds(0, tm), :], ybuf.at[slot, k], sems.at[slot]).wait()
    wt = wt_ref[...]
    y = wt[:, 0:1] * ybuf[slot, 0]
    for k in range(1, EXPERT_TOPK):
        y = y + wt[:, k:k + 1] * ybuf[slot, k]
    o_ref[...] = x_ref[...] + g_ref[0] * y


def _combine_call(ys, dest, x, wt, modr, row, g_chunk, seq, tm=512):
    t, d = x.shape
    nt = t // tm
    rpb = seq // tm
    dest3 = dest.reshape(nt, 1, EXPERT_TOPK * tm)
    return pl.pallas_call(
        _combine_kernel,
        grid=(nt,),
        in_specs=[
            pl.BlockSpec((1, 1, EXPERT_TOPK * tm), lambda i: (i, 0, 0), memory_space=pltpu.SMEM),
            pl.BlockSpec((1, 1, EXPERT_TOPK * tm), lambda i: (jnp.minimum(i + 1, nt - 1), 0, 0),
                         memory_space=pltpu.SMEM),
            pl.BlockSpec(memory_space=pl.ANY),
            pl.BlockSpec((tm, d), lambda i: (i, 0)),
            pl.BlockSpec((tm, LANES), lambda i: (i, 0)),
            _mod_spec(d, rpb, row, g_chunk),
        ],
        out_specs=pl.BlockSpec((tm, d), lambda i: (i, 0)),
        out_shape=jax.ShapeDtypeStruct((t, d), F32),
        scratch_shapes=[pltpu.VMEM((2, EXPERT_TOPK, tm, d), F32), pltpu.SemaphoreType.DMA((2,))],
        compiler_params=_params(("arbitrary",), 40),
        name="moe_combine",
    )(dest3, dest3, ys, x, wt, modr)


def _routing_tables(top_idx, n_experts, tm, n_blocks):
    flat_e = top_idx.reshape(-1)
    onehot = (flat_e[:, None] == jnp.arange(n_experts, dtype=jnp.int32)[None, :]).astype(jnp.int32)
    csum = jnp.cumsum(onehot, axis=0)
    counts = csum[-1]
    pos = jnp.sum((csum - 1) * onehot, axis=1)
    nblk = (counts + tm - 1) // tm
    blk_end = jnp.cumsum(nblk)
    blk_start = blk_end - nblk
    dest = (blk_start[flat_e] * tm + pos).astype(jnp.int32)
    n_used = blk_end[-1]

    blocks = jnp.arange(n_blocks, dtype=jnp.int32)
    x_map = jnp.minimum(blocks, n_used - 1)
    blk_e = jnp.minimum(jnp.sum(blk_end[None, :] <= x_map[:, None], axis=1), n_experts - 1)
    from_end = blk_end[blk_e] - blocks
    is_tail = (from_end <= TAIL_BLOCKS) & (blocks < n_used)
    kind = jnp.where(blocks >= n_used, BLOCK_UNUSED,
                     jnp.where(is_tail, BLOCK_TAIL, BLOCK_COMPUTE))
    last_compute = lax.cummax(jnp.where(kind == BLOCK_COMPUTE, blocks, -1))
    w_expert = blk_e[jnp.maximum(last_compute, 0)]
    tail_map = lax.cummax(jnp.where(is_tail, TAIL_BLOCKS * (blk_e + 1) - from_end, 0))
    slot_from_end = TAIL_BLOCKS - jnp.arange(TAIL_BLOCKS, dtype=jnp.int32)
    tail_block = jnp.clip(blk_end[:, None] - slot_from_end[None, :], 0, n_blocks - 1)
    cand = jnp.concatenate([blk_end - 1, blocks])
    cand_ok = jnp.concatenate([nblk > 0, blocks >= n_used])
    zero_blocks = jnp.maximum(cand[jnp.argsort(~cand_ok, stable=True)], 0)
    i32 = lambda a: a.astype(jnp.int32)
    tables = dict(kind=i32(kind), x_map=i32(x_map), w_expert=i32(w_expert),
                  tail_map=i32(tail_map), tail_block=i32(tail_block.reshape(-1)),
                  n_tail=i32(jnp.minimum(nblk, TAIL_BLOCKS)),
                  zero_blocks=i32(zero_blocks), n_zero=i32(jnp.sum(cand_ok)).reshape(1))
    return dest, tables


def _rope_tables(seq, head_dim, scale=1.0):
    half = head_dim // 2
    inv_freq = jnp.exp(-math.log(ROPE_THETA) * jnp.arange(half, dtype=F32) / half)
    ang = jnp.arange(seq).astype(F32)[:, None] * inv_freq[None, :]
    cos, sin = jnp.cos(ang) * scale, jnp.sin(ang) * scale
    return (jnp.concatenate([cos, cos, cos, cos], axis=-1),
            jnp.concatenate([-sin, sin, -sin, sin], axis=-1))


def kernel(x, c, mod_w, mod_b, norm_mix, norm_ffn, conv_in, conv_w, conv_out, ffn_gate, ffn_up,
           ffn_down, qkv_w, q_norm, k_norm, attn_out, router_w, router_b, exp_gate, exp_up,
           exp_down):
    bsz, seq, d = x.shape
    depth = mod_w.shape[0]
    assert depth == 2, "layer 0 = short-conv + dense FFN, layer 1 = MoBA + MoE"
    t = bsz * seq
    head_dim = d // N_HEADS
    n_experts = router_w.shape[-1]

    mod = _mod_call(c, mod_w, mod_b)
    modr = mod.reshape(depth * bsz, 1, 6 * d)
    xt = x.reshape(t, d)

    row = bsz
    h = _normmod_call(xt, norm_mix[0], modr, 0, 1, 0, seq)
    bu = _convin_call(h, conv_in[0], conv_w[0], seq)
    xt, h = _resmm_call(bu, conv_out[0].astype(BF16), xt, modr, 0, 2, seq,
                        next_norm=(norm_ffn[0], 0, 4, 3))
    xt, h = _ffn_call(h, ffn_gate[0].astype(BF16), ffn_up[0].astype(BF16),
                      ffn_down[0].astype(BF16), xt, modr, 0, 5, seq,
                      next_norm=(norm_mix[1], row, 1, 0))

    w_qkv = qkv_w[0]
    cos_q, sin_q = _rope_tables(seq, head_dim, head_dim ** -0.5 * math.log2(math.e))
    cos_k, sin_k = _rope_tables(seq, head_dim)
    q = _qkv_call(h, w_qkv, 0, seq, head_dim, q_norm[0], cos_q, sin_q)
    k = _qkv_call(h, w_qkv, 1, seq, head_dim, k_norm[0], cos_k, sin_k)
    v = _qkv_call(h, w_qkv, 2, seq, head_dim)
    o = _attn_call(q, k, v, seq, head_dim)
    xt, hf, top_idx, top_w = _resmm_call(o, attn_out[0].astype(BF16), xt, modr, row, 2, seq,
                                         next_norm=(norm_ffn[1], row, 4, 3),
                                         router=(router_w[0], router_b[0]))
    n_blocks = (t * EXPERT_TOPK) // EXPERT_ROWS + n_experts
    dest, tables = _routing_tables(top_idx[:, :EXPERT_TOPK], n_experts, EXPERT_ROWS, n_blocks)
    xs = _dispatch_call(hf, dest, tables, n_blocks * EXPERT_ROWS, EXPERT_ROWS)
    y_tail, wg, wu, wd = _expert_tail_call(xs, tables, exp_gate[0], exp_up[0], exp_down[0],
                                           EXPERT_ROWS)
    ys = _expert_call(xs, tables, y_tail, wg, wu, wd, EXPERT_ROWS)
    xt = _combine_call(ys, dest, xt, top_w, modr, row, 5, seq)
    return xt.reshape(bsz, seq, d)
```

```python
import functools
import math

import jax
import jax.numpy as jnp
from jax import lax
from jax.experimental import pallas as pl
from jax.experimental.pallas import tpu as pltpu

N_HEADS = 16
CONV_WIDTH = 3
MOBA_BLOCK = 256
MOBA_TOPK = 3
ATTN_HEADS_PER_STEP = 2
ROPE_THETA = 10000.0
EXPERT_TOPK = 2
EXPERT_ROWS = 512
ROW_DMA_UNROLL = 8
NORM_EPS = 1e-6
NEG_INF = -1e30

LANES = 128
SUBLANES = 8
BF16_SUBLANES = 16
MIB = 1024 * 1024

F32 = jnp.float32
BF16 = jnp.bfloat16

_NT_DIMS = (((1,), (1,)), ((), ()))


def _params(semantics, vmem_mib):
    return pltpu.CompilerParams(dimension_semantics=semantics,
                                vmem_limit_bytes=vmem_mib * MIB)


def _dot(a, b):
    return jnp.dot(a, b, preferred_element_type=F32)


def _silu(a):
    return a * (1.0 / (1.0 + jnp.exp(-a)))


def _cast_weight_once(w_ref, w_scr):
    @pl.when(pl.program_id(1) == 0)
    def _():
        w_scr[...] = w_ref[...].astype(BF16)


def _mod_kernel(c_ref, w_ref, b_ref, o_ref):
    ca = _silu(c_ref[...])
    o_ref[0] = _dot(ca.astype(BF16), w_ref[0].astype(BF16)) + b_ref[0]


def _mod_call(c, mod_w, mod_b, tn=1024):
    depth, d, n = mod_w.shape
    bsz = c.shape[0]
    return pl.pallas_call(
        _mod_kernel,
        grid=(depth, n // tn),
        in_specs=[
            pl.BlockSpec((bsz, d), lambda l, j: (0, 0)),
            pl.BlockSpec((1, d, tn), lambda l, j: (l, 0, j)),
            pl.BlockSpec((1, 1, tn), lambda l, j: (l, 0, j)),
        ],
        out_specs=pl.BlockSpec((1, bsz, tn), lambda l, j: (l, 0, j)),
        out_shape=jax.ShapeDtypeStruct((depth, bsz, n), F32),
        compiler_params=_params(("arbitrary", "arbitrary"), 40),
        name="adaln_mod",
    )(c, mod_w, mod_b.reshape(depth, 1, n))


def _normmod(x, gain, scale, shift):
    ms = jnp.mean(x * x, axis=-1, keepdims=True)
    y = x * lax.rsqrt(ms + NORM_EPS) * gain
    return y * (1.0 + scale) + shift


def _normmod_kernel(x_ref, gain_ref, sc_ref, sh_ref, o_ref):
    o_ref[...] = _normmod(x_ref[...], gain_ref[...], sc_ref[0], sh_ref[0]).astype(o_ref.dtype)


def _mod_spec(d, rows_per_batch, row, chunk):
    return pl.BlockSpec((1, 1, d), lambda i, *_: (row + i // rows_per_batch, 0, chunk))


def _normmod_call(x, gain, modr, row, sc_chunk, sh_chunk, seq, tm=512):
    t, d = x.shape
    rpb = seq // tm
    return pl.pallas_call(
        _normmod_kernel,
        grid=(t // tm,),
        in_specs=[
            pl.BlockSpec((tm, d), lambda i: (i, 0)),
            pl.BlockSpec((1, d), lambda i: (0, 0)),
            _mod_spec(d, rpb, row, sc_chunk),
            _mod_spec(d, rpb, row, sh_chunk),
        ],
        out_specs=pl.BlockSpec((tm, d), lambda i: (i, 0)),
        out_shape=jax.ShapeDtypeStruct((t, d), BF16),
        compiler_params=_params(("arbitrary",), 32),
        name="normmod",
    )(x, gain.reshape(1, d), modr, modr)


def _convin_kernel(h_ref, wb_ref, wc_ref, wv_ref, cw_ref, o_ref, u_scr, wb_scr, wc_scr, wv_scr):
    seq = h_ref.shape[0]
    for w_ref, w_scr in ((wb_ref, wb_scr), (wc_ref, wc_scr), (wv_ref, wv_scr)):
        _cast_weight_once(w_ref, w_scr)
    h = h_ref[...]
    u = _dot(h, wc_scr[...]) * _dot(h, wv_scr[...])
    u_scr[0:SUBLANES, :] = jnp.zeros((SUBLANES, u.shape[1]), F32)
    u_scr[SUBLANES:, :] = u
    cw = cw_ref[...]
    conv = cw[CONV_WIDTH - 1:CONV_WIDTH, :] * u
    for tap in range(1, CONV_WIDTH):
        conv = conv + (cw[CONV_WIDTH - 1 - tap:CONV_WIDTH - tap, :]
                       * u_scr[SUBLANES - tap:SUBLANES - tap + seq, :])
    o_ref[...] = (_dot(h, wb_scr[...]) * conv).astype(o_ref.dtype)


def _convin_call(h, w_in, conv_w, seq, tn=256):
    t, d = h.shape
    nj = d // tn
    return pl.pallas_call(
        _convin_kernel,
        grid=(nj, t // seq),
        in_specs=[
            pl.BlockSpec((seq, d), lambda j, b: (b, 0)),
            pl.BlockSpec((d, tn), lambda j, b: (0, j)),
            pl.BlockSpec((d, tn), lambda j, b: (0, j + nj)),
            pl.BlockSpec((d, tn), lambda j, b: (0, j + 2 * nj)),
            pl.BlockSpec((CONV_WIDTH, tn), lambda j, b: (0, j)),
        ],
        out_specs=pl.BlockSpec((seq, tn), lambda j, b: (b, j)),
        out_shape=jax.ShapeDtypeStruct((t, d), BF16),
        scratch_shapes=[pltpu.VMEM((seq + SUBLANES, tn), F32)] + [pltpu.VMEM((d, tn), BF16)] * 3,
        compiler_params=_params(("arbitrary", "arbitrary"), 54),
        name="conv_in",
    )(h, w_in, w_in, w_in, conv_w)


def _resmm_norm_kernel(a_ref, w_ref, x_ref, g_ref, gain_ref, sc_ref, sh_ref, o_ref, h_ref):
    xn = x_ref[...] + g_ref[0] * _dot(a_ref[...], w_ref[...])
    o_ref[...] = xn
    h_ref[...] = _normmod(xn, gain_ref[...], sc_ref[0], sh_ref[0]).astype(h_ref.dtype)


def _resmm_route_kernel(a_ref, w_ref, x_ref, g_ref, gain_ref, sc_ref, sh_ref, wr_ref, br_ref,
                        o_ref, h_ref, idx_ref, wt_ref, *, n_experts):
    xn = x_ref[...] + g_ref[0] * _dot(a_ref[...], w_ref[...])
    o_ref[...] = xn
    h = _normmod(xn, gain_ref[...], sc_ref[0], sh_ref[0])
    h_ref[...] = h
    idx_ref[...], wt_ref[...] = _route(h, wr_ref[...], br_ref[...], n_experts)


def _resmm_call(a, w, x, modr, row, g_chunk, seq, next_norm, router=None, tm=512):
    t, k = a.shape
    d = w.shape[1]
    rpb = seq // tm
    gain, nrow, sc_chunk, sh_chunk = next_norm
    in_specs = [
        pl.BlockSpec((tm, k), lambda i: (i, 0)),
        pl.BlockSpec((k, d), lambda i: (0, 0)),
        pl.BlockSpec((tm, d), lambda i: (i, 0)),
        _mod_spec(d, rpb, row, g_chunk),
        pl.BlockSpec((1, d), lambda i: (0, 0)),
        _mod_spec(d, rpb, nrow, sc_chunk),
        _mod_spec(d, rpb, nrow, sh_chunk),
    ]
    args = (a, w, x, modr, gain.reshape(1, d), modr, modr)
    row_spec = pl.BlockSpec((tm, d), lambda i: (i, 0))
    lane_spec = pl.BlockSpec((tm, LANES), lambda i: (i, 0))
    cp = _params(("arbitrary",), 56)
    if router is None:
        return pl.pallas_call(
            _resmm_norm_kernel, grid=(t // tm,), in_specs=in_specs,
            out_specs=[row_spec, row_spec],
            out_shape=[jax.ShapeDtypeStruct((t, d), F32), jax.ShapeDtypeStruct((t, d), BF16)],
            compiler_params=cp, name="res_matmul_norm")(*args)
    w_router, b_router = router
    n_experts = w_router.shape[1]
    wr = jnp.zeros((d, LANES), F32).at[:, :n_experts].set(w_router)
    br = jnp.zeros((1, LANES), F32).at[0, :n_experts].set(b_router)
    in_specs += [pl.BlockSpec((d, LANES), lambda i: (0, 0)),
                 pl.BlockSpec((1, LANES), lambda i: (0, 0))]
    return pl.pallas_call(
        functools.partial(_resmm_route_kernel, n_experts=n_experts), grid=(t // tm,),
        in_specs=in_specs, out_specs=[row_spec, row_spec, lane_spec, lane_spec],
        out_shape=[jax.ShapeDtypeStruct((t, d), F32), jax.ShapeDtypeStruct((t, d), F32),
                   jax.ShapeDtypeStruct((t, LANES), jnp.int32),
                   jax.ShapeDtypeStruct((t, LANES), F32)],
        compiler_params=cp, name="res_matmul_route")(*args, wr, br)


def _ffn_kernel(h_ref, wg_ref, wu_ref, wd_ref, x_ref, g_ref, gain_ref, sc_ref, sh_ref,
                o_ref, hn_ref):
    j = pl.program_id(1)

    h = h_ref[...]
    act = _silu(_dot(h, wg_ref[...])) * _dot(h, wu_ref[...])
    prev = jnp.where(j == 0, 0.0, o_ref[...])
    o_ref[...] = prev + _dot(act.astype(BF16), wd_ref[...])

    @pl.when(j == pl.num_programs(1) - 1)
    def _():
        xn = x_ref[...] + g_ref[0] * o_ref[...]
        o_ref[...] = xn
        hn_ref[...] = _normmod(xn, gain_ref[...], sc_ref[0], sh_ref[0]).astype(hn_ref.dtype)


def _ffn_call(h, wg, wu, wd, x, modr, row, g_chunk, seq, next_norm, tm=512, tf=512):
    t, d = h.shape
    f = wg.shape[1]
    rpb = seq // tm
    gain, nrow, sc_chunk, sh_chunk = next_norm
    row_spec = pl.BlockSpec((tm, d), lambda i, j: (i, 0))
    return pl.pallas_call(
        _ffn_kernel,
        grid=(t // tm, f // tf),
        in_specs=[
            row_spec,
            pl.BlockSpec((d, tf), lambda i, j: (0, j)),
            pl.BlockSpec((d, tf), lambda i, j: (0, j)),
            pl.BlockSpec((tf, d), lambda i, j: (j, 0)),
            row_spec,
            _mod_spec(d, rpb, row, g_chunk),
            pl.BlockSpec((1, d), lambda i, j: (0, 0)),
            _mod_spec(d, rpb, nrow, sc_chunk),
            _mod_spec(d, rpb, nrow, sh_chunk),
        ],
        out_specs=[row_spec, row_spec],
        out_shape=[jax.ShapeDtypeStruct((t, d), F32), jax.ShapeDtypeStruct((t, d), BF16)],
        compiler_params=_params(("arbitrary", "arbitrary"), 52),
        name="dense_ffn",
    )(h, wg, wu, wd, x, modr, gain.reshape(1, d), modr, modr)


def _qk_kernel(h_ref, w_ref, gain_ref, cos_ref, sin_ref, ones_ref, perm_ref, o_ref, w_scr,
               *, head_dim):
    _cast_weight_once(w_ref, w_scr)
    acc = _dot(h_ref[...], w_scr[...])
    pair = ones_ref.shape[0]
    gain = gain_ref[...]
    cos = cos_ref[...]
    sin = sin_ref[...]
    ones = ones_ref[...]
    perm = perm_ref[...]
    for p0 in range(0, acc.shape[1], pair):
        xp = acc[:, p0:p0 + pair]
        ss = _dot((xp * xp).astype(BF16), ones)
        inv = lax.rsqrt(ss * (1.0 / head_dim) + NORM_EPS)
        z = xp * gain
        rot = _dot(z.astype(BF16), perm)
        out = (z * cos + rot * sin) * inv
        o_ref[:, p0:p0 + pair] = out.astype(o_ref.dtype)


def _v_kernel(h_ref, w_ref, o_ref, w_scr):
    _cast_weight_once(w_ref, w_scr)
    o_ref[...] = _dot(h_ref[...], w_scr[...]).astype(o_ref.dtype)


def _head_pair_matrices(head_dim):
    pair = 2 * head_dim
    src = jnp.arange(pair)[:, None]
    dst = jnp.arange(pair)[None, :]
    same_head = (src // head_dim) == (dst // head_dim)
    ones = same_head.astype(BF16)
    perm = (same_head & ((src % head_dim) == ((dst + head_dim // 2) % head_dim))).astype(BF16)
    return ones, perm


def _qkv_call(h, w_qkv, which, seq, head_dim, gain=None, cos=None, sin=None, tm=2048, tn=512):
    t, d = h.shape
    nj = d // tn
    off = which * nj
    rpb = seq // tm
    h_spec = pl.BlockSpec((tm, d), lambda j, i: (i, 0))
    w_spec = pl.BlockSpec((d, tn), lambda j, i: (0, j + off))
    out_spec = pl.BlockSpec((tm, tn), lambda j, i: (i, j))
    out_shape = jax.ShapeDtypeStruct((t, d), BF16)
    cp = _params(("arbitrary", "arbitrary"), 40)
    w_scratch = [pltpu.VMEM((d, tn), BF16)]
    if gain is None:
        return pl.pallas_call(_v_kernel, grid=(nj, t // tm), in_specs=[h_spec, w_spec],
                              out_specs=out_spec, out_shape=out_shape,
                              scratch_shapes=w_scratch, compiler_params=cp,
                              name="v_proj")(h, w_qkv)
    pair = 2 * head_dim
    ones, perm = _head_pair_matrices(head_dim)
    tab_spec = pl.BlockSpec((tm, pair), lambda j, i: (i % rpb, 0))
    const_spec = pl.BlockSpec((pair, pair), lambda j, i: (0, 0))
    return pl.pallas_call(
        functools.partial(_qk_kernel, head_dim=head_dim),
        grid=(nj, t // tm),
        in_specs=[h_spec, w_spec, pl.BlockSpec((1, pair), lambda j, i: (0, 0)),
                  tab_spec, tab_spec, const_spec, const_spec],
        out_specs=out_spec, out_shape=out_shape, scratch_shapes=w_scratch, compiler_params=cp,
        name="qk_proj",
    )(h, w_qkv, jnp.tile(gain.reshape(1, head_dim), (1, 2)), cos, sin, ones, perm)


def _attn_kernel(q_ref, k_ref, v_ref, o_ref, vt_scr, s_scr):
    seq = q_ref.shape[0]
    hd = q_ref.shape[1] // ATTN_HEADS_PER_STEP
    blk = MOBA_BLOCK
    nb = seq // blk
    heads = range(ATTN_HEADS_PER_STEP)
    cols = [slice(hh * hd, (hh + 1) * hd) for hh in heads]

    km_parts = []
    for hh in heads:
        rows = []
        for n in range(nb):
            vt_scr[hh, n, 0:hd, :] = (v_ref[n * blk:(n + 1) * blk, cols[hh]]
                                      .astype(F32).T.astype(BF16))
            vt_scr[hh, n, hd:, :] = jnp.ones((vt_scr.shape[2] - hd, blk), BF16)
            kb = k_ref[n * blk:(n + 1) * blk, cols[hh]].astype(F32)
            rows.append(jnp.sum(kb, axis=0, keepdims=True) * (1.0 / blk))
        km = jnp.concatenate(rows, axis=0)
        km_a = km.astype(BF16)
        rem = km - km_a.astype(F32)
        km_b = rem.astype(BF16)
        km_c = (rem - km_b.astype(F32)).astype(BF16)
        km_parts.append(jnp.concatenate([km_a, km_b, km_c, jnp.zeros_like(km_a)], axis=0))

    causal = (lax.broadcasted_iota(jnp.int32, (blk, blk), 0)
              <= lax.broadcasted_iota(jnp.int32, (blk, blk), 1))

    def masked_scores(hh, qi):
        qb = q_ref[qi * blk:(qi + 1) * blk, cols[hh]]
        sel = None
        if qi > MOBA_TOPK:
            g3 = lax.dot_general(km_parts[hh], qb, _NT_DIMS, preferred_element_type=F32)
            gate = g3[0:nb] + g3[nb:2 * nb] + g3[2 * nb:3 * nb]
            g_row = [gate[n:n + 1, :] for n in range(qi)]
            rank = [jnp.zeros((1, blk), F32) for _ in range(qi)]
            for a in range(qi):
                for b in range(a + 1, qi):
                    a_first = (g_row[a] >= g_row[b]).astype(F32)
                    rank[b] = rank[b] + a_first
                    rank[a] = rank[a] + (1.0 - a_first)
            sel = [rank[n] < MOBA_TOPK for n in range(qi)]

        m_run = None
        for n in range(qi + 1):
            s = lax.dot_general(k_ref[n * blk:(n + 1) * blk, cols[hh]], qb, _NT_DIMS,
                                preferred_element_type=F32)
            if n == qi:
                s = jnp.where(causal, s, NEG_INF)
            elif sel is not None:
                s = jnp.where(sel[n], s, NEG_INF)
            s_scr[hh, qi % 2, n] = s
            cm = jnp.max(s, axis=0, keepdims=True)
            m_run = cm if m_run is None else jnp.maximum(m_run, cm)
        return m_run

    def weighted_values(hh, qi, m_run):
        acc = None
        for n in range(qi + 1):
            p = jnp.exp2(s_scr[hh, qi % 2, n] - m_run)
            pv = _dot(vt_scr[hh, n], p.astype(BF16))
            acc = pv if acc is None else acc + pv
        out_t = acc[0:hd] * (1.0 / acc[hd:hd + 1])
        o_ref[qi * blk:(qi + 1) * blk, cols[hh]] = out_t.T.astype(o_ref.dtype)

    m_next = [masked_scores(hh, 0) for hh in heads]
    for qi in range(nb):
        m_cur = m_next
        if qi + 1 < nb:
            m_next = [masked_scores(hh, qi + 1) for hh in heads]
        for hh in heads:
            weighted_values(hh, qi, m_cur[hh])


def _attn_call(q, k, v, seq, head_dim):
    t, d = q.shape
    width = ATTN_HEADS_PER_STEP * head_dim
    nb = seq // MOBA_BLOCK
    spec = pl.BlockSpec((seq, width), lambda b, h: (b, h))
    return pl.pallas_call(
        _attn_kernel,
        grid=(t // seq, d // width),
        in_specs=[spec, spec, spec],
        out_specs=spec,
        out_shape=jax.ShapeDtypeStruct((t, d), BF16),
        scratch_shapes=[
            pltpu.VMEM((ATTN_HEADS_PER_STEP, nb, head_dim + BF16_SUBLANES, MOBA_BLOCK), BF16),
            pltpu.VMEM((ATTN_HEADS_PER_STEP, 2, nb, MOBA_BLOCK, MOBA_BLOCK), F32)],
        compiler_params=_params(("arbitrary", "arbitrary"), 40),
        name="moba_attention",
    )(q, k, v)


def _route(h, w, bias, n_experts):
    h_a = h.astype(BF16)
    h_b = (h - h_a.astype(F32)).astype(BF16)
    w_a = w.astype(BF16)
    w_b = (w - w_a.astype(F32)).astype(BF16)
    both = _dot(h_a, jnp.concatenate([w_a, w_b], axis=1))
    logits = both[:, :LANES] + both[:, LANES:] + _dot(h_b, w_a) + bias
    lane = lax.broadcasted_iota(jnp.int32, logits.shape, 1)
    logits = jnp.where(lane < n_experts, logits, -jnp.inf)
    m1 = jnp.max(logits, axis=-1, keepdims=True)
    i1 = jnp.min(jnp.where(logits == m1, lane, LANES), axis=-1, keepdims=True)
    rest = jnp.where(lane == i1, -jnp.inf, logits)
    m2 = jnp.max(rest, axis=-1, keepdims=True)
    i2 = jnp.min(jnp.where(rest == m2, lane, LANES), axis=-1, keepdims=True)
    e2 = jnp.exp(m2 - m1)
    w1 = 1.0 / (1.0 + e2)
    w2 = e2 * w1
    return (jnp.where(lane == 0, i1, jnp.where(lane == 1, i2, 0)),
            jnp.where(lane == 0, w1, jnp.where(lane == 1, w2, 0.0)))


def _dispatch_kernel(zb_ref, nz_ref, dest_ref, h_ref, xs_ref, zbuf, sem, zsem):
    tm = h_ref.shape[0]
    zrows = zbuf.shape[0]

    @pl.when(pl.program_id(0) == 0)
    def _():
        zbuf[...] = jnp.zeros(zbuf.shape, zbuf.dtype)

        def zero_copy(k):
            row0 = pl.multiple_of(zb_ref[k] * zrows, zrows)
            return pltpu.make_async_copy(zbuf, xs_ref.at[pl.ds(row0, zrows), :], zsem)

        def start(k, c):
            zero_copy(k).start()
            return c

        def wait(k, c):
            zero_copy(k).wait()
            return c

        lax.fori_loop(0, nz_ref[0], start, 0)
        lax.fori_loop(0, nz_ref[0], wait, 0)

    def issue(r, c):
        for k in range(EXPERT_TOPK):
            d = dest_ref[0, 0, EXPERT_TOPK * r + k]
            pltpu.make_async_copy(h_ref.at[pl.ds(r, 1), :], xs_ref.at[pl.ds(d, 1), :],
                                  sem).start(priority=k % 2)
        return c

    lax.fori_loop(0, tm, issue, 0, unroll=ROW_DMA_UNROLL)
    for k in range(EXPERT_TOPK):
        pltpu.make_async_copy(h_ref, xs_ref.at[pl.ds(0, tm), :], sem).wait()


def _dispatch_call(h, dest, tables, n_rows, block_rows, tm=512):
    t, d = h.shape
    nt = t // tm
    grid_spec = pltpu.PrefetchScalarGridSpec(
        num_scalar_prefetch=2,
        grid=(nt,),
        in_specs=[
            pl.BlockSpec((1, 1, EXPERT_TOPK * tm), lambda i, zb, nz: (i, 0, 0),
                         memory_space=pltpu.SMEM),
            pl.BlockSpec((tm, d), lambda i, zb, nz: (i, 0)),
        ],
        out_specs=pl.BlockSpec(memory_space=pl.ANY),
        scratch_shapes=[pltpu.VMEM((block_rows, d), h.dtype), pltpu.SemaphoreType.DMA,
                        pltpu.SemaphoreType.DMA],
    )
    return pl.pallas_call(
        _dispatch_kernel,
        grid_spec=grid_spec,
        out_shape=jax.ShapeDtypeStruct((n_rows, d), h.dtype),
        compiler_params=_params(("arbitrary",), 32),
        name="moe_dispatch",
    )(tables["zero_blocks"], tables["n_zero"], dest.reshape(nt, 1, EXPERT_TOPK * tm), h)


BLOCK_COMPUTE, BLOCK_TAIL, BLOCK_UNUSED = 0, 1, 2


def _expert_kernel(kind_ref, xmap_ref, we_ref, tmap_ref, xs_ref, wg_ref, wu_ref, wd_ref,
                   ytail_ref, ys_ref):
    del xmap_ref, we_ref, tmap_ref
    i = pl.program_id(0)
    j = pl.program_id(1)
    kind = kind_ref[i]

    @pl.when((kind == BLOCK_UNUSED) & (j == 0))
    def _():
        ys_ref[...] = jnp.zeros(ys_ref.shape, F32)

    @pl.when((kind == BLOCK_TAIL) & (j == 0))
    def _():
        ys_ref[...] = ytail_ref[...]

    @pl.when(kind == BLOCK_COMPUTE)
    def _():
        x = xs_ref[...].astype(BF16)
        act = _silu(_dot(x, wg_ref[0])) * _dot(x, wu_ref[0])
        prev = jnp.where(j == 0, 0.0, ys_ref[...])
        ys_ref[...] = prev + _dot(act.astype(BF16), wd_ref[0])


def _expert_call(xs, tables, y_tail, wg, wu, wd, tm, tf=1024):
    n_rows, d = xs.shape
    f = wg.shape[2]
    nf = f // tf

    def wj(i, j, kind):
        return jnp.where(kind[i] == BLOCK_COMPUTE, j, nf - 1)

    grid_spec = pltpu.PrefetchScalarGridSpec(
        num_scalar_prefetch=4,
        grid=(n_rows // tm, nf),
        in_specs=[
            pl.BlockSpec((tm, d), lambda i, j, kind, xm, we, fm: (xm[i], 0)),
            pl.BlockSpec((1, d, tf), lambda i, j, kind, xm, we, fm: (we[i], 0, wj(i, j, kind))),
            pl.BlockSpec((1, d, tf), lambda i, j, kind, xm, we, fm: (we[i], 0, wj(i, j, kind))),
            pl.BlockSpec((1, tf, d), lambda i, j, kind, xm, we, fm: (we[i], wj(i, j, kind), 0)),
            pl.BlockSpec((tm, d), lambda i, j, kind, xm, we, fm: (fm[i], 0)),
        ],
        out_specs=pl.BlockSpec((tm, d), lambda i, j, kind, xm, we, fm: (i, 0)),
    )
    return pl.pallas_call(
        _expert_kernel,
        grid_spec=grid_spec,
        out_shape=jax.ShapeDtypeStruct((n_rows, d), F32),
        compiler_params=_params(("arbitrary", "arbitrary"), 60),
        name="moe_experts",
    )(tables["kind"], tables["x_map"], tables["w_expert"], tables["tail_map"],
      xs, wg, wu, wd, y_tail)


TAIL_BLOCKS = 2


def _expert_tail_kernel(tb_ref, nt_ref, *refs):
    del tb_ref
    xs_refs = refs[:TAIL_BLOCKS]
    wg_ref, wu_ref, wd_ref, ys_ref, wgb_ref, wub_ref, wdb_ref, xb_scr = refs[TAIL_BLOCKS:]
    tm = xs_refs[0].shape[0]
    e = pl.program_id(0)
    j = pl.program_id(1)

    @pl.when(j == 0)
    def _():
        for k, xs_ref in enumerate(xs_refs):
            xb_scr[k * tm:(k + 1) * tm, :] = xs_ref[...].astype(BF16)
        ys_ref[...] = jnp.zeros(ys_ref.shape, F32)

    x = xb_scr[...]
    wg = wg_ref[0].astype(BF16)
    wu = wu_ref[0].astype(BF16)
    act = _silu(_dot(x, wg)) * _dot(x, wu)
    wd = wd_ref[0].astype(BF16)
    part = _dot(act.astype(BF16), wd)
    for k in range(TAIL_BLOCKS):
        rows = slice(k * tm, (k + 1) * tm)
        ys_ref[rows, :] += jnp.where(nt_ref[e] >= TAIL_BLOCKS - k, part[rows, :], 0.0)
    wgb_ref[0] = wg
    wub_ref[0] = wu
    wdb_ref[0] = wd


def _expert_tail_call(xs, tables, wg, wu, wd, tm, tf=256):
    d = xs.shape[1]
    n_experts, _, f = wg.shape

    def xs_spec(k):
        return pl.BlockSpec((tm, d), lambda e, j, tb, nt: (tb[TAIL_BLOCKS * e + k], 0))

    w_in = lambda e, j, tb, nt: (e, 0, j)
    w_out = lambda e, j, tb, nt: (e, j, 0)
    grid_spec = pltpu.PrefetchScalarGridSpec(
        num_scalar_prefetch=2,
        grid=(n_experts, f // tf),
        in_specs=[xs_spec(k) for k in range(TAIL_BLOCKS)] + [
            pl.BlockSpec((1, d, tf), w_in),
            pl.BlockSpec((1, d, tf), w_in),
            pl.BlockSpec((1, tf, d), w_out),
        ],
        out_specs=[
            pl.BlockSpec((TAIL_BLOCKS * tm, d), lambda e, j, tb, nt: (e, 0)),
            pl.BlockSpec((1, d, tf), w_in),
            pl.BlockSpec((1, d, tf), w_in),
            pl.BlockSpec((1, tf, d), w_out),
        ],
        scratch_shapes=[pltpu.VMEM((TAIL_BLOCKS * tm, d), BF16)],
    )
    return pl.pallas_call(
        _expert_tail_kernel,
        grid_spec=grid_spec,
        out_shape=[
            jax.ShapeDtypeStruct((n_experts * TAIL_BLOCKS * tm, d), F32),
            jax.ShapeDtypeStruct(wg.shape, BF16),
            jax.ShapeDtypeStruct(wu.shape, BF16),
            jax.ShapeDtypeStruct(wd.shape, BF16),
        ],
        compiler_params=_params(("arbitrary", "arbitrary"), 58),
        name="moe_experts_tail",
    )(tables["tail_block"], tables["n_tail"], *([xs] * TAIL_BLOCKS), wg, wu, wd)


def _combine_kernel(dest_ref, dest_next_ref, ys_ref, x_ref, wt_ref, g_ref, o_ref, ybuf, sems):
    tm = x_ref.shape[0]
    i = pl.program_id(0)
    slot = i % 2

    def start_gathers(d_ref, dst_slot):
        def issue(r, c):
            for k in range(EXPERT_TOPK):
                d = d_ref[0, 0, EXPERT_TOPK * r + k]
                pltpu.make_async_copy(ys_ref.at[pl.ds(d, 1), :],
                                      ybuf.at[dst_slot, k, pl.ds(r, 1), :],
                                      sems.at[dst_slot]).start(priority=k % 2)
            return c

        lax.fori_loop(0, tm, issue, 0, unroll=ROW_DMA_UNROLL)

    @pl.when(i == 0)
    def _():
        start_gathers(dest_ref, 0)

    @pl.when(i + 1 < pl.num_programs(0))
    def _():
        start_gathers(dest_next_ref, 1 - slot)

    for k in range(EXPERT_TOPK):
        pltpu.make_async_copy(ys_ref.at[pl.ds(0, tm), :], ybuf.at[slot, k], sems.at[slot]).wait()
    wt = wt_ref[...]
    y = wt[:, 0:1] * ybuf[slot, 0]
    for k in range(1, EXPERT_TOPK):
        y = y + wt[:, k:k + 1] * ybuf[slot, k]
    o_ref[...] = x_ref[...] + g_ref[0] * y


def _combine_call(ys, dest, x, wt, modr, row, g_chunk, seq, tm=512):
    t, d = x.shape
    nt = t // tm
    rpb = seq // tm
    dest3 = dest.reshape(nt, 1, EXPERT_TOPK * tm)
    return pl.pallas_call(
        _combine_kernel,
        grid=(nt,),
        in_specs=[
            pl.BlockSpec((1, 1, EXPERT_TOPK * tm), lambda i: (i, 0, 0), memory_space=pltpu.SMEM),
            pl.BlockSpec((1, 1, EXPERT_TOPK * tm), lambda i: (jnp.minimum(i + 1, nt - 1), 0, 0),
                         memory_space=pltpu.SMEM),
            pl.BlockSpec(memory_space=pl.ANY),
            pl.BlockSpec((tm, d), lambda i: (i, 0)),
            pl.BlockSpec((tm, LANES), lambda i: (i, 0)),
            _mod_spec(d, rpb, row, g_chunk),
        ],
        out_specs=pl.BlockSpec((tm, d), lambda i: (i, 0)),
        out_shape=jax.ShapeDtypeStruct((t, d), F32),
        scratch_shapes=[pltpu.VMEM((2, EXPERT_TOPK, tm, d), F32), pltpu.SemaphoreType.DMA((2,))],
        compiler_params=_params(("arbitrary",), 40),
        name="moe_combine",
    )(dest3, dest3, ys, x, wt, modr)


def _routing_tables(top_idx, n_experts, tm, n_blocks):
    flat_e = top_idx.reshape(-1)
    onehot = (flat_e[:, None] == jnp.arange(n_experts, dtype=jnp.int32)[None, :]).astype(jnp.int32)
    csum = jnp.cumsum(onehot, axis=0)
    counts = csum[-1]
    pos = jnp.sum((csum - 1) * onehot, axis=1)
    nblk = (counts + tm - 1) // tm
    blk_end = jnp.cumsum(nblk)
    blk_start = blk_end - nblk
    dest = (blk_start[flat_e] * tm + pos).astype(jnp.int32)
    n_used = blk_end[-1]

    blocks = jnp.arange(n_blocks, dtype=jnp.int32)
    x_map = jnp.minimum(blocks, n_used - 1)
    blk_e = jnp.minimum(jnp.sum(blk_end[None, :] <= x_map[:, None], axis=1), n_experts - 1)
    from_end = blk_end[blk_e] - blocks
    is_tail = (from_end <= TAIL_BLOCKS) & (blocks < n_used)
    kind = jnp.where(blocks >= n_used, BLOCK_UNUSED,
                     jnp.where(is_tail, BLOCK_TAIL, BLOCK_COMPUTE))
    last_compute = lax.cummax(jnp.where(kind == BLOCK_COMPUTE, blocks, -1))
    w_expert = blk_e[jnp.maximum(last_compute, 0)]
    tail_map = lax.cummax(jnp.where(is_tail, TAIL_BLOCKS * (blk_e + 1) - from_end, 0))
    slot_from_end = TAIL_BLOCKS - jnp.arange(TAIL_BLOCKS, dtype=jnp.int32)
    tail_block = jnp.clip(blk_end[:, None] - slot_from_end[None, :], 0, n_blocks - 1)
    cand = jnp.concatenate([blk_end - 1, blocks])
    cand_ok = jnp.concatenate([nblk > 0, blocks >= n_used])
    zero_blocks = jnp.maximum(cand[jnp.argsort(~cand_ok, stable=True)], 0)
    i32 = lambda a: a.astype(jnp.int32)
    tables = dict(kind=i32(kind), x_map=i32(x_map), w_expert=i32(w_expert),
                  tail_map=i32(tail_map), tail_block=i32(tail_block.reshape(-1)),
                  n_tail=i32(jnp.minimum(nblk, TAIL_BLOCKS)),
                  zero_blocks=i32(zero_blocks), n_zero=i32(jnp.sum(cand_ok)).reshape(1))
    return dest, tables


def _rope_tables(seq, head_dim, scale=1.0):
    half = head_dim // 2
    inv_freq = jnp.exp(-math.log(ROPE_THETA) * jnp.arange(half, dtype=F32) / half)
    ang = jnp.arange(seq).astype(F32)[:, None] * inv_freq[None, :]
    cos, sin = jnp.cos(ang) * scale, jnp.sin(ang) * scale
    return (jnp.concatenate([cos, cos, cos, cos], axis=-1),
            jnp.concatenate([-sin, sin, -sin, sin], axis=-1))


def kernel(x, c, mod_w, mod_b, norm_mix, norm_ffn, conv_in, conv_w, conv_out, ffn_gate, ffn_up,
           ffn_down, qkv_w, q_norm, k_norm, attn_out, router_w, router_b, exp_gate, exp_up,
           exp_down):
    bsz, seq, d = x.shape
    depth = mod_w.shape[0]
    assert depth == 2, "layer 0 = short-conv + dense FFN, layer 1 = MoBA + MoE"
    t = bsz * seq
    head_dim = d // N_HEADS
    n_experts = router_w.shape[-1]

    mod = _mod_call(c, mod_w, mod_b)
    modr = mod.reshape(depth * bsz, 1, 6 * d)
    xt = x.reshape(t, d)

    row = bsz
    h = _normmod_call(xt, norm_mix[0], modr, 0, 1, 0, seq)
    bu = _convin_call(h, conv_in[0], conv_w[0], seq)
    xt, h = _resmm_call(bu, conv_out[0].astype(BF16), xt, modr, 0, 2, seq,
                        next_norm=(norm_ffn[0], 0, 4, 3))
    xt, h = _ffn_call(h, ffn_gate[0].astype(BF16), ffn_up[0].astype(BF16),
                      ffn_down[0].astype(BF16), xt, modr, 0, 5, seq,
                      next_norm=(norm_mix[1], row, 1, 0))

    w_qkv = qkv_w[0]
    cos_q, sin_q = _rope_tables(seq, head_dim, head_dim ** -0.5 * math.log2(math.e))
    cos_k, sin_k = _rope_tables(seq, head_dim)
    q = _qkv_call(h, w_qkv, 0, seq, head_dim, q_norm[0], cos_q, sin_q)
    k = _qkv_call(h, w_qkv, 1, seq, head_dim, k_norm[0], cos_k, sin_k)
    v = _qkv_call(h, w_qkv, 2, seq, head_dim)
    o = _attn_call(q, k, v, seq, head_dim)
    xt, hf, top_idx, top_w = _resmm_call(o, attn_out[0].astype(BF16), xt, modr, row, 2, seq,
                                         next_norm=(norm_ffn[1], row, 4, 3),
                                         router=(router_w[0], router_b[0]))
    n_blocks = (t * EXPERT_TOPK) // EXPERT_ROWS + n_experts
    dest, tables = _routing_tables(top_idx[:, :EXPERT_TOPK], n_experts, EXPERT_ROWS, n_blocks)
    xs = _dispatch_call(hf, dest, tables, n_blocks * EXPERT_ROWS, EXPERT_ROWS)
    y_tail, wg, wu, wd = _expert_tail_call(xs, tables, exp_gate[0], exp_up[0], exp_down[0],
                                           EXPERT_ROWS)
    ys = _expert_call(xs, tables, y_tail, wg, wu, wd, EXPERT_ROWS)
    xt = _combine_call(ys, dest, xt, top_w, modr, row, 5, seq)
    return xt.reshape(bsz, seq, d)
```
